```python
import jax, jax.numpy as jnp
from jax import lax
import numpy as np

D_MODEL = 1024
BATCH = 4
SEQ = 8192
DEPTH = 2
DEC_BATCH = 128
DEC_SEQ = 1
PAST_LEN = 16384
PAGE_SIZE = 128

HG_HEADS = 4
HG_DK = 128
HG_DV = 128
HG_WIDTH = HG_HEADS * HG_DK
HG_VWIDTH = HG_HEADS * HG_DV
HG_CHUNK = 64
SW_Q_HEADS = 8
SW_KV_HEADS = 2
SW_HD = 64
SW_GROUP = SW_Q_HEADS // SW_KV_HEADS
WINDOW = 128
ROPE_THETA = 10000.0
IN_SIZES = (HG_WIDTH, HG_WIDTH, HG_VWIDTH, HG_VWIDTH,
            SW_Q_HEADS * SW_HD, SW_KV_HEADS * SW_HD, SW_KV_HEADS * SW_HD,
            D_MODEL, D_MODEL)
IN_WIDTH = 2 * HG_WIDTH + 2 * HG_VWIDTH + (SW_Q_HEADS + 2 * SW_KV_HEADS) * SW_HD + 2 * D_MODEL
D_FF_DENSE = 2816
N_EXPERTS = 8
TOP_K = 2
D_FF_EXPERT = 3584
N_DENSE = (DEPTH + 1) // 2
N_MOE = DEPTH // 2
RMS_EPS = 1e-6

kernel_name = "hgrn2_swa_sink_hybrid_step"


def rmsnorm(x, w):
    xf = x.astype(jnp.float32)
    r = xf * lax.rsqrt(jnp.mean(xf * xf, axis=-1, keepdims=True) + RMS_EPS)
    return (r * w.astype(jnp.float32)).astype(x.dtype)


def rope(x, pos):
    half = x.shape[-1] // 2
    inv = ROPE_THETA ** (-jnp.arange(half, dtype=jnp.float32) / half)
    ang = pos.astype(jnp.float32)[:, None] * inv[None, :]
    cos = jnp.cos(ang)[:, None, :]
    sin = jnp.sin(ang)[:, None, :]
    xf = x.astype(jnp.float32)
    x1, x2 = xf[..., :half], xf[..., half:]
    return jnp.concatenate([x1 * cos - x2 * sin, x2 * cos + x1 * sin], axis=-1).astype(x.dtype)


def hgrn_chunk(S, qkgv):
    q, k, g, v = qkgv
    C = q.shape[2]
    b = jnp.cumsum(g, axis=2)
    causal = jnp.tril(jnp.ones((C, C), dtype=bool))
    diff = b[:, :, :, None, :] - b[:, :, None, :, :]
    decay = jnp.exp(jnp.where(causal[:, :, None], diff, -jnp.inf))
    A = jnp.einsum('bhtsd,bhsd->bhts', q[:, :, :, None, :] * decay, k)
    o = (jnp.einsum('bhts,bhsv->bhtv', A, v)
         + jnp.einsum('bhtd,bhdv->bhtv', q * jnp.exp(b), S))
    b_last = b[:, :, -1:, :]
    S_new = (jnp.exp(b_last[:, :, 0, :])[..., None] * S
             + jnp.einsum('bhsd,bhsv->bhdv', k * jnp.exp(b_last - b), v))
    return S_new, o


def hgrn_scan(S0, q, k, g, v, chunk):
    B, L, H, _ = q.shape
    n = L // chunk

    def to_chunks(t):
        return t.reshape(B, n, chunk, H, t.shape[-1]).transpose(1, 0, 3, 2, 4)

    S, o = lax.scan(hgrn_chunk, S0, (to_chunks(q), to_chunks(k), to_chunks(g), to_chunks(v)))
    return S, o.transpose(1, 0, 3, 2, 4).reshape(B, L, H, HG_DV)


def hgrn2_branch(hq, hf, hi, hgate, lb, norm_w, S0, chunk):
    B, L, _ = hq.shape
    q = jax.nn.silu(hq.astype(jnp.float32)) * (HG_DK ** -0.5)
    fgate = lb + (1.0 - lb) * jax.nn.sigmoid(hf.astype(jnp.float32))
    g = jnp.log(fgate)
    k = 1.0 - fgate
    v = hi.astype(jnp.float32)
    shp_k = (B, L, HG_HEADS, HG_DK)
    shp_v = (B, L, HG_HEADS, HG_DV)
    S, o = hgrn_scan(S0, q.reshape(shp_k), k.reshape(shp_k), g.reshape(shp_k), v.reshape(shp_v), chunk)
    o = rmsnorm(o, norm_w) * jax.nn.silu(hgate.astype(jnp.float32).reshape(shp_v))
    return o.reshape(B, L, HG_VWIDTH).astype(hq.dtype), S


def sink_attend(scores, sinks, vv, eq):
    s = sinks.astype(jnp.float32).reshape(SW_KV_HEADS, SW_GROUP, 1, 1)
    m = jnp.maximum(jnp.max(scores, axis=-1, keepdims=True), s)
    p = jnp.exp(scores - m)
    denom = jnp.sum(p, axis=-1, keepdims=True) + jnp.exp(s - m)
    return jnp.einsum(eq, (p / denom).astype(vv.dtype), vv)


def swa_banded(q, k, v, sinks):
    B, L = q.shape[:2]
    nb = L // WINDOW
    qb = q.reshape(B, nb, WINDOW, SW_KV_HEADS, SW_GROUP, SW_HD)
    kb = k.reshape(B, nb, WINDOW, SW_KV_HEADS, SW_HD)
    vb = v.reshape(B, nb, WINDOW, SW_KV_HEADS, SW_HD)

    def shift(t):
        return jnp.concatenate([jnp.zeros_like(t[:, :1]), t[:, :-1]], axis=1)

    kk = jnp.concatenate([shift(kb), kb], axis=2)
    vv = jnp.concatenate([shift(vb), vb], axis=2)
    scores = jnp.einsum('bnqhgd,bnkhd->bnhgqk', qb, kk,
                        preferred_element_type=jnp.float32) * (SW_HD ** -0.5)
    qi = jnp.arange(WINDOW)[:, None] + WINDOW
    ki = jnp.arange(2 * WINDOW)[None, :]
    rel = qi - ki
    band = (rel >= 0) & (rel <= WINDOW)
    first = (jnp.arange(nb) == 0)[:, None, None] & (ki < WINDOW)[None]
    valid = band[None] & ~first
    scores = jnp.where(valid[None, :, None, None], scores, -jnp.inf)
    o = sink_attend(scores, sinks, vv, 'bnhgqk,bnkhd->bnqhgd')
    return o.reshape(B, L, SW_Q_HEADS, SW_HD)


def swa_with_buffer(q, kk, vv, sinks, qpos, kpos):
    B, T = q.shape[:2]
    qg = q.reshape(B, T, SW_KV_HEADS, SW_GROUP, SW_HD)
    scores = jnp.einsum('bqhgd,bkhd->bhgqk', qg, kk,
                        preferred_element_type=jnp.float32) * (SW_HD ** -0.5)
    rel = qpos[:, None] - kpos[None, :]
    valid = (rel >= 0) & (rel <= WINDOW)
    scores = jnp.where(valid[None, None, None], scores, -jnp.inf)
    o = sink_attend(scores, sinks, vv, 'bhgqk,bkhd->bqhgd')
    return o.reshape(B, T, SW_Q_HEADS, SW_HD)


def token_mixer(h, pos, w_in, lb, hg_norm, sinks, w_branch_a, w_branch_b, w_out,
                S0, k_buf, v_buf, chunk):
    B, L, _ = h.shape
    z = h @ w_in
    split_idx = [int(c) for c in np.cumsum(IN_SIZES)[:-1]]
    hq, hf, hi, hgate, sq, sk, sv, ga, gb = jnp.split(z, split_idx, axis=-1)
    o_a, S_new = hgrn2_branch(hq, hf, hi, hgate, lb, hg_norm, S0, chunk)
    q = rope(sq.reshape(B, L, SW_Q_HEADS, SW_HD), pos)
    k = rope(sk.reshape(B, L, SW_KV_HEADS, SW_HD), pos)
    v = sv.reshape(B, L, SW_KV_HEADS, SW_HD)
    if k_buf is None:
        o_b = swa_banded(q, k, v, sinks)
        wb = min(WINDOW, L)
        k_rows, v_rows = k[:, L - wb:], v[:, L - wb:]
    else:
        wb = k_buf.shape[1]
        kk = jnp.concatenate([k_buf, k], axis=1)
        vv = jnp.concatenate([v_buf, v], axis=1)
        kpos = jnp.concatenate([pos[0] - wb + jnp.arange(wb), pos])
        o_b = swa_with_buffer(q, kk, vv, sinks, pos, kpos)
        k_rows, v_rows = kk[:, -wb:], vv[:, -wb:]
    o_b = o_b.reshape(B, L, SW_Q_HEADS * SW_HD)
    merged = jax.nn.sigmoid(ga) * (o_a @ w_branch_a) + jax.nn.sigmoid(gb) * (o_b @ w_branch_b)
    return merged @ w_out, S_new, k_rows, v_rows


def swiglu(x, wg, wu, wd):
    return (jax.nn.silu(x @ wg) * (x @ wu)) @ wd


def moe_ffn(x, w_router, wg, wu, wd):
    B, L, D = x.shape
    xf = x.reshape(B * L, D)
    logits = (xf @ w_router).astype(jnp.float32)
    top_v, top_i = lax.top_k(logits, TOP_K)
    wts = jax.nn.softmax(top_v, axis=-1)
    gates = jnp.sum(jax.nn.one_hot(top_i, N_EXPERTS, dtype=jnp.float32) * wts[..., None], axis=-2)
    out = jnp.zeros_like(xf)
    for e in range(N_EXPERTS):
        out = out + gates[:, e:e + 1].astype(x.dtype) * swiglu(xf, wg[e], wu[e], wd[e])
    return out.reshape(B, L, D)


def setup_inputs(seed: int = 0) -> dict:
    key = jax.random.key(seed)
    ks = jax.random.split(key, 24)
    f32 = jnp.float32

    def nrm(k, shape, scale):
        return jax.random.normal(k, shape, f32) * scale

    wb = min(WINDOW, PAST_LEN)
    return {
        "x_prompt": nrm(ks[0], (BATCH, SEQ, D_MODEL), 1.0),
        "x_sample": nrm(ks[1], (DEC_BATCH, DEC_SEQ, D_MODEL), 1.0),
        "state_hgrn": nrm(ks[2], (DEPTH, DEC_BATCH, HG_HEADS, HG_DK, HG_DV), 0.5),
        "cache_swa_k": nrm(ks[3], (DEPTH, DEC_BATCH, wb, SW_KV_HEADS, SW_HD), 1.0),
        "cache_swa_v": nrm(ks[4], (DEPTH, DEC_BATCH, wb, SW_KV_HEADS, SW_HD), 1.0),
        "norm_mix": 1.0 + nrm(ks[5], (DEPTH, D_MODEL), 0.02),
        "w_in": nrm(ks[6], (DEPTH, D_MODEL, IN_WIDTH), D_MODEL ** -0.5),
        "hg_lb_logits": nrm(ks[7], (DEPTH, HG_WIDTH), 0.5),
        "hg_norm": 1.0 + nrm(ks[8], (DEPTH, HG_DV), 0.02),
        "swa_sinks": nrm(ks[9], (DEPTH, SW_Q_HEADS), 0.5),
        "w_branch_a": nrm(ks[10], (DEPTH, HG_VWIDTH, D_MODEL), HG_VWIDTH ** -0.5),
        "w_branch_b": nrm(ks[11], (DEPTH, SW_Q_HEADS * SW_HD, D_MODEL), (SW_Q_HEADS * SW_HD) ** -0.5),
        "w_out": nrm(ks[12], (DEPTH, D_MODEL, D_MODEL), D_MODEL ** -0.5),
        "norm_ffn": 1.0 + nrm(ks[13], (DEPTH, D_MODEL), 0.02),
        "w_gate_dense": nrm(ks[14], (N_DENSE, D_MODEL, D_FF_DENSE), D_MODEL ** -0.5),
        "w_up_dense": nrm(ks[15], (N_DENSE, D_MODEL, D_FF_DENSE), D_MODEL ** -0.5),
        "w_down_dense": nrm(ks[16], (N_DENSE, D_FF_DENSE, D_MODEL), D_FF_DENSE ** -0.5),
        "w_router": nrm(ks[17], (N_MOE, D_MODEL, N_EXPERTS), D_MODEL ** -0.5),
        "w_gate_moe": nrm(ks[18], (N_MOE, N_EXPERTS, D_MODEL, D_FF_EXPERT), D_MODEL ** -0.5),
        "w_up_moe": nrm(ks[19], (N_MOE, N_EXPERTS, D_MODEL, D_FF_EXPERT), D_MODEL ** -0.5),
        "w_down_moe": nrm(ks[20], (N_MOE, N_EXPERTS, D_FF_EXPERT, D_MODEL), D_FF_EXPERT ** -0.5),
        "norm_final": 1.0 + nrm(ks[21], (D_MODEL,), 0.02),
    }


def reference(x_prompt, x_sample, state_hgrn, cache_swa_k, cache_swa_v,
              norm_mix, w_in, hg_lb_logits, hg_norm, swa_sinks, w_branch_a, w_branch_b, w_out,
              norm_ffn, w_gate_dense, w_up_dense, w_down_dense,
              w_router, w_gate_moe, w_up_moe, w_down_moe, norm_final):
    pos_p = jnp.arange(x_prompt.shape[1])
    pos_s = PAST_LEN + jnp.arange(x_sample.shape[1])
    lb_sm = jax.nn.softmax(hg_lb_logits.astype(jnp.float32), axis=0)
    lb_all = jnp.cumsum(lb_sm, axis=0) - lb_sm[0]

    xp, xs = x_prompt, x_sample
    S_p_list, kp_list, vp_list, S_s_list, ks_list, vs_list = [], [], [], [], [], []
    for l in range(DEPTH):
        mix_args = (w_in[l], lb_all[l], hg_norm[l], swa_sinks[l], w_branch_a[l], w_branch_b[l], w_out[l])
        S0_p = jnp.zeros((xp.shape[0], HG_HEADS, HG_DK, HG_DV), jnp.float32)
        m_p, S_p, k_p, v_p = token_mixer(rmsnorm(xp, norm_mix[l]), pos_p, *mix_args,
                                         S0_p, None, None, HG_CHUNK)
        xp = xp + m_p
        m_s, S_s, k_s, v_s = token_mixer(rmsnorm(xs, norm_mix[l]), pos_s, *mix_args,
                                         state_hgrn[l].astype(jnp.float32), cache_swa_k[l], cache_swa_v[l],
                                         xs.shape[1])
        xs = xs + m_s
        hp = rmsnorm(xp, norm_ffn[l])
        hs = rmsnorm(xs, norm_ffn[l])
        j = l // 2
        if l % 2 == 0:
            xp = xp + swiglu(hp, w_gate_dense[j], w_up_dense[j], w_down_dense[j])
            xs = xs + swiglu(hs, w_gate_dense[j], w_up_dense[j], w_down_dense[j])
        else:
            xp = xp + moe_ffn(hp, w_router[j], w_gate_moe[j], w_up_moe[j], w_down_moe[j])
            xs = xs + moe_ffn(hs, w_router[j], w_gate_moe[j], w_up_moe[j], w_down_moe[j])
        S_p_list.append(S_p.astype(x_prompt.dtype))
        kp_list.append(k_p)
        vp_list.append(v_p)
        S_s_list.append(S_s.astype(state_hgrn.dtype))
        ks_list.append(k_s)
        vs_list.append(v_s)

    y_prompt = rmsnorm(xp, norm_final)
    y_sample = rmsnorm(xs, norm_final)
    new_state_hgrn_prompt = jnp.stack(S_p_list, axis=0)
    new_cache_swa_k_prompt = jnp.stack(kp_list, axis=0)
    new_cache_swa_v_prompt = jnp.stack(vp_list, axis=0)
    new_state_hgrn_sample = jnp.stack(S_s_list, axis=0)
    new_cache_swa_k_sample = jnp.stack(ks_list, axis=0)
    new_cache_swa_v_sample = jnp.stack(vs_list, axis=0)
    return (y_prompt, y_sample, new_state_hgrn_prompt, new_cache_swa_k_prompt, new_cache_swa_v_prompt,
            new_state_hgrn_sample, new_cache_swa_k_sample, new_cache_swa_v_sample)
```

```python
import functools

import jax
import jax.numpy as jnp
from jax import lax
from jax.experimental import pallas as pl
from jax.experimental.pallas import tpu as pltpu

F32 = jnp.float32
BF16 = jnp.bfloat16

D_MODEL = 1024
PAST_LEN = 16384
HG_HEADS = 4
HG_D = 128
HG_W = HG_HEADS * HG_D
SW_Q_HEADS = 8
SW_KV_HEADS = 2
SW_HD = 64
SW_GROUP = SW_Q_HEADS // SW_KV_HEADS
WINDOW = 128
ROPE_THETA = 10000.0
D_FF_DENSE = 2816
N_EXPERTS = 8
D_FF_EXPERT = 3584
RMS_EPS = 1e-6

LANES = 128
SUBLANES = 8
VMEM_LIMIT = 56 * 1024 * 1024

SWQ_W = SW_Q_HEADS * LANES
KV_W = SW_KV_HEADS * SW_HD
Z_WIDTH = 4 * HG_W + SWQ_W + 2 * D_MODEL + 2 * KV_W
ZB_Q, ZB_F, ZB_I, ZB_G = 0, 1, 2, 3
ZB_SWQ, ZB_GA, ZB_GB = 2, 3, 4
ZB_K, ZB_V = 40, 41

HG_CHUNK = 128
MOE_ROWS = 128
DEC_TILE = 8


def _dot_dims(a, b, dims):
    precision = lax.Precision.HIGHEST if a.dtype == F32 else None
    return lax.dot_general(a, b, (dims, ((), ())), precision=precision, preferred_element_type=F32)


def _dot(a, b):
    return _dot_dims(a, b, ((1,), (0,)))


def _dot_nt(a, b):
    return _dot_dims(a, b, ((1,), (1,)))


def _dot_tn(a, b):
    return _dot_dims(a, b, ((0,), (0,)))


def _sigmoid(x):
    return 1.0 / (1.0 + jnp.exp(-x))


def _rms(x, w):
    ms = jnp.mean(x * x, axis=-1, keepdims=True)
    return x * lax.rsqrt(ms + RMS_EPS) * w


def _params(*sem):
    return pltpu.CompilerParams(dimension_semantics=sem, vmem_limit_bytes=VMEM_LIMIT)


def _proj_body(x_ref, nw_ref, w_ref, z_ref, h_scr):
    @pl.when(pl.program_id(1) == 0)
    def _():
        h_scr[...] = _rms(x_ref[...], nw_ref[...]).astype(h_scr.dtype)

    z_ref[...] = _dot(h_scr[...], w_ref[...]).astype(z_ref.dtype)


def _proj(x2d, nw, w, tm, tn):
    m = x2d.shape[0]
    n = w.shape[1]
    return pl.pallas_call(
        _proj_body,
        grid=(m // tm, n // tn),
        in_specs=[
            pl.BlockSpec((tm, D_MODEL), lambda i, j: (i, 0)),
            pl.BlockSpec((1, D_MODEL), lambda i, j: (0, 0)),
            pl.BlockSpec((D_MODEL, tn), lambda i, j: (0, j)),
        ],
        out_specs=pl.BlockSpec((tm, tn), lambda i, j: (i, j)),
        out_shape=jax.ShapeDtypeStruct((m, n), w.dtype),
        scratch_shapes=[pltpu.VMEM((tm, D_MODEL), w.dtype)],
        compiler_params=_params("arbitrary", "arbitrary"),
        name="proj",
    )(x2d, nw, w)


def _lower_bound(lbl, layer):
    mx = jnp.max(lbl, axis=0, keepdims=True)
    e = jnp.exp(lbl - mx)
    sm = e / jnp.sum(e, axis=0, keepdims=True)
    cum = sm[0:1, :]
    for i in range(1, layer + 1):
        cum = cum + sm[i:i + 1, :]
    return cum - sm[0:1, :]


def _split3(x):
    hi = x.astype(BF16)
    r = x - hi.astype(F32)
    mid = r.astype(BF16)
    lo = (r - mid.astype(F32)).astype(BF16)
    return hi, mid, lo


def _hgrn_prompt_body(zq_ref, zf_ref, zi_ref, zg_ref, lbl_ref, nw_ref, o_ref, s_ref, st_scr, b_scr, *, layer):
    c = pl.program_id(1)
    C = HG_CHUNK

    @pl.when(c == 0)
    def _():
        st_scr[...] = jnp.zeros_like(st_scr)

    lb = _lower_bound(lbl_ref[...], layer)
    row = lax.broadcasted_iota(jnp.int32, (C, C), 0)
    col = lax.broadcasted_iota(jnp.int32, (C, C), 1)
    xr = row ^ col
    tri = jnp.where(row >= col, 1.0, 0.0).astype(BF16)
    sub4 = (lax.broadcasted_iota(jnp.int32, (SUBLANES, HG_D), 0) & 4) == 0

    hf = zf_ref[...].astype(F32)
    fg_all = lb + (1.0 - lb) * _sigmoid(hf)
    g_all = jnp.log(fg_all)
    k_all = 1.0 - fg_all
    g1, g2, g3 = _split3(g_all)
    b_all = _dot(tri, g1) + _dot(tri, g2) + _dot(tri, g3)
    hq = zq_ref[...].astype(F32)
    q_all = hq * _sigmoid(hq) * (HG_D ** -0.5)
    gate = zg_ref[...].astype(F32)
    gate_all = gate * _sigmoid(gate)
    nw = nw_ref[...]

    for h in range(HG_HEADS):
        sl = slice(h * HG_D, (h + 1) * HG_D)
        b = b_all[:, sl]
        q = q_all[:, sl]
        k = k_all[:, sl]
        f = fg_all[:, sl]
        v = zi_ref[:, sl]
        b_scr[...] = b

        acc = _dot_nt(q.astype(BF16), k.astype(BF16))
        for m in (1, 2, 4, 8, 16, 32, 64):
            if m == 1:
                qt, kt = q * f, k
            else:
                if m == 2:
                    pieces = []
                    for j in range(C // SUBLANES):
                        lo = jnp.broadcast_to(b_scr[SUBLANES * j + 1:SUBLANES * j + 2, :], (SUBLANES, HG_D))
                        hi = jnp.broadcast_to(b_scr[SUBLANES * j + 5:SUBLANES * j + 6, :], (SUBLANES, HG_D))
                        pieces.append(jnp.where(sub4, lo, hi))
                else:
                    pieces = [jnp.broadcast_to(b_scr[i * 2 * m + m - 1:i * 2 * m + m, :], (2 * m, HG_D))
                              for i in range(C // (2 * m))]
                bref = pieces[0] if len(pieces) == 1 else jnp.concatenate(pieces, axis=0)
                e = jnp.exp(-jnp.abs(b - bref))
                qt, kt = q * e, k * e
            r = _dot_nt(qt.astype(BF16), kt.astype(BF16))
            acc = jnp.where((xr >= m) & (xr < 2 * m), r, acc)
        a = jnp.where(row >= col, acc, 0.0)

        st = st_scr[h]
        o = _dot(a.astype(BF16), v) + _dot_nt((q * jnp.exp(b)).astype(BF16), st.astype(BF16))
        b_last = b[C - 1:C, :]
        kd = k * jnp.exp(b_last - b)
        st_scr[h] = jnp.exp(b_last) * st + _dot_tn(v, kd.astype(BF16))

        o = _rms(o, nw) * gate_all[:, sl]
        o_ref[:, sl] = o.astype(o_ref.dtype)

    @pl.when(c == pl.num_programs(1) - 1)
    def _():
        for h in range(HG_HEADS):
            s_ref[h] = st_scr[h].T


def _hgrn_prompt(z3, lbl, nw, layer):
    bsz, seq, _ = z3.shape
    C = HG_CHUNK

    def zspec(blk):
        return pl.BlockSpec((None, C, HG_W), lambda b, c: (b, c, blk))

    return pl.pallas_call(
        functools.partial(_hgrn_prompt_body, layer=layer),
        grid=(bsz, seq // C),
        in_specs=[zspec(ZB_Q), zspec(ZB_F), zspec(ZB_I), zspec(ZB_G),
                  pl.BlockSpec(lbl.shape, lambda b, c: (0, 0)),
                  pl.BlockSpec((1, HG_D), lambda b, c: (0, 0))],
        out_specs=[pl.BlockSpec((None, C, HG_W), lambda b, c: (b, c, 0)),
                   pl.BlockSpec((None, HG_HEADS, HG_D, HG_D), lambda b, c: (b, 0, 0, 0))],
        out_shape=[jax.ShapeDtypeStruct((bsz, seq, HG_W), BF16),
                   jax.ShapeDtypeStruct((bsz, HG_HEADS, HG_D, HG_D), F32)],
        scratch_shapes=[pltpu.VMEM((HG_HEADS, HG_D, HG_D), F32), pltpu.VMEM((C, HG_D), F32)],
        compiler_params=_params("arbitrary", "arbitrary"),
        name="hgrn_prompt",
    )(z3, z3, z3, z3, lbl, nw)


def _hgrn_decode_body(zq_ref, zf_ref, zi_ref, zg_ref, lbl_ref, nw_ref, s_ref, o_ref, sn_ref, o_scr, *, layer):
    lb = _lower_bound(lbl_ref[...], layer)
    fg = lb + (1.0 - lb) * _sigmoid(zf_ref[...].astype(F32))
    kk = 1.0 - fg
    hq = zq_ref[...].astype(F32)
    q = hq * _sigmoid(hq) * (HG_D ** -0.5)
    v = zi_ref[...].astype(F32)
    eye = (lax.broadcasted_iota(jnp.int32, (HG_D, HG_D), 0)
           == lax.broadcasted_iota(jnp.int32, (HG_D, HG_D), 1))

    def column(r):
        return jnp.sum(jnp.where(eye, jnp.broadcast_to(r, (HG_D, HG_D)), 0.0), axis=1, keepdims=True)

    for i in range(DEC_TILE):
        for h in range(HG_HEADS):
            sl = slice(h * HG_D, (h + 1) * HG_D)
            sn = column(fg[i:i + 1, sl]) * s_ref[i, h] + column(kk[i:i + 1, sl]) * v[i:i + 1, sl]
            sn_ref[i, h] = sn
            o_scr[i:i + 1, sl] = jnp.sum(column(q[i:i + 1, sl]) * sn, axis=0, keepdims=True)

    gate = zg_ref[...].astype(F32)
    gate = gate * _sigmoid(gate)
    nw = nw_ref[...]
    for h in range(HG_HEADS):
        sl = slice(h * HG_D, (h + 1) * HG_D)
        o_ref[:, sl] = (_rms(o_scr[:, sl], nw) * gate[:, sl]).astype(o_ref.dtype)


def _hgrn_decode(z2, state, lbl, nw, layer):
    n = z2.shape[0]
    T = DEC_TILE

    def zspec(blk):
        return pl.BlockSpec((T, HG_W), lambda i: (i, blk))

    sspec = pl.BlockSpec((T, HG_HEADS, HG_D, HG_D), lambda i: (i, 0, 0, 0))
    return pl.pallas_call(
        functools.partial(_hgrn_decode_body, layer=layer),
        grid=(n // T,),
        in_specs=[zspec(ZB_Q), zspec(ZB_F), zspec(ZB_I), zspec(ZB_G),
                  pl.BlockSpec(lbl.shape, lambda i: (0, 0)),
                  pl.BlockSpec((1, HG_D), lambda i: (0, 0)),
                  sspec],
        out_specs=[pl.BlockSpec((T, HG_W), lambda i: (i, 0)), sspec],
        out_shape=[jax.ShapeDtypeStruct((n, HG_W), z2.dtype),
                   jax.ShapeDtypeStruct(state.shape, F32)],
        scratch_shapes=[pltpu.VMEM((T, HG_W), F32)],
        compiler_params=_params("arbitrary"),
        name="hgrn_decode",
    )(z2, z2, z2, z2, lbl, nw, state)


def _rope(x, cos, sin, perm):
    return x.astype(F32) * cos + _dot(x, perm) * sin


def _swa_prompt_body(sinks_ref, zq_ref, zk_ref, zv_ref, cos_ref, sin_ref, perm_ref,
                     o_ref, kr_ref, vr_ref, kprev, vprev):
    n = pl.program_id(1)
    W = WINDOW

    @pl.when(n == 0)
    def _():
        kprev[...] = jnp.zeros_like(kprev)
        vprev[...] = jnp.zeros_like(vprev)

    cos = cos_ref[...]
    sin = sin_ref[...]
    perm = perm_ref[...]
    k_rot = _rope(zk_ref[...], cos, sin, perm)
    kr_ref[...] = k_rot
    vr_ref[...] = zv_ref[...].astype(F32)
    k_cur = k_rot.astype(BF16)
    v_cur = zv_ref[...]
    kk = jnp.concatenate([kprev[...], k_cur], axis=0)
    vv = jnp.concatenate([vprev[...], v_cur], axis=0)

    qi = lax.broadcasted_iota(jnp.int32, (W, 2 * W), 0)
    kj = lax.broadcasted_iota(jnp.int32, (W, 2 * W), 1)
    valid = (kj >= qi) & (kj <= qi + W) & ((n > 0) | (kj >= W))

    for h in range(SW_Q_HEADS):
        sl = slice(h * LANES, (h + 1) * LANES)
        qh = _rope(zq_ref[:, sl], cos, sin, perm) * (SW_HD ** -0.5)
        s = jnp.where(valid, _dot_nt(qh.astype(BF16), kk), -jnp.inf)
        sink = sinks_ref[h]
        mx = jnp.maximum(jnp.max(s, axis=-1, keepdims=True), sink)
        p = jnp.exp(s - mx)
        den = jnp.sum(p, axis=-1, keepdims=True) + jnp.exp(sink - mx)
        o_ref[:, sl] = (_dot(p.astype(BF16), vv) / den).astype(o_ref.dtype)

    kprev[...] = k_cur
    vprev[...] = v_cur


def _swa_prompt(z3, sinks, cos, sin, perm):
    bsz, seq, _ = z3.shape
    W = WINDOW
    grid_spec = pltpu.PrefetchScalarGridSpec(
        num_scalar_prefetch=1,
        grid=(bsz, seq // W),
        in_specs=[pl.BlockSpec((None, W, SWQ_W), lambda b, n, s: (b, n, ZB_SWQ)),
                  pl.BlockSpec((None, W, KV_W), lambda b, n, s: (b, n, ZB_K)),
                  pl.BlockSpec((None, W, KV_W), lambda b, n, s: (b, n, ZB_V)),
                  pl.BlockSpec((W, LANES), lambda b, n, s: (n, 0)),
                  pl.BlockSpec((W, LANES), lambda b, n, s: (n, 0)),
                  pl.BlockSpec((LANES, LANES), lambda b, n, s: (0, 0))],
        out_specs=[pl.BlockSpec((None, W, SWQ_W), lambda b, n, s: (b, n, 0)),
                   pl.BlockSpec((None, W, KV_W), lambda b, n, s: (b, 0, 0)),
                   pl.BlockSpec((None, W, KV_W), lambda b, n, s: (b, 0, 0))],
        scratch_shapes=[pltpu.VMEM((W, KV_W), BF16), pltpu.VMEM((W, KV_W), BF16)],
    )
    return pl.pallas_call(
        _swa_prompt_body,
        grid_spec=grid_spec,
        out_shape=[jax.ShapeDtypeStruct((bsz, seq, SWQ_W), BF16),
                   jax.ShapeDtypeStruct((bsz, W, KV_W), F32),
                   jax.ShapeDtypeStruct((bsz, W, KV_W), F32)],
        compiler_params=_params("arbitrary", "arbitrary"),
        name="swa_prompt",
    )(sinks, z3, z3, z3, cos, sin, perm)


def _swa_decode_body(q_ref, zk_ref, zv_ref, kc_ref, vc_ref, cos_ref, sin_ref, perm_ref, sink_ref,
                     o_ref, kn_ref, vn_ref):
    cos = cos_ref[...]
    sin = sin_ref[...]
    perm = perm_ref[...]
    k_new = _rope(zk_ref[...], cos, sin, perm)
    v_new = zv_ref[...].astype(F32)
    kn_ref[...] = k_new
    vn_ref[...] = v_new
    sink = sink_ref[...]
    for i in range(DEC_TILE):
        q = _rope(q_ref[i], cos, sin, perm) * (SW_HD ** -0.5)
        s = _dot_nt(q, kc_ref[i])
        s_new = jnp.sum(q * k_new[i:i + 1, :], axis=-1, keepdims=True)
        mx = jnp.maximum(jnp.maximum(jnp.max(s, axis=-1, keepdims=True), s_new), sink)
        p = jnp.exp(s - mx)
        p_new = jnp.exp(s_new - mx)
        den = jnp.sum(p, axis=-1, keepdims=True) + p_new + jnp.exp(sink - mx)
        o = _dot(p, vc_ref[i]) + p_new * v_new[i:i + 1, :]
        o_ref[i] = (o / den).astype(o_ref.dtype)


def _swa_decode(q3, z2, kc, vc, cos, sin, perm, sink_col):
    n = z2.shape[0]
    T = DEC_TILE
    W = kc.shape[1]
    return pl.pallas_call(
        _swa_decode_body,
        grid=(n // T,),
        in_specs=[pl.BlockSpec((T, SW_Q_HEADS, LANES), lambda i: (i, 0, 0)),
                  pl.BlockSpec((T, KV_W), lambda i: (i, ZB_K)),
                  pl.BlockSpec((T, KV_W), lambda i: (i, ZB_V)),
                  pl.BlockSpec((T, W, KV_W), lambda i: (i, 0, 0)),
                  pl.BlockSpec((T, W, KV_W), lambda i: (i, 0, 0)),
                  pl.BlockSpec((1, LANES), lambda i: (0, 0)),
                  pl.BlockSpec((1, LANES), lambda i: (0, 0)),
                  pl.BlockSpec((LANES, LANES), lambda i: (0, 0)),
                  pl.BlockSpec((SW_Q_HEADS, 1), lambda i: (0, 0))],
        out_specs=[pl.BlockSpec((T, SW_Q_HEADS, LANES), lambda i: (i, 0, 0)),
                   pl.BlockSpec((T, KV_W), lambda i: (i, 0)),
                   pl.BlockSpec((T, KV_W), lambda i: (i, 0))],
        out_shape=[jax.ShapeDtypeStruct((n, SW_Q_HEADS, LANES), q3.dtype),
                   jax.ShapeDtypeStruct((n, KV_W), F32),
                   jax.ShapeDtypeStruct((n, KV_W), F32)],
        compiler_params=_params("arbitrary"),
        name="swa_decode",
    )(q3, z2, z2, kc, vc, cos, sin, perm, sink_col)


def _merge_body(x_ref, oa_ref, ob_ref, ga_ref, gb_ref, wa_ref, wb_ref, wo_ref, nw_ref, x1_ref, hn_ref):
    ta = _dot(oa_ref[...], wa_ref[...])
    tb = _dot(ob_ref[...], wb_ref[...])
    mg = _sigmoid(ga_ref[...].astype(F32)) * ta + _sigmoid(gb_ref[...].astype(F32)) * tb
    x1 = x_ref[...] + _dot(mg.astype(wo_ref.dtype), wo_ref[...])
    x1_ref[...] = x1
    hn_ref[...] = _rms(x1, nw_ref[...]).astype(hn_ref.dtype)


def _merge(x2d, oa, ob, z2, wa, wb, wo, nw, tm):
    m = x2d.shape[0]

    def const(shape):
        return pl.BlockSpec(shape, lambda i: (0, 0))

    return pl.pallas_call(
        _merge_body,
        grid=(m // tm,),
        in_specs=[pl.BlockSpec((tm, D_MODEL), lambda i: (i, 0)),
                  pl.BlockSpec((tm, HG_W), lambda i: (i, 0)),
                  pl.BlockSpec((tm, SWQ_W), lambda i: (i, 0)),
                  pl.BlockSpec((tm, D_MODEL), lambda i: (i, ZB_GA)),
                  pl.BlockSpec((tm, D_MODEL), lambda i: (i, ZB_GB)),
                  const(wa.shape), const(wb.shape), const(wo.shape), const((1, D_MODEL))],
        out_specs=[pl.BlockSpec((tm, D_MODEL), lambda i: (i, 0)),
                   pl.BlockSpec((tm, D_MODEL), lambda i: (i, 0))],
        out_shape=[jax.ShapeDtypeStruct((m, D_MODEL), F32),
                   jax.ShapeDtypeStruct((m, D_MODEL), wo.dtype)],
        compiler_params=_params("arbitrary"),
        name="merge",
    )(x2d, oa, ob, z2, z2, wa, wb, wo, nw)


FFN_CHUNKS = (768, 768, 768, 512)


def _ffn_body(hn_ref, x1_ref, wg_ref, wu_ref, wd_ref, o_ref, acc_ref):
    hn = hn_ref[...]
    off = 0
    for i, width in enumerate(FFN_CHUNKS):
        g = _dot(hn, wg_ref[:, off:off + width])
        u = _dot(hn, wu_ref[:, off:off + width])
        a = (g * _sigmoid(g) * u).astype(BF16)
        y = _dot(a, wd_ref[off:off + width, :])
        if i == 0:
            acc_ref[...] = x1_ref[...] + y
        else:
            acc_ref[...] += y
        off += width
    o_ref[...] = acc_ref[...]


def _ffn_dense(hn, x1, wg, wu, wd, tm):
    m = hn.shape[0]

    def const(shape):
        return pl.BlockSpec(shape, lambda i: (0, 0))

    return pl.pallas_call(
        _ffn_body,
        grid=(m // tm,),
        in_specs=[pl.BlockSpec((tm, D_MODEL), lambda i: (i, 0)),
                  pl.BlockSpec((tm, D_MODEL), lambda i: (i, 0)),
                  const(wg.shape), const(wu.shape), const(wd.shape)],
        out_specs=pl.BlockSpec((tm, D_MODEL), lambda i: (i, 0)),
        out_shape=jax.ShapeDtypeStruct((m, D_MODEL), F32),
        scratch_shapes=[pltpu.VMEM((tm, D_MODEL), F32)],
        compiler_params=_params("arbitrary"),
        name="ffn_dense",
    )(hn, x1, wg, wu, wd)


def _ffn_stream_body(hn_ref, x1_ref, wg_ref, wu_ref, wd_ref, o_ref):
    @pl.when(pl.program_id(0) == 0)
    def _():
        o_ref[...] = x1_ref[...]

    hn = hn_ref[...]
    g = _dot(hn, wg_ref[...])
    u = _dot(hn, wu_ref[...])
    o_ref[...] += _dot((g * _sigmoid(g) * u).astype(wd_ref.dtype), wd_ref[...])


def _ffn_stream(hn, x1, wg, wu, wd, tf):
    m = hn.shape[0]
    full = pl.BlockSpec((m, D_MODEL), lambda f: (0, 0))
    return pl.pallas_call(
        _ffn_stream_body,
        grid=(wg.shape[1] // tf,),
        in_specs=[full, full,
                  pl.BlockSpec((D_MODEL, tf), lambda f: (0, f)),
                  pl.BlockSpec((D_MODEL, tf), lambda f: (0, f)),
                  pl.BlockSpec((tf, D_MODEL), lambda f: (f, 0))],
        out_specs=full,
        out_shape=jax.ShapeDtypeStruct((m, D_MODEL), F32),
        compiler_params=_params("arbitrary"),
        name="ffn_stream",
    )(hn, x1, wg, wu, wd)


def _router_body(hn_ref, wr_ref, gates_ref, rank_ref, cnt_ref):
    T = hn_ref.shape[0]
    logits = _dot(hn_ref[...], wr_ref[...])
    lane = lax.broadcasted_iota(jnp.int32, logits.shape, 1).astype(F32)
    lg = jnp.where(lane < N_EXPERTS, logits, -jnp.inf)
    m1 = jnp.max(lg, axis=-1, keepdims=True)
    i1 = jnp.min(jnp.where(lg == m1, lane, float(LANES)), axis=-1, keepdims=True)
    lg2 = jnp.where(lane == i1, -jnp.inf, lg)
    m2 = jnp.max(lg2, axis=-1, keepdims=True)
    i2 = jnp.min(jnp.where(lg2 == m2, lane, float(LANES)), axis=-1, keepdims=True)
    e2 = jnp.exp(m2 - m1)
    gates = jnp.where(lane == i1, 1.0 / (1.0 + e2), 0.0) + jnp.where(lane == i2, e2 / (1.0 + e2), 0.0)
    gates_ref[...] = gates
    sel = jnp.where(gates > 0.0, 1.0, 0.0)
    r = lax.broadcasted_iota(jnp.int32, (T, T), 0)
    c = lax.broadcasted_iota(jnp.int32, (T, T), 1)
    before = jnp.where(r > c, 1.0, 0.0).astype(BF16)
    rank_ref[...] = _dot(before, sel.astype(BF16))
    cnt_ref[...] = jnp.broadcast_to(jnp.sum(sel, axis=0, keepdims=True), cnt_ref.shape)


def _router(hn, wr, T):
    m = hn.shape[0]
    nt = m // T
    return pl.pallas_call(
        _router_body,
        grid=(nt,),
        in_specs=[pl.BlockSpec((T, D_MODEL), lambda i: (i, 0)),
                  pl.BlockSpec((D_MODEL, LANES), lambda i: (0, 0))],
        out_specs=[pl.BlockSpec((T, LANES), lambda i: (i, 0)),
                   pl.BlockSpec((T, LANES), lambda i: (i, 0)),
                   pl.BlockSpec((None, SUBLANES, LANES), lambda i: (i, 0, 0))],
        out_shape=[jax.ShapeDtypeStruct((m, LANES), F32),
                   jax.ShapeDtypeStruct((m, LANES), F32),
                   jax.ShapeDtypeStruct((nt, SUBLANES, LANES), F32)],
        compiler_params=_params("arbitrary"),
        name="router",
    )(hn, wr)


def _moe_body(cnt_ref, hn_ref, gates_ref, rank_ref, wg_ref, wu_ref, wd_ref, out_ref, xc_scr, y_scr):
    t = pl.program_id(0)
    e = pl.program_id(1)
    f = pl.program_id(2)
    T = hn_ref.shape[0]
    R = min(MOE_ROWS, T)
    nblk = (cnt_ref[t * N_EXPERTS + e] + (R - 1)) // R

    @pl.when((e == 0) & (f == 0))
    def _():
        out_ref[...] = jnp.zeros_like(out_ref)

    lane = lax.broadcasted_iota(jnp.int32, (T, LANES), 1)
    ge = jnp.sum(jnp.where(lane == e, gates_ref[...], 0.0), axis=1, keepdims=True)
    re = jnp.sum(jnp.where(lane == e, rank_ref[...], 0.0), axis=1, keepdims=True)
    re = jnp.where(ge > 0.0, re, -1.0)
    slot = lax.broadcasted_iota(jnp.int32, (1, R), 1).astype(F32)

    def select(r):
        return jnp.where(re == slot + (r * R).astype(F32), 1.0, 0.0).astype(BF16)

    @pl.when(f == 0)
    def _():
        def compact(r, carry):
            rows = pl.ds(pl.multiple_of(r * R, R), R)
            xc_scr[rows, :] = _dot_tn(select(r), hn_ref[...]).astype(BF16)
            return carry
        lax.fori_loop(0, nblk, compact, 0)

    def expert(r, carry):
        rows = pl.ds(pl.multiple_of(r * R, R), R)
        x = xc_scr[rows, :]
        g = _dot(x, wg_ref[...])
        u = _dot(x, wu_ref[...])
        y = _dot((g * _sigmoid(g) * u).astype(BF16), wd_ref[...])

        @pl.when(f == 0)
        def _():
            y_scr[rows, :] = y

        @pl.when(f > 0)
        def _():
            y_scr[rows, :] += y
        return carry
    lax.fori_loop(0, nblk, expert, 0)

    @pl.when(f == pl.num_programs(2) - 1)
    def _():
        def scatter(r, carry):
            rows = pl.ds(pl.multiple_of(r * R, R), R)
            y = y_scr[rows, :]
            y_hi = y.astype(BF16)
            y_lo = (y - y_hi.astype(F32)).astype(BF16)
            s = select(r)
            out_ref[...] += ge * (_dot(s, y_hi) + _dot(s, y_lo))
            return carry
        lax.fori_loop(0, nblk, scatter, 0)


def _moe(hn, gates, rank, counts, wg, wu, wd, T, nf):
    m = hn.shape[0]
    tf = D_FF_EXPERT // nf
    grid_spec = pltpu.PrefetchScalarGridSpec(
        num_scalar_prefetch=1,
        grid=(m // T, N_EXPERTS, nf),
        in_specs=[pl.BlockSpec((T, D_MODEL), lambda t, e, f, c: (t, 0)),
                  pl.BlockSpec((T, LANES), lambda t, e, f, c: (t, 0)),
                  pl.BlockSpec((T, LANES), lambda t, e, f, c: (t, 0)),
                  pl.BlockSpec((None, D_MODEL, tf), lambda t, e, f, c: (e, 0, f)),
                  pl.BlockSpec((None, D_MODEL, tf), lambda t, e, f, c: (e, 0, f)),
                  pl.BlockSpec((None, tf, D_MODEL), lambda t, e, f, c: (e, f, 0))],
        out_specs=pl.BlockSpec((T, D_MODEL), lambda t, e, f, c: (t, 0)),
        scratch_shapes=[pltpu.VMEM((T, D_MODEL), BF16), pltpu.VMEM((T, D_MODEL), F32)],
    )
    return pl.pallas_call(
        _moe_body,
        grid_spec=grid_spec,
        out_shape=jax.ShapeDtypeStruct((m, D_MODEL), F32),
        compiler_params=_params("arbitrary", "arbitrary", "arbitrary"),
        name="moe",
    )(counts, hn, gates, rank, wg, wu, wd)


def _moe_ffn(hn, wr, wg, wu, wd, T, nf):
    gates, rank, cnt = _router(hn, wr, T)
    counts = cnt[:, 0, :N_EXPERTS].astype(jnp.int32).reshape(-1)
    return _moe(hn.astype(BF16), gates, rank, counts, wg, wu, wd, T, nf)


def _add_norm_body(x_ref, y_ref, nw_ref, o_ref):
    o_ref[...] = _rms(x_ref[...] + y_ref[...], nw_ref[...])


def _add_body(x_ref, y_ref, o_ref):
    o_ref[...] = x_ref[...] + y_ref[...]


def _add_norm(x, y, nw, tm):
    m = x.shape[0]
    spec = pl.BlockSpec((tm, D_MODEL), lambda i: (i, 0))
    return pl.pallas_call(
        _add_norm_body, grid=(m // tm,),
        in_specs=[spec, spec, pl.BlockSpec((1, D_MODEL), lambda i: (0, 0))],
        out_specs=spec, out_shape=jax.ShapeDtypeStruct((m, D_MODEL), F32),
        compiler_params=_params("arbitrary"), name="add_norm",
    )(x, y, nw)


def _norm_body(x_ref, nw_ref, o_ref):
    o_ref[...] = _rms(x_ref[...], nw_ref[...])


def _norm(x, nw, tm):
    m = x.shape[0]
    spec = pl.BlockSpec((tm, D_MODEL), lambda i: (i, 0))
    return pl.pallas_call(
        _norm_body, grid=(m // tm,),
        in_specs=[spec, pl.BlockSpec((1, D_MODEL), lambda i: (0, 0))],
        out_specs=spec, out_shape=jax.ShapeDtypeStruct((m, D_MODEL), F32),
        compiler_params=_params("arbitrary"), name="norm",
    )(x, nw)


def _add(x, y, tm):
    m = x.shape[0]
    spec = pl.BlockSpec((tm, D_MODEL), lambda i: (i, 0))
    return pl.pallas_call(
        _add_body, grid=(m // tm,), in_specs=[spec, spec], out_specs=spec,
        out_shape=jax.ShapeDtypeStruct((m, D_MODEL), F32),
        compiler_params=_params("arbitrary"), name="add",
    )(x, y)


def _layout_w_in(w):
    o = 0
    hg = w[:, o:o + 4 * HG_W]; o += 4 * HG_W
    sq = w[:, o:o + SW_Q_HEADS * SW_HD]; o += SW_Q_HEADS * SW_HD
    sk = w[:, o:o + KV_W]; o += KV_W
    sv = w[:, o:o + KV_W]; o += KV_W
    ga = w[:, o:o + D_MODEL]; o += D_MODEL
    gb = w[:, o:o + D_MODEL]
    zero = jnp.zeros((w.shape[0], SW_HD), w.dtype)
    groups = []
    for h in range(SW_Q_HEADS):
        qh = sq[:, h * SW_HD:(h + 1) * SW_HD]
        groups += [qh, zero] if h // SW_GROUP == 0 else [zero, qh]
    return jnp.concatenate([hg] + groups + [ga, gb, sk, sv], axis=1)


def _layout_w_branch_b(w):
    zero = jnp.zeros((SW_HD, w.shape[1]), w.dtype)
    groups = []
    for h in range(SW_Q_HEADS):
        wh = w[h * SW_HD:(h + 1) * SW_HD]
        groups += [wh, zero] if h // SW_GROUP == 0 else [zero, wh]
    return jnp.concatenate(groups, axis=0)


def _rope_tables(pos):
    half = SW_HD // 2
    inv = ROPE_THETA ** (-jnp.arange(half, dtype=F32) / half)
    ang = pos.astype(F32)[:, None] * inv[None, :]
    cos = jnp.cos(ang)
    sin = jnp.sin(ang)
    reps = LANES // SW_HD
    return jnp.tile(cos, (1, 2 * reps)), jnp.tile(jnp.concatenate([-sin, sin], axis=1), (1, reps))


def _rotate_half_matrix():
    j = jnp.arange(LANES)
    src = jnp.where((j % SW_HD) < SW_HD // 2, j + SW_HD // 2, j - SW_HD // 2)
    return (jnp.arange(LANES)[:, None] == src[None, :]).astype(F32)


def _tile(m, pref):
    return pref if m % pref == 0 else m


def kernel(x_prompt, x_sample, state_hgrn, cache_swa_k, cache_swa_v, norm_mix, w_in, hg_lb_logits, hg_norm,
           swa_sinks, w_branch_a, w_branch_b, w_out, norm_ffn, w_gate_dense, w_up_dense, w_down_dense,
           w_router, w_gate_moe, w_up_moe, w_down_moe, norm_final):
    depth = w_in.shape[0]
    bsz, seq, _ = x_prompt.shape
    nsamp = x_sample.shape[0]
    mp = bsz * seq
    wb = cache_swa_k.shape[2]

    cos_p, sin_p = _rope_tables(jnp.arange(seq))
    cos_s, sin_s = _rope_tables(PAST_LEN + jnp.arange(1))
    perm32 = _rotate_half_matrix()
    perm = perm32.astype(BF16)
    lbl = hg_lb_logits.astype(F32)

    xp = x_prompt.reshape(mp, D_MODEL)
    xs = x_sample.reshape(nsamp, D_MODEL)
    outs = {k: [] for k in ("sp", "kp", "vp", "ss", "ks", "vs")}
    moe_p = moe_s = None

    for l in range(depth):
        if moe_p is not None:
            xp = _add(xp, moe_p, _tile(mp, 1024))
            xs = _add(xs, moe_s, nsamp)
            moe_p = moe_s = None
        w_in32 = _layout_w_in(w_in[l])
        wa32 = w_branch_a[l]
        wbb32 = _layout_w_branch_b(w_branch_b[l])
        wo32 = w_out[l]
        w_in_l = w_in32.astype(BF16)
        wa = wa32.astype(BF16)
        wbb = wbb32.astype(BF16)
        wo = wo32.astype(BF16)
        nmix = norm_mix[l].reshape(1, D_MODEL)
        nffn = norm_ffn[l].reshape(1, D_MODEL)
        hgn = hg_norm[l].reshape(1, HG_D)
        sinks = swa_sinks[l].astype(F32)

        zp = _proj(xp, nmix, w_in_l, _tile(mp, 1024), 1792)
        zp3 = zp.reshape(bsz, seq, Z_WIDTH)
        oa_p, s_p = _hgrn_prompt(zp3, lbl, hgn, l)
        ob_p, k_p, v_p = _swa_prompt(zp3, sinks, cos_p, sin_p, perm)
        x1p, hnp = _merge(xp, oa_p.reshape(mp, HG_W), ob_p.reshape(mp, SWQ_W), zp, wa, wbb, wo, nffn,
                          _tile(mp, 512))

        zs = _proj(xs, nmix, w_in32, nsamp, 896)
        oa_s, s_s = _hgrn_decode(zs, state_hgrn[l].astype(F32), lbl, hgn, l)
        q3 = zs[:, ZB_SWQ * SWQ_W:(ZB_SWQ + 1) * SWQ_W].reshape(nsamp, SW_Q_HEADS, LANES)
        kc = cache_swa_k[l].reshape(nsamp, wb, KV_W).astype(F32)
        vc = cache_swa_v[l].reshape(nsamp, wb, KV_W).astype(F32)
        ob_s, k_s, v_s = _swa_decode(q3, zs, kc, vc, cos_s, sin_s, perm32, sinks.reshape(SW_Q_HEADS, 1))
        x1s, hns = _merge(xs, oa_s, ob_s.reshape(nsamp, SWQ_W), zs, wa32, wbb32, wo32, nffn, nsamp)

        j = l // 2
        if l % 2 == 0:
            xp = _ffn_dense(hnp, x1p, w_gate_dense[j].astype(BF16), w_up_dense[j].astype(BF16),
                            w_down_dense[j].astype(BF16), _tile(mp, 512))
            xs = _ffn_stream(hns, x1s, w_gate_dense[j], w_up_dense[j], w_down_dense[j], 256)
        else:
            wr32 = jnp.pad(w_router[j], ((0, 0), (0, LANES - N_EXPERTS)))
            wg = w_gate_moe[j].astype(BF16)
            wu = w_up_moe[j].astype(BF16)
            wd = w_down_moe[j].astype(BF16)
            moe_p = _moe_ffn(hnp, wr32.astype(BF16), wg, wu, wd, _tile(mp, 1024), 2)
            moe_s = _moe_ffn(hns, wr32, wg, wu, wd, nsamp, 2)
            xp, xs = x1p, x1s

        outs["sp"].append(s_p)
        outs["kp"].append(k_p.reshape(bsz, WINDOW, SW_KV_HEADS, SW_HD))
        outs["vp"].append(v_p.reshape(bsz, WINDOW, SW_KV_HEADS, SW_HD))
        outs["ss"].append(s_s.astype(state_hgrn.dtype))
        k_rows = jnp.concatenate([kc[:, 1:], k_s[:, None, :]], axis=1)
        v_rows = jnp.concatenate([vc[:, 1:], v_s[:, None, :]], axis=1)
        outs["ks"].append(k_rows.reshape(nsamp, wb, SW_KV_HEADS, SW_HD))
        outs["vs"].append(v_rows.reshape(nsamp, wb, SW_KV_HEADS, SW_HD))

    nfin = norm_final.reshape(1, D_MODEL)
    if moe_p is not None:
        yp = _add_norm(xp, moe_p, nfin, _tile(mp, 1024))
        ys = _add_norm(xs, moe_s, nfin, nsamp)
    else:
        yp = _norm(xp, nfin, _tile(mp, 1024))
        ys = _norm(xs, nfin, nsamp)

    return (yp.reshape(x_prompt.shape), ys.reshape(x_sample.shape),
            jnp.stack(outs["sp"]), jnp.stack(outs["kp"]), jnp.stack(outs["vp"]),
            jnp.stack(outs["ss"]), jnp.stack(outs["ks"]), jnp.stack(outs["vs"]))
```

```python
import functools

import jax
import jax.numpy as jnp
from jax import lax
from jax.experimental import pallas as pl
from jax.experimental.pallas import tpu as pltpu

F32 = jnp.float32
BF16 = jnp.bfloat16

D_MODEL = 1024
PAST_LEN = 16384
HG_HEADS = 4
HG_D = 128
HG_W = HG_HEADS * HG_D
SW_Q_HEADS = 8
SW_KV_HEADS = 2
SW_HD = 64
SW_GROUP = SW_Q_HEADS // SW_KV_HEADS
WINDOW = 128
ROPE_THETA = 10000.0
D_FF_DENSE = 2816
N_EXPERTS = 8
D_FF_EXPERT = 3584
RMS_EPS = 1e-6

LANES = 128
SUBLANES = 8
VMEM_LIMIT = 56 * 1024 * 1024

SWQ_W = SW_Q_HEADS * LANES
KV_W = SW_KV_HEADS * SW_HD
Z_WIDTH = 4 * HG_W + SWQ_W + 2 * D_MODEL + 2 * KV_W
ZB_Q, ZB_F, ZB_I, ZB_G = 0, 1, 2, 3
ZB_SWQ, ZB_GA, ZB_GB = 2, 3, 4
ZB_K, ZB_V = 40, 41

HG_CHUNK = 128
MOE_BLOCKS = (256, 128, 64)
DEC_TILE = 8


def _dot_dims(a, b, dims):
    precision = lax.Precision.HIGHEST if a.dtype == F32 else None
    return lax.dot_general(a, b, (dims, ((), ())), precision=precision, preferred_element_type=F32)


def _dot(a, b):
    return _dot_dims(a, b, ((1,), (0,)))


def _dot_nt(a, b):
    return _dot_dims(a, b, ((1,), (1,)))


def _dot_tn(a, b):
    return _dot_dims(a, b, ((0,), (0,)))


def _sigmoid(x):
    return 1.0 / (1.0 + jnp.exp(-x))


def _rms(x, w):
    ms = jnp.mean(x * x, axis=-1, keepdims=True)
    return x * lax.rsqrt(ms + RMS_EPS) * w


def _params(*sem):
    return pltpu.CompilerParams(dimension_semantics=sem, vmem_limit_bytes=VMEM_LIMIT)


def _proj_body(x_ref, nw_ref, w_ref, z_ref, h_scr):
    @pl.when(pl.program_id(1) == 0)
    def _():
        h_scr[...] = _rms(x_ref[...], nw_ref[...]).astype(h_scr.dtype)

    z_ref[...] = _dot(h_scr[...], w_ref[...]).astype(z_ref.dtype)


def _proj(x2d, nw, w, tm, tn):
    m = x2d.shape[0]
    n = w.shape[1]
    return pl.pallas_call(
        _proj_body,
        grid=(m // tm, n // tn),
        in_specs=[
            pl.BlockSpec((tm, D_MODEL), lambda i, j: (i, 0)),
            pl.BlockSpec((1, D_MODEL), lambda i, j: (0, 0)),
            pl.BlockSpec((D_MODEL, tn), lambda i, j: (0, j)),
        ],
        out_specs=pl.BlockSpec((tm, tn), lambda i, j: (i, j)),
        out_shape=jax.ShapeDtypeStruct((m, n), w.dtype),
        scratch_shapes=[pltpu.VMEM((tm, D_MODEL), w.dtype)],
        compiler_params=_params("arbitrary", "arbitrary"),
        name="proj",
    )(x2d, nw, w)


def _lower_bound(lbl, layer):
    mx = jnp.max(lbl, axis=0, keepdims=True)
    e = jnp.exp(lbl - mx)
    sm = e / jnp.sum(e, axis=0, keepdims=True)
    cum = sm[0:1, :]
    for i in range(1, layer + 1):
        cum = cum + sm[i:i + 1, :]
    return cum - sm[0:1, :]


def _split3(x):
    hi = x.astype(BF16)
    r = x - hi.astype(F32)
    mid = r.astype(BF16)
    lo = (r - mid.astype(F32)).astype(BF16)
    return hi, mid, lo


def _hgrn_prompt_body(zq_ref, zf_ref, zi_ref, zg_ref, lbl_ref, nw_ref, o_ref, s_ref, st_scr, b_scr, *, layer):
    c = pl.program_id(1)
    C = HG_CHUNK

    @pl.when(c == 0)
    def _():
        st_scr[...] = jnp.zeros_like(st_scr)

    lb = _lower_bound(lbl_ref[...], layer)
    row = lax.broadcasted_iota(jnp.int32, (C, C), 0)
    col = lax.broadcasted_iota(jnp.int32, (C, C), 1)
    xr = row ^ col
    tri = jnp.where(row >= col, 1.0, 0.0).astype(BF16)
    sub4 = (lax.broadcasted_iota(jnp.int32, (SUBLANES, HG_D), 0) & 4) == 0

    hf = zf_ref[...].astype(F32)
    fg_all = lb + (1.0 - lb) * _sigmoid(hf)
    g_all = jnp.log(fg_all)
    k_all = 1.0 - fg_all
    g1, g2, g3 = _split3(g_all)
    b_all = _dot(tri, g1) + _dot(tri, g2) + _dot(tri, g3)
    hq = zq_ref[...].astype(F32)
    q_all = hq * _sigmoid(hq) * (HG_D ** -0.5)
    gate = zg_ref[...].astype(F32)
    gate_all = gate * _sigmoid(gate)
    nw = nw_ref[...]

    for h in range(HG_HEADS):
        sl = slice(h * HG_D, (h + 1) * HG_D)
        b = b_all[:, sl]
        q = q_all[:, sl]
        k = k_all[:, sl]
        f = fg_all[:, sl]
        v = zi_ref[:, sl]
        b_scr[...] = b

        acc = _dot_nt(q.astype(BF16), k.astype(BF16))
        for m in (1, 2, 4, 8, 16, 32, 64):
            if m == 1:
                qt, kt = q * f, k
            else:
                if m == 2:
                    pieces = []
                    for j in range(C // SUBLANES):
                        lo = jnp.broadcast_to(b_scr[SUBLANES * j + 1:SUBLANES * j + 2, :], (SUBLANES, HG_D))
                        hi = jnp.broadcast_to(b_scr[SUBLANES * j + 5:SUBLANES * j + 6, :], (SUBLANES, HG_D))
                        pieces.append(jnp.where(sub4, lo, hi))
                else:
                    pieces = [jnp.broadcast_to(b_scr[i * 2 * m + m - 1:i * 2 * m + m, :], (2 * m, HG_D))
                              for i in range(C // (2 * m))]
                bref = pieces[0] if len(pieces) == 1 else jnp.concatenate(pieces, axis=0)
                e = jnp.exp(-jnp.abs(b - bref))
                qt, kt = q * e, k * e
            r = _dot_nt(qt.astype(BF16), kt.astype(BF16))
            acc = jnp.where((xr >= m) & (xr < 2 * m), r, acc)
        a = jnp.where(row >= col, acc, 0.0)

        st = st_scr[h]
        o = _dot(a.astype(BF16), v) + _dot_nt((q * jnp.exp(b)).astype(BF16), st.astype(BF16))
        b_last = b[C - 1:C, :]
        kd = k * jnp.exp(b_last - b)
        st_scr[h] = jnp.exp(b_last) * st + _dot_tn(v, kd.astype(BF16))

        o = _rms(o, nw) * gate_all[:, sl]
        o_ref[:, sl] = o.astype(o_ref.dtype)

    @pl.when(c == pl.num_programs(1) - 1)
    def _():
        for h in range(HG_HEADS):
            s_ref[h] = st_scr[h].T


def _hgrn_prompt(z3, lbl, nw, layer):
    bsz, seq, _ = z3.shape
    C = HG_CHUNK

    def zspec(blk):
        return pl.BlockSpec((None, C, HG_W), lambda b, c: (b, c, blk))

    return pl.pallas_call(
        functools.partial(_hgrn_prompt_body, layer=layer),
        grid=(bsz, seq // C),
        in_specs=[zspec(ZB_Q), zspec(ZB_F), zspec(ZB_I), zspec(ZB_G),
                  pl.BlockSpec(lbl.shape, lambda b, c: (0, 0)),
                  pl.BlockSpec((1, HG_D), lambda b, c: (0, 0))],
        out_specs=[pl.BlockSpec((None, C, HG_W), lambda b, c: (b, c, 0)),
                   pl.BlockSpec((None, HG_HEADS, HG_D, HG_D), lambda b, c: (b, 0, 0, 0))],
        out_shape=[jax.ShapeDtypeStruct((bsz, seq, HG_W), BF16),
                   jax.ShapeDtypeStruct((bsz, HG_HEADS, HG_D, HG_D), F32)],
        scratch_shapes=[pltpu.VMEM((HG_HEADS, HG_D, HG_D), F32), pltpu.VMEM((C, HG_D), F32)],
        compiler_params=_params("arbitrary", "arbitrary"),
        name="hgrn_prompt",
    )(z3, z3, z3, z3, lbl, nw)


def _hgrn_decode_body(zq_ref, zf_ref, zi_ref, zg_ref, lbl_ref, nw_ref, s_ref, o_ref, sn_ref, o_scr, *, layer):
    lb = _lower_bound(lbl_ref[...], layer)
    fg = lb + (1.0 - lb) * _sigmoid(zf_ref[...].astype(F32))
    kk = 1.0 - fg
    hq = zq_ref[...].astype(F32)
    q = hq * _sigmoid(hq) * (HG_D ** -0.5)
    v = zi_ref[...].astype(F32)
    eye = (lax.broadcasted_iota(jnp.int32, (HG_D, HG_D), 0)
           == lax.broadcasted_iota(jnp.int32, (HG_D, HG_D), 1))

    def column(r):
        return jnp.sum(jnp.where(eye, jnp.broadcast_to(r, (HG_D, HG_D)), 0.0), axis=1, keepdims=True)

    for i in range(DEC_TILE):
        for h in range(HG_HEADS):
            sl = slice(h * HG_D, (h + 1) * HG_D)
            sn = column(fg[i:i + 1, sl]) * s_ref[i, h] + column(kk[i:i + 1, sl]) * v[i:i + 1, sl]
            sn_ref[i, h] = sn
            o_scr[i:i + 1, sl] = jnp.sum(column(q[i:i + 1, sl]) * sn, axis=0, keepdims=True)

    gate = zg_ref[...].astype(F32)
    gate = gate * _sigmoid(gate)
    nw = nw_ref[...]
    for h in range(HG_HEADS):
        sl = slice(h * HG_D, (h + 1) * HG_D)
        o_ref[:, sl] = (_rms(o_scr[:, sl], nw) * gate[:, sl]).astype(o_ref.dtype)


def _hgrn_decode(z2, state, lbl, nw, layer):
    n = z2.shape[0]
    T = DEC_TILE

    def zspec(blk):
        return pl.BlockSpec((T, HG_W), lambda i: (i, blk))

    sspec = pl.BlockSpec((T, HG_HEADS, HG_D, HG_D), lambda i: (i, 0, 0, 0))
    return pl.pallas_call(
        functools.partial(_hgrn_decode_body, layer=layer),
        grid=(n // T,),
        in_specs=[zspec(ZB_Q), zspec(ZB_F), zspec(ZB_I), zspec(ZB_G),
                  pl.BlockSpec(lbl.shape, lambda i: (0, 0)),
                  pl.BlockSpec((1, HG_D), lambda i: (0, 0)),
                  sspec],
        out_specs=[pl.BlockSpec((T, HG_W), lambda i: (i, 0)), sspec],
        out_shape=[jax.ShapeDtypeStruct((n, HG_W), z2.dtype),
                   jax.ShapeDtypeStruct(state.shape, F32)],
        scratch_shapes=[pltpu.VMEM((T, HG_W), F32)],
        compiler_params=_params("arbitrary"),
        name="hgrn_decode",
    )(z2, z2, z2, z2, lbl, nw, state)


def _rope(x, cos, sin, perm):
    return x.astype(F32) * cos + _dot(x, perm) * sin


def _swa_prompt_body(sinks_ref, zq_ref, zk_ref, zv_ref, cos_ref, sin_ref, perm_ref,
                     o_ref, kr_ref, vr_ref, kprev, vprev):
    n = pl.program_id(1)
    W = WINDOW

    @pl.when(n == 0)
    def _():
        kprev[...] = jnp.zeros_like(kprev)
        vprev[...] = jnp.zeros_like(vprev)

    cos = cos_ref[...]
    sin = sin_ref[...]
    perm = perm_ref[...]
    k_rot = _rope(zk_ref[...], cos, sin, perm)
    kr_ref[...] = k_rot
    vr_ref[...] = zv_ref[...].astype(F32)
    k_cur = k_rot.astype(BF16)
    v_cur = zv_ref[...]
    kk = jnp.concatenate([kprev[...], k_cur], axis=0)
    vv = jnp.concatenate([vprev[...], v_cur], axis=0)

    qi = lax.broadcasted_iota(jnp.int32, (W, 2 * W), 0)
    kj = lax.broadcasted_iota(jnp.int32, (W, 2 * W), 1)
    valid = (kj >= qi) & (kj <= qi + W) & ((n > 0) | (kj >= W))

    heads = range(SW_Q_HEADS)
    q_in = [zq_ref[:, h * LANES:(h + 1) * LANES] for h in heads]
    rot = _dot(jnp.concatenate(q_in, axis=0), perm)
    scale = SW_HD ** -0.5
    q_rot = [((q_in[h].astype(F32) * cos + rot[h * W:(h + 1) * W] * sin) * scale).astype(BF16) for h in heads]
    s_all = _dot_nt(jnp.concatenate(q_rot, axis=0), kk)
    p_all, den_all = [], []
    for h in heads:
        s = jnp.where(valid, s_all[h * W:(h + 1) * W], -jnp.inf)
        sink = sinks_ref[h]
        mx = jnp.maximum(jnp.max(s, axis=-1, keepdims=True), sink)
        p = jnp.exp(s - mx)
        den_all.append(jnp.sum(p, axis=-1, keepdims=True) + jnp.exp(sink - mx))
        p_all.append(p.astype(BF16))
    o_all = _dot(jnp.concatenate(p_all, axis=0), vv)
    for h in heads:
        o_ref[:, h * LANES:(h + 1) * LANES] = (o_all[h * W:(h + 1) * W] / den_all[h]).astype(o_ref.dtype)

    kprev[...] = k_cur
    vprev[...] = v_cur


def _swa_prompt(z3, sinks, cos, sin, perm):
    bsz, seq, _ = z3.shape
    W = WINDOW
    grid_spec = pltpu.PrefetchScalarGridSpec(
        num_scalar_prefetch=1,
        grid=(bsz, seq // W),
        in_specs=[pl.BlockSpec((None, W, SWQ_W), lambda b, n, s: (b, n, ZB_SWQ)),
                  pl.BlockSpec((None, W, KV_W), lambda b, n, s: (b, n, ZB_K)),
                  pl.BlockSpec((None, W, KV_W), lambda b, n, s: (b, n, ZB_V)),
                  pl.BlockSpec((W, LANES), lambda b, n, s: (n, 0)),
                  pl.BlockSpec((W, LANES), lambda b, n, s: (n, 0)),
                  pl.BlockSpec((LANES, LANES), lambda b, n, s: (0, 0))],
        out_specs=[pl.BlockSpec((None, W, SWQ_W), lambda b, n, s: (b, n, 0)),
                   pl.BlockSpec((None, W, KV_W), lambda b, n, s: (b, 0, 0)),
                   pl.BlockSpec((None, W, KV_W), lambda b, n, s: (b, 0, 0))],
        scratch_shapes=[pltpu.VMEM((W, KV_W), BF16), pltpu.VMEM((W, KV_W), BF16)],
    )
    return pl.pallas_call(
        _swa_prompt_body,
        grid_spec=grid_spec,
        out_shape=[jax.ShapeDtypeStruct((bsz, seq, SWQ_W), BF16),
                   jax.ShapeDtypeStruct((bsz, W, KV_W), F32),
                   jax.ShapeDtypeStruct((bsz, W, KV_W), F32)],
        compiler_params=_params("arbitrary", "arbitrary"),
        name="swa_prompt",
    )(sinks, z3, z3, z3, cos, sin, perm)


def _swa_decode_body(q_ref, zk_ref, zv_ref, kc_ref, vc_ref, cos_ref, sin_ref, perm_ref, sink_ref,
                     o_ref, kn_ref, vn_ref):
    cos = cos_ref[...]
    sin = sin_ref[...]
    perm = perm_ref[...]
    k_new = _rope(zk_ref[...], cos, sin, perm)
    v_new = zv_ref[...].astype(F32)
    kn_ref[...] = k_new
    vn_ref[...] = v_new
    sink = sink_ref[...]
    for i in range(DEC_TILE):
        q = _rope(q_ref[i], cos, sin, perm) * (SW_HD ** -0.5)
        s = _dot_nt(q, kc_ref[i])
        s_new = jnp.sum(q * k_new[i:i + 1, :], axis=-1, keepdims=True)
        mx = jnp.maximum(jnp.maximum(jnp.max(s, axis=-1, keepdims=True), s_new), sink)
        p = jnp.exp(s - mx)
        p_new = jnp.exp(s_new - mx)
        den = jnp.sum(p, axis=-1, keepdims=True) + p_new + jnp.exp(sink - mx)
        o = _dot(p, vc_ref[i]) + p_new * v_new[i:i + 1, :]
        o_ref[i] = (o / den).astype(o_ref.dtype)


def _swa_decode(q3, z2, kc, vc, cos, sin, perm, sink_col):
    n = z2.shape[0]
    T = DEC_TILE
    W = kc.shape[1]
    return pl.pallas_call(
        _swa_decode_body,
        grid=(n // T,),
        in_specs=[pl.BlockSpec((T, SW_Q_HEADS, LANES), lambda i: (i, 0, 0)),
                  pl.BlockSpec((T, KV_W), lambda i: (i, ZB_K)),
                  pl.BlockSpec((T, KV_W), lambda i: (i, ZB_V)),
                  pl.BlockSpec((T, W, KV_W), lambda i: (i, 0, 0)),
                  pl.BlockSpec((T, W, KV_W), lambda i: (i, 0, 0)),
                  pl.BlockSpec((1, LANES), lambda i: (0, 0)),
                  pl.BlockSpec((1, LANES), lambda i: (0, 0)),
                  pl.BlockSpec((LANES, LANES), lambda i: (0, 0)),
                  pl.BlockSpec((SW_Q_HEADS, 1), lambda i: (0, 0))],
        out_specs=[pl.BlockSpec((T, SW_Q_HEADS, LANES), lambda i: (i, 0, 0)),
                   pl.BlockSpec((T, KV_W), lambda i: (i, 0)),
                   pl.BlockSpec((T, KV_W), lambda i: (i, 0))],
        out_shape=[jax.ShapeDtypeStruct((n, SW_Q_HEADS, LANES), q3.dtype),
                   jax.ShapeDtypeStruct((n, KV_W), F32),
                   jax.ShapeDtypeStruct((n, KV_W), F32)],
        compiler_params=_params("arbitrary"),
        name="swa_decode",
    )(q3, z2, z2, kc, vc, cos, sin, perm, sink_col)


def _merge_body(x_ref, oa_ref, ob_ref, ga_ref, gb_ref, wa_ref, wb_ref, wo_ref, nw_ref, x1_ref, hn_ref):
    ta = _dot(oa_ref[...], wa_ref[...])
    tb = _dot(ob_ref[...], wb_ref[...])
    mg = _sigmoid(ga_ref[...].astype(F32)) * ta + _sigmoid(gb_ref[...].astype(F32)) * tb
    x1 = x_ref[...] + _dot(mg.astype(wo_ref.dtype), wo_ref[...])
    x1_ref[...] = x1
    hn_ref[...] = _rms(x1, nw_ref[...]).astype(hn_ref.dtype)


def _merge(x2d, oa, ob, z2, wa, wb, wo, nw, tm):
    m = x2d.shape[0]

    def const(shape):
        return pl.BlockSpec(shape, lambda i: (0, 0))

    return pl.pallas_call(
        _merge_body,
        grid=(m // tm,),
        in_specs=[pl.BlockSpec((tm, D_MODEL), lambda i: (i, 0)),
                  pl.BlockSpec((tm, HG_W), lambda i: (i, 0)),
                  pl.BlockSpec((tm, SWQ_W), lambda i: (i, 0)),
                  pl.BlockSpec((tm, D_MODEL), lambda i: (i, ZB_GA)),
                  pl.BlockSpec((tm, D_MODEL), lambda i: (i, ZB_GB)),
                  const(wa.shape), const(wb.shape), const(wo.shape), const((1, D_MODEL))],
        out_specs=[pl.BlockSpec((tm, D_MODEL), lambda i: (i, 0)),
                   pl.BlockSpec((tm, D_MODEL), lambda i: (i, 0))],
        out_shape=[jax.ShapeDtypeStruct((m, D_MODEL), F32),
                   jax.ShapeDtypeStruct((m, D_MODEL), wo.dtype)],
        compiler_params=_params("arbitrary"),
        name="merge",
    )(x2d, oa, ob, z2, z2, wa, wb, wo, nw)


FFN_CHUNKS = (768, 768, 768, 512)


def _ffn_body(hn_ref, x1_ref, wg_ref, wu_ref, wd_ref, o_ref, acc_ref):
    hn = hn_ref[...]
    off = 0
    for i, width in enumerate(FFN_CHUNKS):
        g = _dot(hn, wg_ref[:, off:off + width])
        u = _dot(hn, wu_ref[:, off:off + width])
        a = (g * _sigmoid(g) * u).astype(BF16)
        y = _dot(a, wd_ref[off:off + width, :])
        if i == 0:
            acc_ref[...] = x1_ref[...] + y
        else:
            acc_ref[...] += y
        off += width
    o_ref[...] = acc_ref[...]


def _ffn_dense(hn, x1, wg, wu, wd, tm):
    m = hn.shape[0]

    def const(shape):
        return pl.BlockSpec(shape, lambda i: (0, 0))

    return pl.pallas_call(
        _ffn_body,
        grid=(m // tm,),
        in_specs=[pl.BlockSpec((tm, D_MODEL), lambda i: (i, 0)),
                  pl.BlockSpec((tm, D_MODEL), lambda i: (i, 0)),
                  const(wg.shape), const(wu.shape), const(wd.shape)],
        out_specs=pl.BlockSpec((tm, D_MODEL), lambda i: (i, 0)),
        out_shape=jax.ShapeDtypeStruct((m, D_MODEL), F32),
        scratch_shapes=[pltpu.VMEM((tm, D_MODEL), F32)],
        compiler_params=_params("arbitrary"),
        name="ffn_dense",
    )(hn, x1, wg, wu, wd)


def _ffn_stream_body(hn_ref, x1_ref, wg_ref, wu_ref, wd_ref, o_ref):
    @pl.when(pl.program_id(0) == 0)
    def _():
        o_ref[...] = x1_ref[...]

    hn = hn_ref[...]
    g = _dot(hn, wg_ref[...])
    u = _dot(hn, wu_ref[...])
    o_ref[...] += _dot((g * _sigmoid(g) * u).astype(wd_ref.dtype), wd_ref[...])


def _ffn_stream(hn, x1, wg, wu, wd, tf):
    m = hn.shape[0]
    full = pl.BlockSpec((m, D_MODEL), lambda f: (0, 0))
    return pl.pallas_call(
        _ffn_stream_body,
        grid=(wg.shape[1] // tf,),
        in_specs=[full, full,
                  pl.BlockSpec((D_MODEL, tf), lambda f: (0, f)),
                  pl.BlockSpec((D_MODEL, tf), lambda f: (0, f)),
                  pl.BlockSpec((tf, D_MODEL), lambda f: (f, 0))],
        out_specs=full,
        out_shape=jax.ShapeDtypeStruct((m, D_MODEL), F32),
        compiler_params=_params("arbitrary"),
        name="ffn_stream",
    )(hn, x1, wg, wu, wd)


def _router_body(hn_ref, wr_ref, gates_ref, rank_ref, cnt_ref):
    T = hn_ref.shape[0]
    logits = _dot(hn_ref[...], wr_ref[...])
    lane = lax.broadcasted_iota(jnp.int32, logits.shape, 1).astype(F32)
    lg = jnp.where(lane < N_EXPERTS, logits, -jnp.inf)
    m1 = jnp.max(lg, axis=-1, keepdims=True)
    i1 = jnp.min(jnp.where(lg == m1, lane, float(LANES)), axis=-1, keepdims=True)
    lg2 = jnp.where(lane == i1, -jnp.inf, lg)
    m2 = jnp.max(lg2, axis=-1, keepdims=True)
    i2 = jnp.min(jnp.where(lg2 == m2, lane, float(LANES)), axis=-1, keepdims=True)
    e2 = jnp.exp(m2 - m1)
    gates = jnp.where(lane == i1, 1.0 / (1.0 + e2), 0.0) + jnp.where(lane == i2, e2 / (1.0 + e2), 0.0)
    gates_ref[...] = gates
    sel = jnp.where(gates > 0.0, 1.0, 0.0)
    r = lax.broadcasted_iota(jnp.int32, (T, T), 0)
    c = lax.broadcasted_iota(jnp.int32, (T, T), 1)
    before = jnp.where(r > c, 1.0, 0.0).astype(BF16)
    rank_ref[...] = _dot(before, sel.astype(BF16))
    cnt_ref[...] = jnp.broadcast_to(jnp.sum(sel, axis=0, keepdims=True), cnt_ref.shape)


def _router(hn, wr, T):
    m = hn.shape[0]
    nt = m // T
    return pl.pallas_call(
        _router_body,
        grid=(nt,),
        in_specs=[pl.BlockSpec((T, D_MODEL), lambda i: (i, 0)),
                  pl.BlockSpec((D_MODEL, LANES), lambda i: (0, 0))],
        out_specs=[pl.BlockSpec((T, LANES), lambda i: (i, 0)),
                   pl.BlockSpec((T, LANES), lambda i: (i, 0)),
                   pl.BlockSpec((None, SUBLANES, LANES), lambda i: (i, 0, 0))],
        out_shape=[jax.ShapeDtypeStruct((m, LANES), F32),
                   jax.ShapeDtypeStruct((m, LANES), F32),
                   jax.ShapeDtypeStruct((nt, SUBLANES, LANES), F32)],
        compiler_params=_params("arbitrary"),
        name="router",
    )(hn, wr)


def _moe_body(cnt_ref, hn_ref, gates_ref, rank_ref, wg_ref, wu_ref, wd_ref, out_ref, xc_scr, y_scr):
    t = pl.program_id(0)
    e = pl.program_id(1)
    f = pl.program_id(2)
    T = hn_ref.shape[0]
    sizes = [s for s in MOE_BLOCKS if s <= T]
    big, small = sizes[0], sizes[-1]
    rows = ((cnt_ref[t * N_EXPERTS + e] + (small - 1)) // small) * small
    nbig = rows // big
    rem = rows - nbig * big

    def blocks(fn):
        def step(r, carry):
            fn(pl.multiple_of(r * big, big), big)
            return carry
        lax.fori_loop(0, nbig, step, 0)
        for s in sizes[1:]:
            @pl.when((rem & s) != 0)
            def _(s=s):
                fn(pl.multiple_of(nbig * big + (rem & (big - 2 * s)), s), s)

    @pl.when((e == 0) & (f == 0))
    def _():
        out_ref[...] = jnp.zeros_like(out_ref)

    lane = lax.broadcasted_iota(jnp.int32, (T, LANES), 1)
    ge = jnp.sum(jnp.where(lane == e, gates_ref[...], 0.0), axis=1, keepdims=True)
    re = jnp.sum(jnp.where(lane == e, rank_ref[...], 0.0), axis=1, keepdims=True)
    re = jnp.where(ge > 0.0, re, -1.0)

    def select(start, size):
        slot = lax.broadcasted_iota(jnp.int32, (1, size), 1) + start
        return jnp.where(re == slot.astype(F32), 1.0, 0.0).astype(BF16)

    @pl.when(f == 0)
    def _():
        def compact(start, size):
            xc_scr[pl.ds(start, size), :] = _dot_tn(select(start, size), hn_ref[...]).astype(BF16)
        blocks(compact)

    def expert(start, size):
        x = xc_scr[pl.ds(start, size), :]
        g = _dot(x, wg_ref[...])
        u = _dot(x, wu_ref[...])
        y = _dot((g * _sigmoid(g) * u).astype(BF16), wd_ref[...])

        @pl.when(f == 0)
        def _():
            y_scr[pl.ds(start, size), :] = y

        @pl.when(f > 0)
        def _():
            y_scr[pl.ds(start, size), :] += y
    blocks(expert)

    @pl.when(f == pl.num_programs(2) - 1)
    def _():
        def scatter(start, size):
            y = y_scr[pl.ds(start, size), :].astype(BF16)
            out_ref[...] += ge * _dot(select(start, size), y)
        blocks(scatter)


def _moe(hn, gates, rank, counts, wg, wu, wd, T, nf):
    m = hn.shape[0]
    tf = D_FF_EXPERT // nf
    grid_spec = pltpu.PrefetchScalarGridSpec(
        num_scalar_prefetch=1,
        grid=(m // T, N_EXPERTS, nf),
        in_specs=[pl.BlockSpec((T, D_MODEL), lambda t, e, f, c: (t, 0)),
                  pl.BlockSpec((T, LANES), lambda t, e, f, c: (t, 0)),
                  pl.BlockSpec((T, LANES), lambda t, e, f, c: (t, 0)),
                  pl.BlockSpec((None, D_MODEL, tf), lambda t, e, f, c: (e, 0, f)),
                  pl.BlockSpec((None, D_MODEL, tf), lambda t, e, f, c: (e, 0, f)),
                  pl.BlockSpec((None, tf, D_MODEL), lambda t, e, f, c: (e, f, 0))],
        out_specs=pl.BlockSpec((T, D_MODEL), lambda t, e, f, c: (t, 0)),
        scratch_shapes=[pltpu.VMEM((T, D_MODEL), BF16), pltpu.VMEM((T, D_MODEL), F32)],
    )
    return pl.pallas_call(
        _moe_body,
        grid_spec=grid_spec,
        out_shape=jax.ShapeDtypeStruct((m, D_MODEL), F32),
        compiler_params=_params("arbitrary", "arbitrary", "arbitrary"),
        name="moe",
    )(counts, hn, gates, rank, wg, wu, wd)


def _moe_ffn(hn, wr, wg, wu, wd, T, nf):
    gates, rank, cnt = _router(hn, wr, T)
    counts = cnt[:, 0, :N_EXPERTS].astype(jnp.int32).reshape(-1)
    return _moe(hn.astype(BF16), gates, rank, counts, wg, wu, wd, T, nf)


def _add_norm_body(x_ref, y_ref, nw_ref, o_ref):
    o_ref[...] = _rms(x_ref[...] + y_ref[...], nw_ref[...])


def _add_body(x_ref, y_ref, o_ref):
    o_ref[...] = x_ref[...] + y_ref[...]


def _add_norm(x, y, nw, tm):
    m = x.shape[0]
    spec = pl.BlockSpec((tm, D_MODEL), lambda i: (i, 0))
    return pl.pallas_call(
        _add_norm_body, grid=(m // tm,),
        in_specs=[spec, spec, pl.BlockSpec((1, D_MODEL), lambda i: (0, 0))],
        out_specs=spec, out_shape=jax.ShapeDtypeStruct((m, D_MODEL), F32),
        compiler_params=_params("arbitrary"), name="add_norm",
    )(x, y, nw)


def _norm_body(x_ref, nw_ref, o_ref):
    o_ref[...] = _rms(x_ref[...], nw_ref[...])


def _norm(x, nw, tm):
    m = x.shape[0]
    spec = pl.BlockSpec((tm, D_MODEL), lambda i: (i, 0))
    return pl.pallas_call(
        _norm_body, grid=(m // tm,),
        in_specs=[spec, pl.BlockSpec((1, D_MODEL), lambda i: (0, 0))],
        out_specs=spec, out_shape=jax.ShapeDtypeStruct((m, D_MODEL), F32),
        compiler_params=_params("arbitrary"), name="norm",
    )(x, nw)


def _add(x, y, tm):
    m = x.shape[0]
    spec = pl.BlockSpec((tm, D_MODEL), lambda i: (i, 0))
    return pl.pallas_call(
        _add_body, grid=(m // tm,), in_specs=[spec, spec], out_specs=spec,
        out_shape=jax.ShapeDtypeStruct((m, D_MODEL), F32),
        compiler_params=_params("arbitrary"), name="add",
    )(x, y)


def _layout_w_in(w):
    o = 0
    hg = w[:, o:o + 4 * HG_W]; o += 4 * HG_W
    sq = w[:, o:o + SW_Q_HEADS * SW_HD]; o += SW_Q_HEADS * SW_HD
    sk = w[:, o:o + KV_W]; o += KV_W
    sv = w[:, o:o + KV_W]; o += KV_W
    ga = w[:, o:o + D_MODEL]; o += D_MODEL
    gb = w[:, o:o + D_MODEL]
    zero = jnp.zeros((w.shape[0], SW_HD), w.dtype)
    groups = []
    for h in range(SW_Q_HEADS):
        qh = sq[:, h * SW_HD:(h + 1) * SW_HD]
        groups += [qh, zero] if h // SW_GROUP == 0 else [zero, qh]
    return jnp.concatenate([hg] + groups + [ga, gb, sk, sv], axis=1)


def _layout_w_branch_b(w):
    zero = jnp.zeros((SW_HD, w.shape[1]), w.dtype)
    groups = []
    for h in range(SW_Q_HEADS):
        wh = w[h * SW_HD:(h + 1) * SW_HD]
        groups += [wh, zero] if h // SW_GROUP == 0 else [zero, wh]
    return jnp.concatenate(groups, axis=0)


def _rope_tables(pos):
    half = SW_HD // 2
    inv = ROPE_THETA ** (-jnp.arange(half, dtype=F32) / half)
    ang = pos.astype(F32)[:, None] * inv[None, :]
    cos = jnp.cos(ang)
    sin = jnp.sin(ang)
    reps = LANES // SW_HD
    return jnp.tile(cos, (1, 2 * reps)), jnp.tile(jnp.concatenate([-sin, sin], axis=1), (1, reps))


def _rotate_half_matrix():
    j = jnp.arange(LANES)
    src = jnp.where((j % SW_HD) < SW_HD // 2, j + SW_HD // 2, j - SW_HD // 2)
    return (jnp.arange(LANES)[:, None] == src[None, :]).astype(F32)


def _tile(m, pref):
    return pref if m % pref == 0 else m


def kernel(x_prompt, x_sample, state_hgrn, cache_swa_k, cache_swa_v, norm_mix, w_in, hg_lb_logits, hg_norm,
           swa_sinks, w_branch_a, w_branch_b, w_out, norm_ffn, w_gate_dense, w_up_dense, w_down_dense,
           w_router, w_gate_moe, w_up_moe, w_down_moe, norm_final):
    depth = w_in.shape[0]
    bsz, seq, _ = x_prompt.shape
    nsamp = x_sample.shape[0]
    mp = bsz * seq
    wb = cache_swa_k.shape[2]

    cos_p, sin_p = _rope_tables(jnp.arange(seq))
    cos_s, sin_s = _rope_tables(PAST_LEN + jnp.arange(1))
    perm32 = _rotate_half_matrix()
    perm = perm32.astype(BF16)
    lbl = hg_lb_logits.astype(F32)

    xp = x_prompt.reshape(mp, D_MODEL)
    xs = x_sample.reshape(nsamp, D_MODEL)
    outs = {k: [] for k in ("sp", "kp", "vp", "ss", "ks", "vs")}
    moe_p = moe_s = None

    for l in range(depth):
        if moe_p is not None:
            xp = _add(xp, moe_p, _tile(mp, 1024))
            xs = _add(xs, moe_s, nsamp)
            moe_p = moe_s = None
        w_in32 = _layout_w_in(w_in[l])
        wa32 = w_branch_a[l]
        wbb32 = _layout_w_branch_b(w_branch_b[l])
        wo32 = w_out[l]
        w_in_l = w_in32.astype(BF16)
        wa = wa32.astype(BF16)
        wbb = wbb32.astype(BF16)
        wo = wo32.astype(BF16)
        nmix = norm_mix[l].reshape(1, D_MODEL)
        nffn = norm_ffn[l].reshape(1, D_MODEL)
        hgn = hg_norm[l].reshape(1, HG_D)
        sinks = swa_sinks[l].astype(F32)

        zp = _proj(xp, nmix, w_in_l, _tile(mp, 1024), 1792)
        zp3 = zp.reshape(bsz, seq, Z_WIDTH)
        oa_p, s_p = _hgrn_prompt(zp3, lbl, hgn, l)
        ob_p, k_p, v_p = _swa_prompt(zp3, sinks, cos_p, sin_p, perm)
        x1p, hnp = _merge(xp, oa_p.reshape(mp, HG_W), ob_p.reshape(mp, SWQ_W), zp, wa, wbb, wo, nffn,
                          _tile(mp, 512))

        zs = _proj(xs, nmix, w_in32, nsamp, 896)
        oa_s, s_s = _hgrn_decode(zs, state_hgrn[l].astype(F32), lbl, hgn, l)
        q3 = zs[:, ZB_SWQ * SWQ_W:(ZB_SWQ + 1) * SWQ_W].reshape(nsamp, SW_Q_HEADS, LANES)
        kc = cache_swa_k[l].reshape(nsamp, wb, KV_W).astype(F32)
        vc = cache_swa_v[l].reshape(nsamp, wb, KV_W).astype(F32)
        ob_s, k_s, v_s = _swa_decode(q3, zs, kc, vc, cos_s, sin_s, perm32, sinks.reshape(SW_Q_HEADS, 1))
        x1s, hns = _merge(xs, oa_s, ob_s.reshape(nsamp, SWQ_W), zs, wa32, wbb32, wo32, nffn, nsamp)

        j = l // 2
        if l % 2 == 0:
            xp = _ffn_dense(hnp, x1p, w_gate_dense[j].astype(BF16), w_up_dense[j].astype(BF16),
                            w_down_dense[j].astype(BF16), _tile(mp, 512))
            xs = _ffn_stream(hns, x1s, w_gate_dense[j], w_up_dense[j], w_down_dense[j], 256)
        else:
            wr32 = jnp.pad(w_router[j], ((0, 0), (0, LANES - N_EXPERTS)))
            wg = w_gate_moe[j].astype(BF16)
            wu = w_up_moe[j].astype(BF16)
            wd = w_down_moe[j].astype(BF16)
            moe_p = _moe_ffn(hnp, wr32.astype(BF16), wg, wu, wd, _tile(mp, 1024), 2)
            moe_s = _moe_ffn(hns, wr32, wg, wu, wd, nsamp, 2)
            xp, xs = x1p, x1s

        outs["sp"].append(s_p)
        outs["kp"].append(k_p.reshape(bsz, WINDOW, SW_KV_HEADS, SW_HD))
        outs["vp"].append(v_p.reshape(bsz, WINDOW, SW_KV_HEADS, SW_HD))
        outs["ss"].append(s_s.astype(state_hgrn.dtype))
        k_rows = jnp.concatenate([kc[:, 1:], k_s[:, None, :]], axis=1)
        v_rows = jnp.concatenate([vc[:, 1:], v_s[:, None, :]], axis=1)
        outs["ks"].append(k_rows.reshape(nsamp, wb, SW_KV_HEADS, SW_HD))
        outs["vs"].append(v_rows.reshape(nsamp, wb, SW_KV_HEADS, SW_HD))

    nfin = norm_final.reshape(1, D_MODEL)
    if moe_p is not None:
        yp = _add_norm(xp, moe_p, nfin, _tile(mp, 1024))
        ys = _add_norm(xs, moe_s, nfin, nsamp)
    else:
        yp = _norm(xp, nfin, _tile(mp, 1024))
        ys = _norm(xs, nfin, nsamp)

    return (yp.reshape(x_prompt.shape), ys.reshape(x_sample.shape),
            jnp.stack(outs["sp"]), jnp.stack(outs["kp"]), jnp.stack(outs["vp"]),
            jnp.stack(outs["ss"]), jnp.stack(outs["ks"]), jnp.stack(outs["vs"]))
```

```python
import functools

import jax
import jax.numpy as jnp
from jax import lax
from jax.experimental import pallas as pl
from jax.experimental.pallas import tpu as pltpu

F32 = jnp.float32
BF16 = jnp.bfloat16

D_MODEL = 1024
PAST_LEN = 16384
HG_HEADS = 4
HG_D = 128
HG_W = HG_HEADS * HG_D
SW_Q_HEADS = 8
SW_KV_HEADS = 2
SW_HD = 64
SW_GROUP = SW_Q_HEADS // SW_KV_HEADS
WINDOW = 128
ROPE_THETA = 10000.0
D_FF_DENSE = 2816
N_EXPERTS = 8
D_FF_EXPERT = 3584
RMS_EPS = 1e-6

LANES = 128
SUBLANES = 8
VMEM_LIMIT = 56 * 1024 * 1024

SWQ_W = SW_Q_HEADS * LANES
KV_W = SW_KV_HEADS * SW_HD
Z_WIDTH = 4 * HG_W + SWQ_W + 2 * D_MODEL + 2 * KV_W
ZB_Q, ZB_F, ZB_I, ZB_G = 0, 1, 2, 3
ZB_SWQ, ZB_GA, ZB_GB = 2, 3, 4
ZB_K, ZB_V = 40, 41

HG_CHUNK = 128
DEC_TILE = 8


def _dot_dims(a, b, dims):
    precision = lax.Precision.HIGHEST if a.dtype == F32 else None
    return lax.dot_general(a, b, (dims, ((), ())), precision=precision, preferred_element_type=F32)


def _dot(a, b):
    return _dot_dims(a, b, ((1,), (0,)))


def _dot_nt(a, b):
    return _dot_dims(a, b, ((1,), (1,)))


def _dot_tn(a, b):
    return _dot_dims(a, b, ((0,), (0,)))


def _sigmoid(x):
    return 1.0 / (1.0 + jnp.exp(-x))


def _rms(x, w):
    ms = jnp.mean(x * x, axis=-1, keepdims=True)
    return x * lax.rsqrt(ms + RMS_EPS) * w


def _params(*sem):
    return pltpu.CompilerParams(dimension_semantics=sem, vmem_limit_bytes=VMEM_LIMIT)


def _proj_body(x_ref, nw_ref, w_ref, z_ref, h_scr):
    @pl.when(pl.program_id(1) == 0)
    def _():
        h_scr[...] = _rms(x_ref[...], nw_ref[...]).astype(h_scr.dtype)

    z_ref[...] = _dot(h_scr[...], w_ref[...]).astype(z_ref.dtype)


def _proj(x2d, nw, w, tm, tn):
    m = x2d.shape[0]
    n = w.shape[1]
    return pl.pallas_call(
        _proj_body,
        grid=(m // tm, n // tn),
        in_specs=[
            pl.BlockSpec((tm, D_MODEL), lambda i, j: (i, 0)),
            pl.BlockSpec((1, D_MODEL), lambda i, j: (0, 0)),
            pl.BlockSpec((D_MODEL, tn), lambda i, j: (0, j)),
        ],
        out_specs=pl.BlockSpec((tm, tn), lambda i, j: (i, j)),
        out_shape=jax.ShapeDtypeStruct((m, n), w.dtype),
        scratch_shapes=[pltpu.VMEM((tm, D_MODEL), w.dtype)],
        compiler_params=_params("arbitrary", "arbitrary"),
        name="proj",
    )(x2d, nw, w)


def _lower_bound(lbl, layer):
    mx = jnp.max(lbl, axis=0, keepdims=True)
    e = jnp.exp(lbl - mx)
    sm = e / jnp.sum(e, axis=0, keepdims=True)
    cum = sm[0:1, :]
    for i in range(1, layer + 1):
        cum = cum + sm[i:i + 1, :]
    return cum - sm[0:1, :]


def _split3(x):
    hi = x.astype(BF16)
    r = x - hi.astype(F32)
    mid = r.astype(BF16)
    lo = (r - mid.astype(F32)).astype(BF16)
    return hi, mid, lo


def _hgrn_prompt_body(zq_ref, zf_ref, zi_ref, zg_ref, lbl_ref, nw_ref, o_ref, s_ref, st_scr, b_scr, *, layer):
    c = pl.program_id(1)
    C = HG_CHUNK

    @pl.when(c == 0)
    def _():
        st_scr[...] = jnp.zeros_like(st_scr)

    lb = _lower_bound(lbl_ref[...], layer)
    row = lax.broadcasted_iota(jnp.int32, (C, C), 0)
    col = lax.broadcasted_iota(jnp.int32, (C, C), 1)
    xr = row ^ col
    tri = jnp.where(row >= col, 1.0, 0.0).astype(BF16)
    sub4 = (lax.broadcasted_iota(jnp.int32, (SUBLANES, HG_D), 0) & 4) == 0
    levels = (1, 2, 4, 8, 16, 32, 64)
    pair_level = {m: (xr >= m) & (xr < 2 * m) for m in levels}
    upper_half = {m: (row & m) != 0 for m in levels if m < SUBLANES}

    hf = zf_ref[...].astype(F32)
    fg_all = lb + (1.0 - lb) * _sigmoid(hf)
    g_all = jnp.log2(fg_all)
    k_all = 1.0 - fg_all
    g1, g2, g3 = _split3(g_all)
    b_all = _dot(tri, g1) + _dot(tri, g2) + _dot(tri, g3)
    b_scr[...] = b_all
    hq = zq_ref[...].astype(F32)
    q_all = hq * _sigmoid(hq) * (HG_D ** -0.5)
    gate = zg_ref[...].astype(F32)
    gate_all = gate * _sigmoid(gate)
    nw = nw_ref[...]

    for h in range(HG_HEADS):
        sl = slice(h * HG_D, (h + 1) * HG_D)
        b = b_all[:, sl]
        q = q_all[:, sl]
        k = k_all[:, sl]
        f = fg_all[:, sl]
        v = zi_ref[:, sl]

        acc = _dot_nt(q.astype(BF16), k.astype(BF16))
        for m in levels:
            if m == 1:
                w = jnp.where(upper_half[m], q * f, k)
            elif m < SUBLANES:
                if m == 2:
                    pieces = []
                    for j in range(C // SUBLANES):
                        lo = jnp.broadcast_to(b_scr[SUBLANES * j + 1:SUBLANES * j + 2, sl], (SUBLANES, HG_D))
                        hi = jnp.broadcast_to(b_scr[SUBLANES * j + 5:SUBLANES * j + 6, sl], (SUBLANES, HG_D))
                        pieces.append(jnp.where(sub4, lo, hi))
                else:
                    pieces = [jnp.broadcast_to(b_scr[i * 2 * m + m - 1:i * 2 * m + m, sl], (2 * m, HG_D))
                              for i in range(C // (2 * m))]
                d = b - jnp.concatenate(pieces, axis=0)
                w = jnp.where(upper_half[m], q, k) * jnp.exp2(jnp.where(upper_half[m], d, -d))
            else:
                expo, qk = [], []
                for i in range(C // (2 * m)):
                    lo, mid, hi = i * 2 * m, i * 2 * m + m, (i + 1) * 2 * m
                    bref = b_scr[mid - 1:mid, sl]
                    expo += [bref - b[lo:mid], b[mid:hi] - bref]
                    qk += [k[lo:mid], q[mid:hi]]
                w = jnp.concatenate(qk, axis=0) * jnp.exp2(jnp.concatenate(expo, axis=0))
            wb = w.astype(BF16)
            acc = jnp.where(pair_level[m], _dot_nt(wb, wb), acc)
        a = jnp.where(row >= col, acc, 0.0)

        st = st_scr[h]
        o = _dot(a.astype(BF16), v) + _dot_nt((q * jnp.exp2(b)).astype(BF16), st.astype(BF16))
        b_last = b[C - 1:C, :]
        kd = k * jnp.exp2(b_last - b)
        st_scr[h] = jnp.exp2(b_last) * st + _dot_tn(v, kd.astype(BF16))

        o = _rms(o, nw) * gate_all[:, sl]
        o_ref[:, sl] = o.astype(o_ref.dtype)

    @pl.when(c == pl.num_programs(1) - 1)
    def _():
        for h in range(HG_HEADS):
            s_ref[h] = st_scr[h].T


def _hgrn_prompt(z3, lbl, nw, layer):
    bsz, seq, _ = z3.shape
    C = HG_CHUNK

    def zspec(blk):
        return pl.BlockSpec((None, C, HG_W), lambda b, c: (b, c, blk))

    return pl.pallas_call(
        functools.partial(_hgrn_prompt_body, layer=layer),
        grid=(bsz, seq // C),
        in_specs=[zspec(ZB_Q), zspec(ZB_F), zspec(ZB_I), zspec(ZB_G),
                  pl.BlockSpec(lbl.shape, lambda b, c: (0, 0)),
                  pl.BlockSpec((1, HG_D), lambda b, c: (0, 0))],
        out_specs=[pl.BlockSpec((None, C, HG_W), lambda b, c: (b, c, 0)),
                   pl.BlockSpec((None, HG_HEADS, HG_D, HG_D), lambda b, c: (b, 0, 0, 0))],
        out_shape=[jax.ShapeDtypeStruct((bsz, seq, HG_W), BF16),
                   jax.ShapeDtypeStruct((bsz, HG_HEADS, HG_D, HG_D), F32)],
        scratch_shapes=[pltpu.VMEM((HG_HEADS, HG_D, HG_D), F32), pltpu.VMEM((C, HG_W), F32)],
        compiler_params=_params("arbitrary", "arbitrary"),
        name="hgrn_prompt",
    )(z3, z3, z3, z3, lbl, nw)


def _hgrn_decode_body(zq_ref, zf_ref, zi_ref, zg_ref, lbl_ref, nw_ref, s_ref, o_ref, sn_ref, o_scr, *, layer):
    lb = _lower_bound(lbl_ref[...], layer)
    fg = lb + (1.0 - lb) * _sigmoid(zf_ref[...].astype(F32))
    kk = 1.0 - fg
    hq = zq_ref[...].astype(F32)
    q = hq * _sigmoid(hq) * (HG_D ** -0.5)
    v = zi_ref[...].astype(F32)
    eye = (lax.broadcasted_iota(jnp.int32, (HG_D, HG_D), 0)
           == lax.broadcasted_iota(jnp.int32, (HG_D, HG_D), 1))

    def column(r):
        return jnp.sum(jnp.where(eye, jnp.broadcast_to(r, (HG_D, HG_D)), 0.0), axis=1, keepdims=True)

    for i in range(DEC_TILE):
        for h in range(HG_HEADS):
            sl = slice(h * HG_D, (h + 1) * HG_D)
            sn = column(fg[i:i + 1, sl]) * s_ref[i, h] + column(kk[i:i + 1, sl]) * v[i:i + 1, sl]
            sn_ref[i, h] = sn
            o_scr[i:i + 1, sl] = jnp.sum(column(q[i:i + 1, sl]) * sn, axis=0, keepdims=True)

    gate = zg_ref[...].astype(F32)
    gate = gate * _sigmoid(gate)
    nw = nw_ref[...]
    for h in range(HG_HEADS):
        sl = slice(h * HG_D, (h + 1) * HG_D)
        o_ref[:, sl] = (_rms(o_scr[:, sl], nw) * gate[:, sl]).astype(o_ref.dtype)


def _hgrn_decode(z2, state, lbl, nw, layer):
    n = z2.shape[0]
    T = DEC_TILE

    def zspec(blk):
        return pl.BlockSpec((T, HG_W), lambda i: (i, blk))

    sspec = pl.BlockSpec((T, HG_HEADS, HG_D, HG_D), lambda i: (i, 0, 0, 0))
    return pl.pallas_call(
        functools.partial(_hgrn_decode_body, layer=layer),
        grid=(n // T,),
        in_specs=[zspec(ZB_Q), zspec(ZB_F), zspec(ZB_I), zspec(ZB_G),
                  pl.BlockSpec(lbl.shape, lambda i: (0, 0)),
                  pl.BlockSpec((1, HG_D), lambda i: (0, 0)),
                  sspec],
        out_specs=[pl.BlockSpec((T, HG_W), lambda i: (i, 0)), sspec],
        out_shape=[jax.ShapeDtypeStruct((n, HG_W), z2.dtype),
                   jax.ShapeDtypeStruct(state.shape, F32)],
        scratch_shapes=[pltpu.VMEM((T, HG_W), F32)],
        compiler_params=_params("arbitrary"),
        name="hgrn_decode",
    )(z2, z2, z2, z2, lbl, nw, state)


def _rope(x, cos, sin, perm):
    return x.astype(F32) * cos + _dot(x, perm) * sin


def _swa_prompt_body(sinks_ref, zq_ref, zk_ref, zv_ref, cos_ref, sin_ref, perm_ref,
                     o_ref, kr_ref, vr_ref, kprev, vprev):
    n = pl.program_id(1)
    W = WINDOW

    @pl.when(n == 0)
    def _():
        kprev[...] = jnp.zeros_like(kprev)
        vprev[...] = jnp.zeros_like(vprev)

    cos = cos_ref[...]
    sin = sin_ref[...]
    perm = perm_ref[...]
    k_rot = _rope(zk_ref[...], cos, sin, perm)
    kr_ref[...] = k_rot
    vr_ref[...] = zv_ref[...].astype(F32)
    k_cur = k_rot.astype(BF16)
    v_cur = zv_ref[...]
    kk = jnp.concatenate([kprev[...], k_cur], axis=0)
    vv = jnp.concatenate([vprev[...], v_cur], axis=0)

    qi = lax.broadcasted_iota(jnp.int32, (W, 2 * W), 0)
    kj = lax.broadcasted_iota(jnp.int32, (W, 2 * W), 1)
    valid = (kj >= qi) & (kj <= qi + W) & ((n > 0) | (kj >= W))

    heads = range(SW_Q_HEADS)
    q_in = [zq_ref[:, h * LANES:(h + 1) * LANES] for h in heads]
    rot = _dot(jnp.concatenate(q_in, axis=0), perm)
    scale = SW_HD ** -0.5
    q_rot = [((q_in[h].astype(F32) * cos + rot[h * W:(h + 1) * W] * sin) * scale).astype(BF16) for h in heads]
    s_all = _dot_nt(jnp.concatenate(q_rot, axis=0), kk)
    p_all, den_all = [], []
    for h in heads:
        s = jnp.where(valid, s_all[h * W:(h + 1) * W], -jnp.inf)
        sink = sinks_ref[h]
        mx = jnp.maximum(jnp.max(s, axis=-1, keepdims=True), sink)
        p = jnp.exp(s - mx)
        den_all.append(jnp.sum(p, axis=-1, keepdims=True) + jnp.exp(sink - mx))
        p_all.append(p.astype(BF16))
    o_all = _dot(jnp.concatenate(p_all, axis=0), vv)
    for h in heads:
        o_ref[:, h * LANES:(h + 1) * LANES] = (o_all[h * W:(h + 1) * W] / den_all[h]).astype(o_ref.dtype)

    kprev[...] = k_cur
    vprev[...] = v_cur


def _swa_prompt(z3, sinks, cos, sin, perm):
    bsz, seq, _ = z3.shape
    W = WINDOW
    grid_spec = pltpu.PrefetchScalarGridSpec(
        num_scalar_prefetch=1,
        grid=(bsz, seq // W),
        in_specs=[pl.BlockSpec((None, W, SWQ_W), lambda b, n, s: (b, n, ZB_SWQ)),
                  pl.BlockSpec((None, W, KV_W), lambda b, n, s: (b, n, ZB_K)),
                  pl.BlockSpec((None, W, KV_W), lambda b, n, s: (b, n, ZB_V)),
                  pl.BlockSpec((W, LANES), lambda b, n, s: (n, 0)),
                  pl.BlockSpec((W, LANES), lambda b, n, s: (n, 0)),
                  pl.BlockSpec((LANES, LANES), lambda b, n, s: (0, 0))],
        out_specs=[pl.BlockSpec((None, W, SWQ_W), lambda b, n, s: (b, n, 0)),
                   pl.BlockSpec((None, W, KV_W), lambda b, n, s: (b, 0, 0)),
                   pl.BlockSpec((None, W, KV_W), lambda b, n, s: (b, 0, 0))],
        scratch_shapes=[pltpu.VMEM((W, KV_W), BF16), pltpu.VMEM((W, KV_W), BF16)],
    )
    return pl.pallas_call(
        _swa_prompt_body,
        grid_spec=grid_spec,
        out_shape=[jax.ShapeDtypeStruct((bsz, seq, SWQ_W), BF16),
                   jax.ShapeDtypeStruct((bsz, W, KV_W), F32),
                   jax.ShapeDtypeStruct((bsz, W, KV_W), F32)],
        compiler_params=_params("arbitrary", "arbitrary"),
        name="swa_prompt",
    )(sinks, z3, z3, z3, cos, sin, perm)


def _swa_decode_body(q_ref, zk_ref, zv_ref, kc_ref, vc_ref, cos_ref, sin_ref, perm_ref, sink_ref,
                     o_ref, kn_ref, vn_ref):
    cos = cos_ref[...]
    sin = sin_ref[...]
    perm = perm_ref[...]
    k_new = _rope(zk_ref[...], cos, sin, perm)
    v_new = zv_ref[...].astype(F32)
    kn_ref[...] = k_new
    vn_ref[...] = v_new
    sink = sink_ref[...]
    for i in range(DEC_TILE):
        q = _rope(q_ref[i], cos, sin, perm) * (SW_HD ** -0.5)
        s = _dot_nt(q, kc_ref[i])
        s_new = jnp.sum(q * k_new[i:i + 1, :], axis=-1, keepdims=True)
        mx = jnp.maximum(jnp.maximum(jnp.max(s, axis=-1, keepdims=True), s_new), sink)
        p = jnp.exp(s - mx)
        p_new = jnp.exp(s_new - mx)
        den = jnp.sum(p, axis=-1, keepdims=True) + p_new + jnp.exp(sink - mx)
        o = _dot(p, vc_ref[i]) + p_new * v_new[i:i + 1, :]
        o_ref[i] = (o / den).astype(o_ref.dtype)


def _swa_decode(q3, z2, kc, vc, cos, sin, perm, sink_col):
    n = z2.shape[0]
    T = DEC_TILE
    W = kc.shape[1]
    return pl.pallas_call(
        _swa_decode_body,
        grid=(n // T,),
        in_specs=[pl.BlockSpec((T, SW_Q_HEADS, LANES), lambda i: (i, 0, 0)),
                  pl.BlockSpec((T, KV_W), lambda i: (i, ZB_K)),
                  pl.BlockSpec((T, KV_W), lambda i: (i, ZB_V)),
                  pl.BlockSpec((T, W, KV_W), lambda i: (i, 0, 0)),
                  pl.BlockSpec((T, W, KV_W), lambda i: (i, 0, 0)),
                  pl.BlockSpec((1, LANES), lambda i: (0, 0)),
                  pl.BlockSpec((1, LANES), lambda i: (0, 0)),
                  pl.BlockSpec((LANES, LANES), lambda i: (0, 0)),
                  pl.BlockSpec((SW_Q_HEADS, 1), lambda i: (0, 0))],
        out_specs=[pl.BlockSpec((T, SW_Q_HEADS, LANES), lambda i: (i, 0, 0)),
                   pl.BlockSpec((T, KV_W), lambda i: (i, 0)),
                   pl.BlockSpec((T, KV_W), lambda i: (i, 0))],
        out_shape=[jax.ShapeDtypeStruct((n, SW_Q_HEADS, LANES), q3.dtype),
                   jax.ShapeDtypeStruct((n, KV_W), F32),
                   jax.ShapeDtypeStruct((n, KV_W), F32)],
        compiler_params=_params("arbitrary"),
        name="swa_decode",
    )(q3, z2, z2, kc, vc, cos, sin, perm, sink_col)


def _merge_body(x_ref, oa_ref, ob_ref, ga_ref, gb_ref, wa_ref, wb_ref, wo_ref, nw_ref, x1_ref, hn_ref):
    ta = _dot(oa_ref[...], wa_ref[...])
    tb = _dot(ob_ref[...], wb_ref[...])
    mg = _sigmoid(ga_ref[...].astype(F32)) * ta + _sigmoid(gb_ref[...].astype(F32)) * tb
    x1 = x_ref[...] + _dot(mg.astype(wo_ref.dtype), wo_ref[...])
    x1_ref[...] = x1
    hn_ref[...] = _rms(x1, nw_ref[...]).astype(hn_ref.dtype)


def _merge(x2d, oa, ob, z2, wa, wb, wo, nw, tm):
    m = x2d.shape[0]

    def const(shape):
        return pl.BlockSpec(shape, lambda i: (0, 0))

    return pl.pallas_call(
        _merge_body,
        grid=(m // tm,),
        in_specs=[pl.BlockSpec((tm, D_MODEL), lambda i: (i, 0)),
                  pl.BlockSpec((tm, HG_W), lambda i: (i, 0)),
                  pl.BlockSpec((tm, SWQ_W), lambda i: (i, 0)),
                  pl.BlockSpec((tm, D_MODEL), lambda i: (i, ZB_GA)),
                  pl.BlockSpec((tm, D_MODEL), lambda i: (i, ZB_GB)),
                  const(wa.shape), const(wb.shape), const(wo.shape), const((1, D_MODEL))],
        out_specs=[pl.BlockSpec((tm, D_MODEL), lambda i: (i, 0)),
                   pl.BlockSpec((tm, D_MODEL), lambda i: (i, 0))],
        out_shape=[jax.ShapeDtypeStruct((m, D_MODEL), F32),
                   jax.ShapeDtypeStruct((m, D_MODEL), wo.dtype)],
        compiler_params=_params("arbitrary"),
        name="merge",
    )(x2d, oa, ob, z2, z2, wa, wb, wo, nw)


FFN_CHUNKS = (768, 768, 768, 512)


def _ffn_body(hn_ref, x1_ref, wg_ref, wu_ref, wd_ref, o_ref, acc_ref):
    hn = hn_ref[...]
    off = 0
    for i, width in enumerate(FFN_CHUNKS):
        g = _dot(hn, wg_ref[:, off:off + width])
        u = _dot(hn, wu_ref[:, off:off + width])
        a = (g * _sigmoid(g) * u).astype(BF16)
        y = _dot(a, wd_ref[off:off + width, :])
        if i == 0:
            acc_ref[...] = x1_ref[...] + y
        else:
            acc_ref[...] += y
        off += width
    o_ref[...] = acc_ref[...]


def _ffn_dense(hn, x1, wg, wu, wd, tm):
    m = hn.shape[0]

    def const(shape):
        return pl.BlockSpec(shape, lambda i: (0, 0))

    return pl.pallas_call(
        _ffn_body,
        grid=(m // tm,),
        in_specs=[pl.BlockSpec((tm, D_MODEL), lambda i: (i, 0)),
                  pl.BlockSpec((tm, D_MODEL), lambda i: (i, 0)),
                  const(wg.shape), const(wu.shape), const(wd.shape)],
        out_specs=pl.BlockSpec((tm, D_MODEL), lambda i: (i, 0)),
        out_shape=jax.ShapeDtypeStruct((m, D_MODEL), F32),
        scratch_shapes=[pltpu.VMEM((tm, D_MODEL), F32)],
        compiler_params=_params("arbitrary"),
        name="ffn_dense",
    )(hn, x1, wg, wu, wd)


def _ffn_stream_body(hn_ref, x1_ref, wg_ref, wu_ref, wd_ref, o_ref):
    @pl.when(pl.program_id(0) == 0)
    def _():
        o_ref[...] = x1_ref[...]

    hn = hn_ref[...]
    g = _dot(hn, wg_ref[...])
    u = _dot(hn, wu_ref[...])
    o_ref[...] += _dot((g * _sigmoid(g) * u).astype(wd_ref.dtype), wd_ref[...])


def _ffn_stream(hn, x1, wg, wu, wd, tf):
    m = hn.shape[0]
    full = pl.BlockSpec((m, D_MODEL), lambda f: (0, 0))
    return pl.pallas_call(
        _ffn_stream_body,
        grid=(wg.shape[1] // tf,),
        in_specs=[full, full,
                  pl.BlockSpec((D_MODEL, tf), lambda f: (0, f)),
                  pl.BlockSpec((D_MODEL, tf), lambda f: (0, f)),
                  pl.BlockSpec((tf, D_MODEL), lambda f: (f, 0))],
        out_specs=full,
        out_shape=jax.ShapeDtypeStruct((m, D_MODEL), F32),
        compiler_params=_params("arbitrary"),
        name="ffn_stream",
    )(hn, x1, wg, wu, wd)


RT_E1, RT_E2, RT_G1, RT_G2, RT_R1, RT_R2 = range(6)
SEG_ALIGN = 16
EXPERT_CHUNKS = (1024, 1024, 1024, 512)


def _router_body(hn_ref, wr_ref, route_ref, route_t_ref, cnt_ref):
    T = hn_ref.shape[0]
    logits = _dot(hn_ref[...], wr_ref[...])
    lane = lax.broadcasted_iota(jnp.int32, logits.shape, 1).astype(F32)
    lg = jnp.where(lane < N_EXPERTS, logits, -jnp.inf)
    m1 = jnp.max(lg, axis=-1, keepdims=True)
    i1 = jnp.min(jnp.where(lg == m1, lane, float(LANES)), axis=-1, keepdims=True)
    lg2 = jnp.where(lane == i1, -jnp.inf, lg)
    m2 = jnp.max(lg2, axis=-1, keepdims=True)
    i2 = jnp.min(jnp.where(lg2 == m2, lane, float(LANES)), axis=-1, keepdims=True)
    e2 = jnp.exp(m2 - m1)
    sel = jnp.where((lane == i1) | (lane == i2), 1.0, 0.0)
    r = lax.broadcasted_iota(jnp.int32, (T, T), 0)
    c = lax.broadcasted_iota(jnp.int32, (T, T), 1)
    before = jnp.where(r > c, 1.0, 0.0).astype(BF16)
    rank = _dot(before, sel.astype(BF16))
    r1 = jnp.sum(jnp.where(lane == i1, rank, 0.0), axis=-1, keepdims=True)
    r2 = jnp.sum(jnp.where(lane == i2, rank, 0.0), axis=-1, keepdims=True)
    fields = {RT_E1: i1, RT_E2: i2, RT_G1: 1.0 / (1.0 + e2), RT_G2: e2 / (1.0 + e2), RT_R1: r1, RT_R2: r2}
    route = jnp.zeros_like(logits)
    for idx, val in fields.items():
        route = jnp.where(lane == float(idx), val, route)
    route_ref[...] = route
    route_t_ref[...] = route.T
    cnt_ref[...] = jnp.broadcast_to(jnp.sum(sel, axis=0, keepdims=True), cnt_ref.shape)


def _router(hn, wr, T):
    m = hn.shape[0]
    nt = m // T
    return pl.pallas_call(
        _router_body,
        grid=(nt,),
        in_specs=[pl.BlockSpec((T, D_MODEL), lambda i: (i, 0)),
                  pl.BlockSpec((D_MODEL, LANES), lambda i: (0, 0))],
        out_specs=[pl.BlockSpec((T, LANES), lambda i: (i, 0)),
                   pl.BlockSpec((LANES, T), lambda i: (0, i)),
                   pl.BlockSpec((None, SUBLANES, LANES), lambda i: (i, 0, 0))],
        out_shape=[jax.ShapeDtypeStruct((m, LANES), F32),
                   jax.ShapeDtypeStruct((LANES, m), F32),
                   jax.ShapeDtypeStruct((nt, SUBLANES, LANES), F32)],
        compiler_params=_params("arbitrary"),
        name="router",
    )(hn, wr)


def _segment_copies(seg_ref, so_ref, off_ref, step, make, fn, max_rows):
    sizes = []
    s = max_rows
    while s >= SEG_ALIGN:
        sizes.append(s)
        s //= 2
    for e in range(N_EXPERTS):
        n = seg_ref[step * N_EXPERTS + e]
        src = so_ref[step * N_EXPERTS + e]
        dst = off_ref[step * N_EXPERTS + e]
        for s in sizes:
            @pl.when((n & s) != 0)
            def _(s=s, n=n, src=src, dst=dst):
                done = n & (-2 * s)
                fn(make(pl.multiple_of(src + done, SEG_ALIGN), pl.multiple_of(dst + done, SEG_ALIGN), s))


def _stage_offset(so_ref, step, expert):
    out = jnp.zeros_like(expert)
    for e in range(N_EXPERTS):
        out = jnp.where(expert == float(e), so_ref[step * N_EXPERTS + e].astype(F32), out)
    return out


def _dispatch_body(seg_ref, so_ref, off_ref, hn_ref, rt_ref, init_ref, xs_ref, stage, sem):
    del init_ref
    t = pl.program_id(0)
    nt = pl.num_programs(0)
    T = hn_ref.shape[0]
    S = stage.shape[1]
    slot = t % 2

    def copies(step, slot, fn):
        def make(src, dst, rows):
            return pltpu.make_async_copy(stage.at[slot, pl.ds(src, rows)], xs_ref.at[pl.ds(dst, rows)], sem.at[slot])
        _segment_copies(seg_ref, so_ref, off_ref, step, make, fn, T)

    @pl.when(t >= 2)
    def _():
        copies(t - 2, slot, lambda c: c.wait())

    rt = rt_ref[...]
    tgt1 = _stage_offset(so_ref, t, rt[RT_E1:RT_E1 + 1, :]) + rt[RT_R1:RT_R1 + 1, :]
    tgt2 = _stage_offset(so_ref, t, rt[RT_E2:RT_E2 + 1, :]) + rt[RT_R2:RT_R2 + 1, :]
    rowi = lax.broadcasted_iota(jnp.int32, (S, T), 0).astype(F32)
    sel = jnp.where((rowi == tgt1) | (rowi == tgt2), 1.0, 0.0).astype(BF16)
    stage[slot] = _dot(sel, hn_ref[...]).astype(BF16)
    copies(t, slot, lambda c: c.start())

    @pl.when(t == nt - 1)
    def _():
        copies(t, slot, lambda c: c.wait())

        @pl.when(t >= 1)
        def _():
            copies(t - 1, 1 - slot, lambda c: c.wait())


def _dispatch(hn, route_t, plan, T, S, rows_total):
    m = hn.shape[0]
    grid_spec = pltpu.PrefetchScalarGridSpec(
        num_scalar_prefetch=3,
        grid=(m // T,),
        in_specs=[pl.BlockSpec((T, D_MODEL), lambda t, *_: (t, 0)),
                  pl.BlockSpec((LANES, T), lambda t, *_: (0, t)),
                  pl.BlockSpec(memory_space=pl.ANY)],
        out_specs=pl.BlockSpec(memory_space=pl.ANY),
        scratch_shapes=[pltpu.VMEM((2, S, D_MODEL), BF16), pltpu.SemaphoreType.DMA((2,))],
    )
    return pl.pallas_call(
        _dispatch_body,
        grid_spec=grid_spec,
        out_shape=jax.ShapeDtypeStruct((rows_total, D_MODEL), BF16),
        input_output_aliases={5: 0},
        compiler_params=_params("arbitrary"),
        name="moe_dispatch",
    )(plan["seg"], plan["so"], plan["off"], hn, route_t, jnp.zeros((rows_total, D_MODEL), BF16))


def _experts_body(owner_ref, used_ref, x_ref, wg_ref, wu_ref, wd_ref, y_ref, acc_ref):
    del owner_ref

    @pl.when(pl.program_id(0) < used_ref[0])
    def _():
        x = x_ref[...]
        off = 0
        for i, width in enumerate(EXPERT_CHUNKS):
            g = _dot(x, wg_ref[:, off:off + width])
            u = _dot(x, wu_ref[:, off:off + width])
            y = _dot((g * _sigmoid(g) * u).astype(BF16), wd_ref[off:off + width, :])
            if i == 0:
                acc_ref[...] = y
            else:
                acc_ref[...] += y
            off += width
        y_ref[...] = acc_ref[...].astype(y_ref.dtype)

    @pl.when(pl.program_id(0) >= used_ref[0])
    def _():
        y_ref[...] = jnp.zeros_like(y_ref)


def _experts(xs, plan, wg, wu, wd, bm):
    nblk = xs.shape[0] // bm

    def wspec(shape):
        return pl.BlockSpec((None,) + shape, lambda b, owner, used: (owner[b], 0, 0), pipeline_mode=pl.Buffered(1))

    grid_spec = pltpu.PrefetchScalarGridSpec(
        num_scalar_prefetch=2,
        grid=(nblk,),
        in_specs=[pl.BlockSpec((bm, D_MODEL), lambda b, owner, used: (jnp.minimum(b, used[0] - 1), 0)),
                  wspec((D_MODEL, D_FF_EXPERT)), wspec((D_MODEL, D_FF_EXPERT)), wspec((D_FF_EXPERT, D_MODEL))],
        out_specs=pl.BlockSpec((bm, D_MODEL), lambda b, owner, used: (b, 0)),
        scratch_shapes=[pltpu.VMEM((bm, D_MODEL), F32)],
    )
    return pl.pallas_call(
        _experts_body,
        grid_spec=grid_spec,
        out_shape=jax.ShapeDtypeStruct(xs.shape, BF16),
        compiler_params=_params("arbitrary"),
        name="moe_experts",
    )(plan["owner"], plan["used"], xs, wg, wu, wd)


def _combine_body(seg_ref, so_ref, off_ref, route_ref, x1_ref, nw_ref, ys_ref, o_ref, ybuf, sem, *, final_norm):
    t = pl.program_id(0)
    nt = pl.num_programs(0)
    T = route_ref.shape[0]
    S = ybuf.shape[1]
    slot = t % 2

    def copies(step, slot, fn):
        def make(stage_row, grouped_row, rows):
            return pltpu.make_async_copy(ys_ref.at[pl.ds(grouped_row, rows)], ybuf.at[slot, pl.ds(stage_row, rows)],
                                         sem.at[slot])
        _segment_copies(seg_ref, so_ref, off_ref, step, make, fn, T)

    @pl.when(t == 0)
    def _():
        ybuf[...] = jnp.zeros_like(ybuf)
        copies(0, 0, lambda c: c.start())

    @pl.when(t + 1 < nt)
    def _():
        copies(t + 1, 1 - slot, lambda c: c.start())

    copies(t, slot, lambda c: c.wait())

    route = route_ref[...]
    tgt1 = _stage_offset(so_ref, t, route[:, RT_E1:RT_E1 + 1]) + route[:, RT_R1:RT_R1 + 1]
    tgt2 = _stage_offset(so_ref, t, route[:, RT_E2:RT_E2 + 1]) + route[:, RT_R2:RT_R2 + 1]
    coli = lax.broadcasted_iota(jnp.int32, (T, S), 1).astype(F32)
    sel = (jnp.where(coli == tgt1, route[:, RT_G1:RT_G1 + 1], 0.0)
           + jnp.where(coli == tgt2, route[:, RT_G2:RT_G2 + 1], 0.0))
    x2 = x1_ref[...] + _dot(sel.astype(BF16), ybuf[slot])
    o_ref[...] = _rms(x2, nw_ref[...]) if final_norm else x2


def _combine(ys, route, x1, nw, plan, T, S, final_norm):
    m = x1.shape[0]
    grid_spec = pltpu.PrefetchScalarGridSpec(
        num_scalar_prefetch=3,
        grid=(m // T,),
        in_specs=[pl.BlockSpec((T, LANES), lambda t, *_: (t, 0)),
                  pl.BlockSpec((T, D_MODEL), lambda t, *_: (t, 0)),
                  pl.BlockSpec((1, D_MODEL), lambda t, *_: (0, 0)),
                  pl.BlockSpec(memory_space=pl.ANY)],
        out_specs=pl.BlockSpec((T, D_MODEL), lambda t, *_: (t, 0)),
        scratch_shapes=[pltpu.VMEM((2, S, D_MODEL), BF16), pltpu.SemaphoreType.DMA((2,))],
    )
    return pl.pallas_call(
        functools.partial(_combine_body, final_norm=final_norm),
        grid_spec=grid_spec,
        out_shape=jax.ShapeDtypeStruct((m, D_MODEL), F32),
        compiler_params=_params("arbitrary"),
        name="moe_combine",
    )(plan["seg"], plan["so"], plan["off"], route, x1, nw, ys)


def _moe_plan(cnt, bm, nblk):
    seg = (cnt + (SEG_ALIGN - 1)) // SEG_ALIGN * SEG_ALIGN
    so = jnp.cumsum(seg, axis=1) - seg
    blocks = (jnp.sum(seg, axis=0) + (bm - 1)) // bm
    blk_end = jnp.cumsum(blocks)
    off = (blk_end - blocks)[None, :] * bm + jnp.cumsum(seg, axis=0) - seg
    owner = jnp.minimum(jnp.sum(jnp.arange(nblk)[:, None] >= blk_end[None, :], axis=1), N_EXPERTS - 1)
    as_i32 = lambda a: a.astype(jnp.int32).reshape(-1)
    return {"seg": as_i32(seg), "so": as_i32(so), "off": as_i32(off), "owner": as_i32(owner),
            "used": as_i32(blk_end[-1:])}


def _moe_ffn(hn, x1, nw, wr, wg, wu, wd, T, bm, final_norm):
    m = hn.shape[0]
    tiles = m // T
    pad_rows = tiles * N_EXPERTS * (SEG_ALIGN - 1)
    stage_rows = -(-(2 * T + N_EXPERTS * (SEG_ALIGN - 1)) // LANES) * LANES
    nblk = -(-(2 * m + pad_rows) // bm) + N_EXPERTS
    route, route_t, cnt = _router(hn, wr, T)
    plan = _moe_plan(cnt[:, 0, :N_EXPERTS].astype(jnp.int32), bm, nblk)
    xs = _dispatch(hn.astype(BF16), route_t, plan, T, stage_rows, nblk * bm)
    ys = _experts(xs, plan, wg, wu, wd, bm)
    return _combine(ys, route, x1, nw, plan, T, stage_rows, final_norm)


def _norm_body(x_ref, nw_ref, o_ref):
    o_ref[...] = _rms(x_ref[...], nw_ref[...])


def _norm(x, nw, tm):
    m = x.shape[0]
    spec = pl.BlockSpec((tm, D_MODEL), lambda i: (i, 0))
    return pl.pallas_call(
        _norm_body, grid=(m // tm,),
        in_specs=[spec, pl.BlockSpec((1, D_MODEL), lambda i: (0, 0))],
        out_specs=spec, out_shape=jax.ShapeDtypeStruct((m, D_MODEL), F32),
        compiler_params=_params("arbitrary"), name="norm",
    )(x, nw)


def _layout_w_in(w):
    o = 0
    hg = w[:, o:o + 4 * HG_W]; o += 4 * HG_W
    sq = w[:, o:o + SW_Q_HEADS * SW_HD]; o += SW_Q_HEADS * SW_HD
    sk = w[:, o:o + KV_W]; o += KV_W
    sv = w[:, o:o + KV_W]; o += KV_W
    ga = w[:, o:o + D_MODEL]; o += D_MODEL
    gb = w[:, o:o + D_MODEL]
    zero = jnp.zeros((w.shape[0], SW_HD), w.dtype)
    groups = []
    for h in range(SW_Q_HEADS):
        qh = sq[:, h * SW_HD:(h + 1) * SW_HD]
        groups += [qh, zero] if h // SW_GROUP == 0 else [zero, qh]
    return jnp.concatenate([hg] + groups + [ga, gb, sk, sv], axis=1)


def _layout_w_branch_b(w):
    zero = jnp.zeros((SW_HD, w.shape[1]), w.dtype)
    groups = []
    for h in range(SW_Q_HEADS):
        wh = w[h * SW_HD:(h + 1) * SW_HD]
        groups += [wh, zero] if h // SW_GROUP == 0 else [zero, wh]
    return jnp.concatenate(groups, axis=0)


def _rope_tables(pos):
    half = SW_HD // 2
    inv = ROPE_THETA ** (-jnp.arange(half, dtype=F32) / half)
    ang = pos.astype(F32)[:, None] * inv[None, :]
    cos = jnp.cos(ang)
    sin = jnp.sin(ang)
    reps = LANES // SW_HD
    return jnp.tile(cos, (1, 2 * reps)), jnp.tile(jnp.concatenate([-sin, sin], axis=1), (1, reps))


def _rotate_half_matrix():
    j = jnp.arange(LANES)
    src = jnp.where((j % SW_HD) < SW_HD // 2, j + SW_HD // 2, j - SW_HD // 2)
    return (jnp.arange(LANES)[:, None] == src[None, :]).astype(F32)


def _tile(m, pref):
    return pref if m % pref == 0 else m


def kernel(x_prompt, x_sample, state_hgrn, cache_swa_k, cache_swa_v, norm_mix, w_in, hg_lb_logits, hg_norm,
           swa_sinks, w_branch_a, w_branch_b, w_out, norm_ffn, w_gate_dense, w_up_dense, w_down_dense,
           w_router, w_gate_moe, w_up_moe, w_down_moe, norm_final):
    depth = w_in.shape[0]
    bsz, seq, _ = x_prompt.shape
    nsamp = x_sample.shape[0]
    mp = bsz * seq
    wb = cache_swa_k.shape[2]

    cos_p, sin_p = _rope_tables(jnp.arange(seq))
    cos_s, sin_s = _rope_tables(PAST_LEN + jnp.arange(1))
    perm32 = _rotate_half_matrix()
    perm = perm32.astype(BF16)
    lbl = hg_lb_logits.astype(F32)

    xp = x_prompt.reshape(mp, D_MODEL)
    xs = x_sample.reshape(nsamp, D_MODEL)
    outs = {k: [] for k in ("sp", "kp", "vp", "ss", "ks", "vs")}
    nfin = norm_final.reshape(1, D_MODEL)
    normed = False

    for l in range(depth):
        w_in32 = _layout_w_in(w_in[l])
        wa32 = w_branch_a[l]
        wbb32 = _layout_w_branch_b(w_branch_b[l])
        wo32 = w_out[l]
        w_in_l = w_in32.astype(BF16)
        wa = wa32.astype(BF16)
        wbb = wbb32.astype(BF16)
        wo = wo32.astype(BF16)
        nmix = norm_mix[l].reshape(1, D_MODEL)
        nffn = norm_ffn[l].reshape(1, D_MODEL)
        hgn = hg_norm[l].reshape(1, HG_D)
        sinks = swa_sinks[l].astype(F32)

        zp = _proj(xp, nmix, w_in_l, _tile(mp, 1024), 1792)
        zp3 = zp.reshape(bsz, seq, Z_WIDTH)
        oa_p, s_p = _hgrn_prompt(zp3, lbl, hgn, l)
        ob_p, k_p, v_p = _swa_prompt(zp3, sinks, cos_p, sin_p, perm)
        x1p, hnp = _merge(xp, oa_p.reshape(mp, HG_W), ob_p.reshape(mp, SWQ_W), zp, wa, wbb, wo, nffn,
                          _tile(mp, 512))

        zs = _proj(xs, nmix, w_in32, nsamp, 896)
        oa_s, s_s = _hgrn_decode(zs, state_hgrn[l].astype(F32), lbl, hgn, l)
        q3 = zs[:, ZB_SWQ * SWQ_W:(ZB_SWQ + 1) * SWQ_W].reshape(nsamp, SW_Q_HEADS, LANES)
        kc = cache_swa_k[l].reshape(nsamp, wb, KV_W).astype(F32)
        vc = cache_swa_v[l].reshape(nsamp, wb, KV_W).astype(F32)
        ob_s, k_s, v_s = _swa_decode(q3, zs, kc, vc, cos_s, sin_s, perm32, sinks.reshape(SW_Q_HEADS, 1))
        x1s, hns = _merge(xs, oa_s, ob_s.reshape(nsamp, SWQ_W), zs, wa32, wbb32, wo32, nffn, nsamp)

        j = l // 2
        if l % 2 == 0:
            xp = _ffn_dense(hnp, x1p, w_gate_dense[j].astype(BF16), w_up_dense[j].astype(BF16),
                            w_down_dense[j].astype(BF16), _tile(mp, 512))
            xs = _ffn_stream(hns, x1s, w_gate_dense[j], w_up_dense[j], w_down_dense[j], 256)
        else:
            wr32 = jnp.pad(w_router[j], ((0, 0), (0, LANES - N_EXPERTS)))
            wg = w_gate_moe[j].astype(BF16)
            wu = w_up_moe[j].astype(BF16)
            wd = w_down_moe[j].astype(BF16)
            last = l == depth - 1
            xp = _moe_ffn(hnp, x1p, nfin, wr32.astype(BF16), wg, wu, wd, _tile(mp, 512), 512, last)
            xs = _moe_ffn(hns, x1s, nfin, wr32, wg, wu, wd, nsamp, 128, last)
            normed = last

        outs["sp"].append(s_p)
        outs["kp"].append(k_p.reshape(bsz, WINDOW, SW_KV_HEADS, SW_HD))
        outs["vp"].append(v_p.reshape(bsz, WINDOW, SW_KV_HEADS, SW_HD))
        outs["ss"].append(s_s.astype(state_hgrn.dtype))
        k_rows = jnp.concatenate([kc[:, 1:], k_s[:, None, :]], axis=1)
        v_rows = jnp.concatenate([vc[:, 1:], v_s[:, None, :]], axis=1)
        outs["ks"].append(k_rows.reshape(nsamp, wb, SW_KV_HEADS, SW_HD))
        outs["vs"].append(v_rows.reshape(nsamp, wb, SW_KV_HEADS, SW_HD))

    yp, ys = (xp, xs) if normed else (_norm(xp, nfin, _tile(mp, 1024)), _norm(xs, nfin, nsamp))

    return (yp.reshape(x_prompt.shape), ys.reshape(x_sample.shape),
            jnp.stack(outs["sp"]), jnp.stack(outs["kp"]), jnp.stack(outs["vp"]),
            jnp.stack(outs["ss"]), jnp.stack(outs["ks"]), jnp.stack(outs["vs"]))
```

```python
import functools

import jax
import jax.numpy as jnp
from jax import lax
from jax.experimental import pallas as pl
from jax.experimental.pallas import tpu as pltpu

F32 = jnp.float32
BF16 = jnp.bfloat16

D_MODEL = 1024
PAST_LEN = 16384
HG_HEADS = 4
HG_D = 128
HG_W = HG_HEADS * HG_D
SW_Q_HEADS = 8
SW_KV_HEADS = 2
SW_HD = 64
SW_GROUP = SW_Q_HEADS // SW_KV_HEADS
WINDOW = 128
ROPE_THETA = 10000.0
D_FF_DENSE = 2816
N_EXPERTS = 8
D_FF_EXPERT = 3584
RMS_EPS = 1e-6

LANES = 128
SUBLANES = 8
VMEM_LIMIT = 56 * 1024 * 1024

SWQ_W = SW_Q_HEADS * LANES
KV_W = SW_KV_HEADS * SW_HD
Z_WIDTH = 4 * HG_W + SWQ_W + 2 * D_MODEL + 2 * KV_W
ZB_Q, ZB_F, ZB_I, ZB_G = 0, 1, 2, 3
ZB_SWQ, ZB_GA, ZB_GB = 2, 3, 4
ZB_K, ZB_V = 40, 41

HG_CHUNK = 128
SEQ_BLOCK = 512
DEC_TILE = 8


def _dot_dims(a, b, dims):
    precision = lax.Precision.HIGHEST if a.dtype == F32 else None
    return lax.dot_general(a, b, (dims, ((), ())), precision=precision, preferred_element_type=F32)


def _dot(a, b):
    return _dot_dims(a, b, ((1,), (0,)))


def _dot_nt(a, b):
    return _dot_dims(a, b, ((1,), (1,)))


def _dot_tn(a, b):
    return _dot_dims(a, b, ((0,), (0,)))


def _sigmoid(x):
    return 1.0 / (1.0 + jnp.exp(-x))


def _rms(x, w):
    ms = jnp.mean(x * x, axis=-1, keepdims=True)
    return x * lax.rsqrt(ms + RMS_EPS) * w


def _params(*sem):
    return pltpu.CompilerParams(dimension_semantics=sem, vmem_limit_bytes=VMEM_LIMIT)


def _proj_body(x_ref, nw_ref, w_ref, z_ref, h_scr):
    @pl.when(pl.program_id(1) == 0)
    def _():
        h_scr[...] = _rms(x_ref[...], nw_ref[...]).astype(h_scr.dtype)

    z_ref[...] = _dot(h_scr[...], w_ref[...]).astype(z_ref.dtype)


def _proj(x2d, nw, w, tm, tn):
    m = x2d.shape[0]
    n = w.shape[1]
    return pl.pallas_call(
        _proj_body,
        grid=(m // tm, n // tn),
        in_specs=[
            pl.BlockSpec((tm, D_MODEL), lambda i, j: (i, 0)),
            pl.BlockSpec((1, D_MODEL), lambda i, j: (0, 0)),
            pl.BlockSpec((D_MODEL, tn), lambda i, j: (0, j)),
        ],
        out_specs=pl.BlockSpec((tm, tn), lambda i, j: (i, j)),
        out_shape=jax.ShapeDtypeStruct((m, n), w.dtype),
        scratch_shapes=[pltpu.VMEM((tm, D_MODEL), w.dtype)],
        compiler_params=_params("arbitrary", "arbitrary"),
        name="proj",
    )(x2d, nw, w)


def _lower_bound(lbl, layer):
    mx = jnp.max(lbl, axis=0, keepdims=True)
    e = jnp.exp(lbl - mx)
    sm = e / jnp.sum(e, axis=0, keepdims=True)
    cum = sm[0:1, :]
    for i in range(1, layer + 1):
        cum = cum + sm[i:i + 1, :]
    return cum - sm[0:1, :]


def _split3(x):
    hi = x.astype(BF16)
    r = x - hi.astype(F32)
    mid = r.astype(BF16)
    lo = (r - mid.astype(F32)).astype(BF16)
    return hi, mid, lo


def _hgrn_prompt_body(zq_ref, zf_ref, zi_ref, zg_ref, lbl_ref, nw_ref, o_ref, s_ref, st_scr, b_scr, *, layer):
    c = pl.program_id(1)
    C = HG_CHUNK

    @pl.when(c == 0)
    def _():
        st_scr[...] = jnp.zeros_like(st_scr)

    lb = _lower_bound(lbl_ref[...], layer)
    row = lax.broadcasted_iota(jnp.int32, (C, C), 0)
    col = lax.broadcasted_iota(jnp.int32, (C, C), 1)
    xr = row ^ col
    tri = jnp.where(row >= col, 1.0, 0.0).astype(BF16)
    sub4 = (lax.broadcasted_iota(jnp.int32, (SUBLANES, HG_D), 0) & 4) == 0
    levels = (1, 2, 4, 8, 16, 32, 64)
    pair_level = {m: (xr >= m) & (xr < 2 * m) for m in levels}
    upper_half = {m: (row & m) != 0 for m in levels if m < SUBLANES}
    nw = nw_ref[...]

    def chunk(i, carry):
        rows = pl.ds(pl.multiple_of(i * C, C), C)
        hf = zf_ref[rows, :].astype(F32)
        fg_all = lb + (1.0 - lb) * _sigmoid(hf)
        g_all = jnp.log2(fg_all)
        k_all = 1.0 - fg_all
        g1, g2, g3 = _split3(g_all)
        b_all = _dot(tri, g1) + _dot(tri, g2) + _dot(tri, g3)
        b_scr[...] = b_all
        hq = zq_ref[rows, :].astype(F32)
        q_all = hq * _sigmoid(hq) * (HG_D ** -0.5)
        gate = zg_ref[rows, :].astype(F32)
        gate_all = gate * _sigmoid(gate)
        for h in range(HG_HEADS):
            sl = slice(h * HG_D, (h + 1) * HG_D)
            head(h, rows, b_all[:, sl], q_all[:, sl], k_all[:, sl], fg_all[:, sl], gate_all[:, sl])
        return carry

    def head(h, rows, b, q, k, f, gate):
        sl = slice(h * HG_D, (h + 1) * HG_D)
        v = zi_ref[rows, sl]

        acc = _dot_nt(q.astype(BF16), k.astype(BF16))
        for m in levels:
            if m == 1:
                w = jnp.where(upper_half[m], q * f, k)
            elif m < SUBLANES:
                if m == 2:
                    pieces = []
                    for j in range(C // SUBLANES):
                        lo = jnp.broadcast_to(b_scr[SUBLANES * j + 1:SUBLANES * j + 2, sl], (SUBLANES, HG_D))
                        hi = jnp.broadcast_to(b_scr[SUBLANES * j + 5:SUBLANES * j + 6, sl], (SUBLANES, HG_D))
                        pieces.append(jnp.where(sub4, lo, hi))
                else:
                    pieces = [jnp.broadcast_to(b_scr[i * 2 * m + m - 1:i * 2 * m + m, sl], (2 * m, HG_D))
                              for i in range(C // (2 * m))]
                d = b - jnp.concatenate(pieces, axis=0)
                w = jnp.where(upper_half[m], q, k) * jnp.exp2(jnp.where(upper_half[m], d, -d))
            else:
                expo, qk = [], []
                for i in range(C // (2 * m)):
                    lo, mid, hi = i * 2 * m, i * 2 * m + m, (i + 1) * 2 * m
                    bref = b_scr[mid - 1:mid, sl]
                    expo += [bref - b[lo:mid], b[mid:hi] - bref]
                    qk += [k[lo:mid], q[mid:hi]]
                w = jnp.concatenate(qk, axis=0) * jnp.exp2(jnp.concatenate(expo, axis=0))
            wb = w.astype(BF16)
            acc = jnp.where(pair_level[m], _dot_nt(wb, wb), acc)
        a = jnp.where(row >= col, acc, 0.0)

        st = st_scr[h]
        o = _dot(a.astype(BF16), v) + _dot_nt((q * jnp.exp2(b)).astype(BF16), st.astype(BF16))
        b_last = b[C - 1:C, :]
        kd = k * jnp.exp2(b_last - b)
        st_scr[h] = jnp.exp2(b_last) * st + _dot_tn(v, kd.astype(BF16))

        o_ref[rows, sl] = (_rms(o, nw) * gate).astype(o_ref.dtype)

    lax.fori_loop(0, zq_ref.shape[0] // C, chunk, 0)

    @pl.when(c == pl.num_programs(1) - 1)
    def _():
        for h in range(HG_HEADS):
            s_ref[h] = st_scr[h].T


def _hgrn_prompt(z3, lbl, nw, layer):
    bsz, seq, _ = z3.shape
    C = HG_CHUNK
    rows = _tile(seq, SEQ_BLOCK)

    def zspec(blk):
        return pl.BlockSpec((None, rows, HG_W), lambda b, c: (b, c, blk))

    return pl.pallas_call(
        functools.partial(_hgrn_prompt_body, layer=layer),
        grid=(bsz, seq // rows),
        in_specs=[zspec(ZB_Q), zspec(ZB_F), zspec(ZB_I), zspec(ZB_G),
                  pl.BlockSpec(lbl.shape, lambda b, c: (0, 0)),
                  pl.BlockSpec((1, HG_D), lambda b, c: (0, 0))],
        out_specs=[pl.BlockSpec((None, rows, HG_W), lambda b, c: (b, c, 0)),
                   pl.BlockSpec((None, HG_HEADS, HG_D, HG_D), lambda b, c: (b, 0, 0, 0))],
        out_shape=[jax.ShapeDtypeStruct((bsz, seq, HG_W), BF16),
                   jax.ShapeDtypeStruct((bsz, HG_HEADS, HG_D, HG_D), F32)],
        scratch_shapes=[pltpu.VMEM((HG_HEADS, HG_D, HG_D), F32), pltpu.VMEM((C, HG_W), F32)],
        compiler_params=_params("arbitrary", "arbitrary"),
        name="hgrn_prompt",
    )(z3, z3, z3, z3, lbl, nw)


def _hgrn_decode_body(zq_ref, zf_ref, zi_ref, zg_ref, lbl_ref, nw_ref, s_ref, o_ref, sn_ref, o_scr, *, layer):
    lb = _lower_bound(lbl_ref[...], layer)
    fg = lb + (1.0 - lb) * _sigmoid(zf_ref[...].astype(F32))
    kk = 1.0 - fg
    hq = zq_ref[...].astype(F32)
    q = hq * _sigmoid(hq) * (HG_D ** -0.5)
    v = zi_ref[...].astype(F32)
    eye = (lax.broadcasted_iota(jnp.int32, (HG_D, HG_D), 0)
           == lax.broadcasted_iota(jnp.int32, (HG_D, HG_D), 1))

    def column(r):
        return jnp.sum(jnp.where(eye, jnp.broadcast_to(r, (HG_D, HG_D)), 0.0), axis=1, keepdims=True)

    for i in range(DEC_TILE):
        for h in range(HG_HEADS):
            sl = slice(h * HG_D, (h + 1) * HG_D)
            sn = column(fg[i:i + 1, sl]) * s_ref[i, h] + column(kk[i:i + 1, sl]) * v[i:i + 1, sl]
            sn_ref[i, h] = sn
            o_scr[i:i + 1, sl] = jnp.sum(column(q[i:i + 1, sl]) * sn, axis=0, keepdims=True)

    gate = zg_ref[...].astype(F32)
    gate = gate * _sigmoid(gate)
    nw = nw_ref[...]
    for h in range(HG_HEADS):
        sl = slice(h * HG_D, (h + 1) * HG_D)
        o_ref[:, sl] = (_rms(o_scr[:, sl], nw) * gate[:, sl]).astype(o_ref.dtype)


def _hgrn_decode(z2, state, lbl, nw, layer):
    n = z2.shape[0]
    T = DEC_TILE

    def zspec(blk):
        return pl.BlockSpec((T, HG_W), lambda i: (i, blk))

    sblock = (T, HG_HEADS, HG_D, HG_D)
    return pl.pallas_call(
        functools.partial(_hgrn_decode_body, layer=layer),
        grid=(n // T,),
        in_specs=[zspec(ZB_Q), zspec(ZB_F), zspec(ZB_I), zspec(ZB_G),
                  pl.BlockSpec(lbl.shape, lambda i: (0, 0)),
                  pl.BlockSpec((1, HG_D), lambda i: (0, 0)),
                  pl.BlockSpec((None,) + sblock, lambda i: (layer, i, 0, 0, 0))],
        out_specs=[pl.BlockSpec((T, HG_W), lambda i: (i, 0)), pl.BlockSpec(sblock, lambda i: (i, 0, 0, 0))],
        out_shape=[jax.ShapeDtypeStruct((n, HG_W), z2.dtype),
                   jax.ShapeDtypeStruct(state.shape[1:], F32)],
        scratch_shapes=[pltpu.VMEM((T, HG_W), F32)],
        compiler_params=_params("arbitrary"),
        name="hgrn_decode",
    )(z2, z2, z2, z2, lbl, nw, state)


def _rope(x, cos, sin, perm):
    return x.astype(F32) * cos + _dot(x, perm) * sin


def _swa_prompt_body(sinks_ref, zq_ref, zk_ref, zv_ref, cos_ref, sin_ref, perm_ref,
                     o_ref, kr_ref, vr_ref, kprev, vprev):
    n = pl.program_id(1)
    W = WINDOW

    @pl.when(n == 0)
    def _():
        kprev[...] = jnp.zeros_like(kprev)
        vprev[...] = jnp.zeros_like(vprev)

    perm = perm_ref[...]
    qi = lax.broadcasted_iota(jnp.int32, (W, 2 * W), 0)
    kj = lax.broadcasted_iota(jnp.int32, (W, 2 * W), 1)
    band = (kj >= qi) & (kj <= qi + W)
    heads = range(SW_Q_HEADS)
    scale = SW_HD ** -0.5

    def block(i, carry):
        rows = pl.ds(pl.multiple_of(i * W, W), W)
        cos = cos_ref[rows, :]
        sin = sin_ref[rows, :]
        k_rot = _rope(zk_ref[rows, :], cos, sin, perm)
        kr_ref[...] = k_rot
        vr_ref[...] = zv_ref[rows, :].astype(F32)
        k_cur = k_rot.astype(BF16)
        v_cur = zv_ref[rows, :]
        kk = jnp.concatenate([kprev[...], k_cur], axis=0)
        vv = jnp.concatenate([vprev[...], v_cur], axis=0)
        valid = band & (((n > 0) | (i > 0)) | (kj >= W))

        q_in = [zq_ref[rows, h * LANES:(h + 1) * LANES] for h in heads]
        rot = _dot(jnp.concatenate(q_in, axis=0), perm)
        q_rot = [((q_in[h].astype(F32) * cos + rot[h * W:(h + 1) * W] * sin) * scale).astype(BF16) for h in heads]
        s_all = _dot_nt(jnp.concatenate(q_rot, axis=0), kk)
        p_all, den_all = [], []
        for h in heads:
            s = jnp.where(valid, s_all[h * W:(h + 1) * W], -jnp.inf)
            sink = sinks_ref[h]
            mx = jnp.maximum(jnp.max(s, axis=-1, keepdims=True), sink)
            p = jnp.exp(s - mx)
            den_all.append(jnp.sum(p, axis=-1, keepdims=True) + jnp.exp(sink - mx))
            p_all.append(p.astype(BF16))
        o_all = _dot(jnp.concatenate(p_all, axis=0), vv)
        for h in heads:
            o_ref[rows, h * LANES:(h + 1) * LANES] = (o_all[h * W:(h + 1) * W] / den_all[h]).astype(o_ref.dtype)
        kprev[...] = k_cur
        vprev[...] = v_cur
        return carry

    lax.fori_loop(0, zq_ref.shape[0] // W, block, 0)


def _swa_prompt(z3, sinks, cos, sin, perm):
    bsz, seq, _ = z3.shape
    W = WINDOW
    rows = _tile(seq, SEQ_BLOCK)
    grid_spec = pltpu.PrefetchScalarGridSpec(
        num_scalar_prefetch=1,
        grid=(bsz, seq // rows),
        in_specs=[pl.BlockSpec((None, rows, SWQ_W), lambda b, n, s: (b, n, ZB_SWQ)),
                  pl.BlockSpec((None, rows, KV_W), lambda b, n, s: (b, n, ZB_K)),
                  pl.BlockSpec((None, rows, KV_W), lambda b, n, s: (b, n, ZB_V)),
                  pl.BlockSpec((rows, LANES), lambda b, n, s: (n, 0)),
                  pl.BlockSpec((rows, LANES), lambda b, n, s: (n, 0)),
                  pl.BlockSpec((LANES, LANES), lambda b, n, s: (0, 0))],
        out_specs=[pl.BlockSpec((None, rows, SWQ_W), lambda b, n, s: (b, n, 0)),
                   pl.BlockSpec((None, W, KV_W), lambda b, n, s: (b, 0, 0)),
                   pl.BlockSpec((None, W, KV_W), lambda b, n, s: (b, 0, 0))],
        scratch_shapes=[pltpu.VMEM((W, KV_W), BF16), pltpu.VMEM((W, KV_W), BF16)],
    )
    return pl.pallas_call(
        _swa_prompt_body,
        grid_spec=grid_spec,
        out_shape=[jax.ShapeDtypeStruct((bsz, seq, SWQ_W), BF16),
                   jax.ShapeDtypeStruct((bsz, W, KV_W), F32),
                   jax.ShapeDtypeStruct((bsz, W, KV_W), F32)],
        compiler_params=_params("arbitrary", "arbitrary"),
        name="swa_prompt",
    )(sinks, z3, z3, z3, cos, sin, perm)


def _swa_decode_body(q_ref, zk_ref, zv_ref, kc_ref, vc_ref, cos_ref, sin_ref, perm_ref, sink_ref,
                     o_ref, kn_ref, vn_ref):
    cos = cos_ref[...]
    sin = sin_ref[...]
    perm = perm_ref[...]
    k_new = _rope(zk_ref[...], cos, sin, perm)
    v_new = zv_ref[...].astype(F32)
    kn_ref[...] = k_new
    vn_ref[...] = v_new
    sink = sink_ref[...]
    W = kc_ref.shape[1]
    heads = range(SW_Q_HEADS)
    lane_row = lax.broadcasted_iota(jnp.int32, (1, LANES), 1)
    lane = lax.broadcasted_iota(jnp.int32, (W, LANES), 1)
    for i in range(DEC_TILE):
        q = _rope(q_ref[i], cos, sin, perm) * (SW_HD ** -0.5)
        keys = kc_ref[i]
        vals = vc_ref[i]
        s = jnp.zeros((W, LANES), F32)
        s_new = jnp.zeros((1, LANES), F32)
        for h in heads:
            qh = q[h:h + 1, :]
            s = jnp.where(lane == h, jnp.sum(keys * qh, axis=-1, keepdims=True), s)
            s_new = jnp.where(lane_row == h, jnp.sum(k_new[i:i + 1, :] * qh, axis=-1, keepdims=True), s_new)
        mx = jnp.maximum(jnp.maximum(jnp.max(s, axis=0, keepdims=True), s_new), sink)
        p = jnp.exp(s - mx)
        p_new = jnp.exp(s_new - mx)
        den = jnp.sum(p, axis=0, keepdims=True) + p_new + jnp.exp(sink - mx)
        w_new = p_new / den
        wgt = p / den
        for h in heads:
            o = jnp.sum(wgt[:, h:h + 1] * vals, axis=0, keepdims=True) + w_new[:, h:h + 1] * v_new[i:i + 1, :]
            o_ref[i, h:h + 1, :] = o.astype(o_ref.dtype)


def _swa_decode(q3, z2, kc, vc, layer, cos, sin, perm, sink_row):
    n = z2.shape[0]
    T = DEC_TILE
    W = kc.shape[2]
    cache = pl.BlockSpec((None, T, W, KV_W), lambda i: (layer, i, 0, 0))
    return pl.pallas_call(
        _swa_decode_body,
        grid=(n // T,),
        in_specs=[pl.BlockSpec((T, SW_Q_HEADS, LANES), lambda i: (i, 0, 0)),
                  pl.BlockSpec((T, KV_W), lambda i: (i, ZB_K)),
                  pl.BlockSpec((T, KV_W), lambda i: (i, ZB_V)),
                  cache, cache,
                  pl.BlockSpec((1, LANES), lambda i: (0, 0)),
                  pl.BlockSpec((1, LANES), lambda i: (0, 0)),
                  pl.BlockSpec((LANES, LANES), lambda i: (0, 0)),
                  pl.BlockSpec((1, LANES), lambda i: (0, 0))],
        out_specs=[pl.BlockSpec((T, SW_Q_HEADS, LANES), lambda i: (i, 0, 0)),
                   pl.BlockSpec((T, KV_W), lambda i: (i, 0)),
                   pl.BlockSpec((T, KV_W), lambda i: (i, 0))],
        out_shape=[jax.ShapeDtypeStruct((n, SW_Q_HEADS, LANES), q3.dtype),
                   jax.ShapeDtypeStruct((n, KV_W), F32),
                   jax.ShapeDtypeStruct((n, KV_W), F32)],
        compiler_params=_params("arbitrary"),
        name="swa_decode",
    )(q3, z2, z2, kc, vc, cos, sin, perm, sink_row)


def _merge_body(x_ref, oa_ref, ob_ref, ga_ref, gb_ref, wa_ref, wb_ref, wo_ref, nw_ref, x1_ref, hn_ref):
    ta = _dot(oa_ref[...], wa_ref[...])
    tb = _dot(ob_ref[...], wb_ref[...])
    mg = _sigmoid(ga_ref[...].astype(F32)) * ta + _sigmoid(gb_ref[...].astype(F32)) * tb
    x1 = x_ref[...] + _dot(mg.astype(wo_ref.dtype), wo_ref[...])
    x1_ref[...] = x1
    hn_ref[...] = _rms(x1, nw_ref[...]).astype(hn_ref.dtype)


def _merge(x2d, oa, ob, z2, wa, wb, wo, nw, tm):
    m = x2d.shape[0]

    def const(shape):
        return pl.BlockSpec(shape, lambda i: (0, 0))

    return pl.pallas_call(
        _merge_body,
        grid=(m // tm,),
        in_specs=[pl.BlockSpec((tm, D_MODEL), lambda i: (i, 0)),
                  pl.BlockSpec((tm, HG_W), lambda i: (i, 0)),
                  pl.BlockSpec((tm, SWQ_W), lambda i: (i, 0)),
                  pl.BlockSpec((tm, D_MODEL), lambda i: (i, ZB_GA)),
                  pl.BlockSpec((tm, D_MODEL), lambda i: (i, ZB_GB)),
                  const(wa.shape), const(wb.shape), const(wo.shape), const((1, D_MODEL))],
        out_specs=[pl.BlockSpec((tm, D_MODEL), lambda i: (i, 0)),
                   pl.BlockSpec((tm, D_MODEL), lambda i: (i, 0))],
        out_shape=[jax.ShapeDtypeStruct((m, D_MODEL), F32),
                   jax.ShapeDtypeStruct((m, D_MODEL), wo.dtype)],
        compiler_params=_params("arbitrary"),
        name="merge",
    )(x2d, oa, ob, z2, z2, wa, wb, wo, nw)


FFN_CHUNKS = (768, 768, 768, 512)


def _ffn_body(hn_ref, x1_ref, wg_ref, wu_ref, wd_ref, o_ref, acc_ref):
    hn = hn_ref[...]
    off = 0
    for i, width in enumerate(FFN_CHUNKS):
        g = _dot(hn, wg_ref[:, off:off + width])
        u = _dot(hn, wu_ref[:, off:off + width])
        a = (g * _sigmoid(g) * u).astype(BF16)
        y = _dot(a, wd_ref[off:off + width, :])
        if i == 0:
            acc_ref[...] = x1_ref[...] + y
        else:
            acc_ref[...] += y
        off += width
    o_ref[...] = acc_ref[...]


def _ffn_dense(hn, x1, wg, wu, wd, tm):
    m = hn.shape[0]

    def const(shape):
        return pl.BlockSpec(shape, lambda i: (0, 0))

    return pl.pallas_call(
        _ffn_body,
        grid=(m // tm,),
        in_specs=[pl.BlockSpec((tm, D_MODEL), lambda i: (i, 0)),
                  pl.BlockSpec((tm, D_MODEL), lambda i: (i, 0)),
                  const(wg.shape), const(wu.shape), const(wd.shape)],
        out_specs=pl.BlockSpec((tm, D_MODEL), lambda i: (i, 0)),
        out_shape=jax.ShapeDtypeStruct((m, D_MODEL), F32),
        scratch_shapes=[pltpu.VMEM((tm, D_MODEL), F32)],
        compiler_params=_params("arbitrary"),
        name="ffn_dense",
    )(hn, x1, wg, wu, wd)


def _ffn_stream_body(hn_ref, x1_ref, wg_ref, wu_ref, wd_ref, o_ref):
    @pl.when(pl.program_id(0) == 0)
    def _():
        o_ref[...] = x1_ref[...]

    hn = hn_ref[...]
    g = _dot(hn, wg_ref[...])
    u = _dot(hn, wu_ref[...])
    o_ref[...] += _dot((g * _sigmoid(g) * u).astype(wd_ref.dtype), wd_ref[...])


def _ffn_stream(hn, x1, wg, wu, wd, tf):
    m = hn.shape[0]
    full = pl.BlockSpec((m, D_MODEL), lambda f: (0, 0))
    return pl.pallas_call(
        _ffn_stream_body,
        grid=(wg.shape[1] // tf,),
        in_specs=[full, full,
                  pl.BlockSpec((D_MODEL, tf), lambda f: (0, f)),
                  pl.BlockSpec((D_MODEL, tf), lambda f: (0, f)),
                  pl.BlockSpec((tf, D_MODEL), lambda f: (f, 0))],
        out_specs=full,
        out_shape=jax.ShapeDtypeStruct((m, D_MODEL), F32),
        compiler_params=_params("arbitrary"),
        name="ffn_stream",
    )(hn, x1, wg, wu, wd)


RT_E1, RT_E2, RT_G1, RT_G2, RT_R1, RT_R2 = range(6)
SEG_ALIGN = 16
EXPERT_CHUNKS = (1024, 1024, 1024, 512)


def _router_body(hn_ref, wr_ref, route_ref, route_t_ref, cnt_ref):
    T = hn_ref.shape[0]
    logits = _dot(hn_ref[...], wr_ref[...])
    lane = lax.broadcasted_iota(jnp.int32, logits.shape, 1).astype(F32)
    lg = jnp.where(lane < N_EXPERTS, logits, -jnp.inf)
    m1 = jnp.max(lg, axis=-1, keepdims=True)
    i1 = jnp.min(jnp.where(lg == m1, lane, float(LANES)), axis=-1, keepdims=True)
    lg2 = jnp.where(lane == i1, -jnp.inf, lg)
    m2 = jnp.max(lg2, axis=-1, keepdims=True)
    i2 = jnp.min(jnp.where(lg2 == m2, lane, float(LANES)), axis=-1, keepdims=True)
    e2 = jnp.exp(m2 - m1)
    sel = jnp.where((lane == i1) | (lane == i2), 1.0, 0.0)
    r = lax.broadcasted_iota(jnp.int32, (T, T), 0)
    c = lax.broadcasted_iota(jnp.int32, (T, T), 1)
    before = jnp.where(r > c, 1.0, 0.0).astype(BF16)
    rank = _dot(before, sel.astype(BF16))
    r1 = jnp.sum(jnp.where(lane == i1, rank, 0.0), axis=-1, keepdims=True)
    r2 = jnp.sum(jnp.where(lane == i2, rank, 0.0), axis=-1, keepdims=True)
    fields = {RT_E1: i1, RT_E2: i2, RT_G1: 1.0 / (1.0 + e2), RT_G2: e2 / (1.0 + e2), RT_R1: r1, RT_R2: r2}
    route = jnp.zeros_like(logits)
    for idx, val in fields.items():
        route = jnp.where(lane == float(idx), val, route)
    route_ref[...] = route
    route_t_ref[...] = route.T
    cnt_ref[...] = jnp.broadcast_to(jnp.sum(sel, axis=0, keepdims=True), cnt_ref.shape)


def _router(hn, wr, T):
    m = hn.shape[0]
    nt = m // T
    return pl.pallas_call(
        _router_body,
        grid=(nt,),
        in_specs=[pl.BlockSpec((T, D_MODEL), lambda i: (i, 0)),
                  pl.BlockSpec((D_MODEL, LANES), lambda i: (0, 0))],
        out_specs=[pl.BlockSpec((T, LANES), lambda i: (i, 0)),
                   pl.BlockSpec((LANES, T), lambda i: (0, i)),
                   pl.BlockSpec((None, SUBLANES, LANES), lambda i: (i, 0, 0))],
        out_shape=[jax.ShapeDtypeStruct((m, LANES), F32),
                   jax.ShapeDtypeStruct((LANES, m), F32),
                   jax.ShapeDtypeStruct((nt, SUBLANES, LANES), F32)],
        compiler_params=_params("arbitrary"),
        name="router",
    )(hn, wr)


def _segment_copies(seg_ref, so_ref, off_ref, step, make, fn, max_rows):
    sizes = []
    s = max_rows
    while s >= SEG_ALIGN:
        sizes.append(s)
        s //= 2
    for e in range(N_EXPERTS):
        n = seg_ref[step * N_EXPERTS + e]
        src = so_ref[step * N_EXPERTS + e]
        dst = off_ref[step * N_EXPERTS + e]
        for s in sizes:
            @pl.when((n & s) != 0)
            def _(s=s, n=n, src=src, dst=dst):
                done = n & (-2 * s)
                fn(make(pl.multiple_of(src + done, SEG_ALIGN), pl.multiple_of(dst + done, SEG_ALIGN), s))


def _stage_offset(so_ref, step, expert):
    out = jnp.zeros_like(expert)
    for e in range(N_EXPERTS):
        out = jnp.where(expert == float(e), so_ref[step * N_EXPERTS + e].astype(F32), out)
    return out


def _dispatch_body(seg_ref, so_ref, off_ref, hn_ref, rt_ref, init_ref, xs_ref, stage, sem):
    del init_ref
    t = pl.program_id(0)
    nt = pl.num_programs(0)
    T = hn_ref.shape[0]
    S = stage.shape[1]
    slot = t % 2

    def copies(step, slot, fn):
        def make(src, dst, rows):
            return pltpu.make_async_copy(stage.at[slot, pl.ds(src, rows)], xs_ref.at[pl.ds(dst, rows)], sem.at[slot])
        _segment_copies(seg_ref, so_ref, off_ref, step, make, fn, T)

    @pl.when(t >= 2)
    def _():
        copies(t - 2, slot, lambda c: c.wait())

    rt = rt_ref[...]
    tgt1 = _stage_offset(so_ref, t, rt[RT_E1:RT_E1 + 1, :]) + rt[RT_R1:RT_R1 + 1, :]
    tgt2 = _stage_offset(so_ref, t, rt[RT_E2:RT_E2 + 1, :]) + rt[RT_R2:RT_R2 + 1, :]
    rowi = lax.broadcasted_iota(jnp.int32, (S, T), 0).astype(F32)
    sel = jnp.where((rowi == tgt1) | (rowi == tgt2), 1.0, 0.0).astype(BF16)
    stage[slot] = _dot(sel, hn_ref[...]).astype(BF16)
    copies(t, slot, lambda c: c.start())

    @pl.when(t == nt - 1)
    def _():
        copies(t, slot, lambda c: c.wait())

        @pl.when(t >= 1)
        def _():
            copies(t - 1, 1 - slot, lambda c: c.wait())


def _dispatch(hn, route_t, plan, T, S, rows_total):
    m = hn.shape[0]
    grid_spec = pltpu.PrefetchScalarGridSpec(
        num_scalar_prefetch=3,
        grid=(m // T,),
        in_specs=[pl.BlockSpec((T, D_MODEL), lambda t, *_: (t, 0)),
                  pl.BlockSpec((LANES, T), lambda t, *_: (0, t)),
                  pl.BlockSpec(memory_space=pl.ANY)],
        out_specs=pl.BlockSpec(memory_space=pl.ANY),
        scratch_shapes=[pltpu.VMEM((2, S, D_MODEL), BF16), pltpu.SemaphoreType.DMA((2,))],
    )
    return pl.pallas_call(
        _dispatch_body,
        grid_spec=grid_spec,
        out_shape=jax.ShapeDtypeStruct((rows_total, D_MODEL), BF16),
        input_output_aliases={5: 0},
        compiler_params=_params("arbitrary"),
        name="moe_dispatch",
    )(plan["seg"], plan["so"], plan["off"], hn, route_t, jnp.zeros((rows_total, D_MODEL), BF16))


def _experts_body(owner_ref, used_ref, x_ref, wg_ref, wu_ref, wd_ref, y_ref, acc_ref):
    del owner_ref

    @pl.when(pl.program_id(0) < used_ref[0])
    def _():
        x = x_ref[...]
        off = 0
        for i, width in enumerate(EXPERT_CHUNKS):
            g = _dot(x, wg_ref[:, off:off + width])
            u = _dot(x, wu_ref[:, off:off + width])
            y = _dot((g * _sigmoid(g) * u).astype(BF16), wd_ref[off:off + width, :])
            if i == 0:
                acc_ref[...] = y
            else:
                acc_ref[...] += y
            off += width
        y_ref[...] = acc_ref[...].astype(y_ref.dtype)

    @pl.when(pl.program_id(0) >= used_ref[0])
    def _():
        y_ref[...] = jnp.zeros_like(y_ref)


def _experts(xs, plan, wg, wu, wd, bm):
    nblk = xs.shape[0] // bm

    def wspec(shape):
        return pl.BlockSpec((None,) + shape, lambda b, owner, used: (owner[b], 0, 0), pipeline_mode=pl.Buffered(1))

    grid_spec = pltpu.PrefetchScalarGridSpec(
        num_scalar_prefetch=2,
        grid=(nblk,),
        in_specs=[pl.BlockSpec((bm, D_MODEL), lambda b, owner, used: (jnp.minimum(b, used[0] - 1), 0)),
                  wspec((D_MODEL, D_FF_EXPERT)), wspec((D_MODEL, D_FF_EXPERT)), wspec((D_FF_EXPERT, D_MODEL))],
        out_specs=pl.BlockSpec((bm, D_MODEL), lambda b, owner, used: (b, 0)),
        scratch_shapes=[pltpu.VMEM((bm, D_MODEL), F32)],
    )
    return pl.pallas_call(
        _experts_body,
        grid_spec=grid_spec,
        out_shape=jax.ShapeDtypeStruct(xs.shape, BF16),
        compiler_params=_params("arbitrary"),
        name="moe_experts",
    )(plan["owner"], plan["used"], xs, wg, wu, wd)


def _combine_body(seg_ref, so_ref, off_ref, route_ref, x1_ref, nw_ref, ys_ref, o_ref, ybuf, sem, *, final_norm):
    t = pl.program_id(0)
    nt = pl.num_programs(0)
    T = route_ref.shape[0]
    S = ybuf.shape[1]
    slot = t % 2

    def copies(step, slot, fn):
        def make(stage_row, grouped_row, rows):
            return pltpu.make_async_copy(ys_ref.at[pl.ds(grouped_row, rows)], ybuf.at[slot, pl.ds(stage_row, rows)],
                                         sem.at[slot])
        _segment_copies(seg_ref, so_ref, off_ref, step, make, fn, T)

    @pl.when(t == 0)
    def _():
        ybuf[...] = jnp.zeros_like(ybuf)
        copies(0, 0, lambda c: c.start())

    @pl.when(t + 1 < nt)
    def _():
        copies(t + 1, 1 - slot, lambda c: c.start())

    copies(t, slot, lambda c: c.wait())

    route = route_ref[...]
    tgt1 = _stage_offset(so_ref, t, route[:, RT_E1:RT_E1 + 1]) + route[:, RT_R1:RT_R1 + 1]
    tgt2 = _stage_offset(so_ref, t, route[:, RT_E2:RT_E2 + 1]) + route[:, RT_R2:RT_R2 + 1]
    coli = lax.broadcasted_iota(jnp.int32, (T, S), 1).astype(F32)
    sel = (jnp.where(coli == tgt1, route[:, RT_G1:RT_G1 + 1], 0.0)
           + jnp.where(coli == tgt2, route[:, RT_G2:RT_G2 + 1], 0.0))
    x2 = x1_ref[...] + _dot(sel.astype(BF16), ybuf[slot])
    o_ref[...] = _rms(x2, nw_ref[...]) if final_norm else x2


def _combine(ys, route, x1, nw, plan, T, S, final_norm):
    m = x1.shape[0]
    grid_spec = pltpu.PrefetchScalarGridSpec(
        num_scalar_prefetch=3,
        grid=(m // T,),
        in_specs=[pl.BlockSpec((T, LANES), lambda t, *_: (t, 0)),
                  pl.BlockSpec((T, D_MODEL), lambda t, *_: (t, 0)),
                  pl.BlockSpec((1, D_MODEL), lambda t, *_: (0, 0)),
                  pl.BlockSpec(memory_space=pl.ANY)],
        out_specs=pl.BlockSpec((T, D_MODEL), lambda t, *_: (t, 0)),
        scratch_shapes=[pltpu.VMEM((2, S, D_MODEL), BF16), pltpu.SemaphoreType.DMA((2,))],
    )
    return pl.pallas_call(
        functools.partial(_combine_body, final_norm=final_norm),
        grid_spec=grid_spec,
        out_shape=jax.ShapeDtypeStruct((m, D_MODEL), F32),
        compiler_params=_params("arbitrary"),
        name="moe_combine",
    )(plan["seg"], plan["so"], plan["off"], route, x1, nw, ys)


def _moe_plan(cnt, bm, nblk):
    seg = (cnt + (SEG_ALIGN - 1)) // SEG_ALIGN * SEG_ALIGN
    so = jnp.cumsum(seg, axis=1) - seg
    blocks = (jnp.sum(seg, axis=0) + (bm - 1)) // bm
    blk_end = jnp.cumsum(blocks)
    off = (blk_end - blocks)[None, :] * bm + jnp.cumsum(seg, axis=0) - seg
    owner = jnp.minimum(jnp.sum(jnp.arange(nblk)[:, None] >= blk_end[None, :], axis=1), N_EXPERTS - 1)
    as_i32 = lambda a: a.astype(jnp.int32).reshape(-1)
    return {"seg": as_i32(seg), "so": as_i32(so), "off": as_i32(off), "owner": as_i32(owner),
            "used": as_i32(blk_end[-1:])}


def _moe_ffn(hn, x1, nw, wr, wg, wu, wd, T, bm, final_norm):
    m = hn.shape[0]
    tiles = m // T
    pad_rows = tiles * N_EXPERTS * (SEG_ALIGN - 1)
    stage_rows = -(-(2 * T + N_EXPERTS * (SEG_ALIGN - 1)) // LANES) * LANES
    nblk = -(-(2 * m + pad_rows) // bm) + N_EXPERTS
    route, route_t, cnt = _router(hn, wr, T)
    plan = _moe_plan(cnt[:, 0, :N_EXPERTS].astype(jnp.int32), bm, nblk)
    xs = _dispatch(hn.astype(BF16), route_t, plan, T, stage_rows, nblk * bm)
    ys = _experts(xs, plan, wg, wu, wd, bm)
    return _combine(ys, route, x1, nw, plan, T, stage_rows, final_norm)


def _norm_body(x_ref, nw_ref, o_ref):
    o_ref[...] = _rms(x_ref[...], nw_ref[...])


def _norm(x, nw, tm):
    m = x.shape[0]
    spec = pl.BlockSpec((tm, D_MODEL), lambda i: (i, 0))
    return pl.pallas_call(
        _norm_body, grid=(m // tm,),
        in_specs=[spec, pl.BlockSpec((1, D_MODEL), lambda i: (0, 0))],
        out_specs=spec, out_shape=jax.ShapeDtypeStruct((m, D_MODEL), F32),
        compiler_params=_params("arbitrary"), name="norm",
    )(x, nw)


def _layout_w_in(w):
    o = 0
    hg = w[:, o:o + 4 * HG_W]; o += 4 * HG_W
    sq = w[:, o:o + SW_Q_HEADS * SW_HD]; o += SW_Q_HEADS * SW_HD
    sk = w[:, o:o + KV_W]; o += KV_W
    sv = w[:, o:o + KV_W]; o += KV_W
    ga = w[:, o:o + D_MODEL]; o += D_MODEL
    gb = w[:, o:o + D_MODEL]
    zero = jnp.zeros((w.shape[0], SW_HD), w.dtype)
    groups = []
    for h in range(SW_Q_HEADS):
        qh = sq[:, h * SW_HD:(h + 1) * SW_HD]
        groups += [qh, zero] if h // SW_GROUP == 0 else [zero, qh]
    return jnp.concatenate([hg] + groups + [ga, gb, sk, sv], axis=1)


def _layout_w_branch_b(w):
    zero = jnp.zeros((SW_HD, w.shape[1]), w.dtype)
    groups = []
    for h in range(SW_Q_HEADS):
        wh = w[h * SW_HD:(h + 1) * SW_HD]
        groups += [wh, zero] if h // SW_GROUP == 0 else [zero, wh]
    return jnp.concatenate(groups, axis=0)


def _rope_tables(pos):
    half = SW_HD // 2
    inv = ROPE_THETA ** (-jnp.arange(half, dtype=F32) / half)
    ang = pos.astype(F32)[:, None] * inv[None, :]
    cos = jnp.cos(ang)
    sin = jnp.sin(ang)
    reps = LANES // SW_HD
    return jnp.tile(cos, (1, 2 * reps)), jnp.tile(jnp.concatenate([-sin, sin], axis=1), (1, reps))


def _rotate_half_matrix():
    j = jnp.arange(LANES)
    src = jnp.where((j % SW_HD) < SW_HD // 2, j + SW_HD // 2, j - SW_HD // 2)
    return (jnp.arange(LANES)[:, None] == src[None, :]).astype(F32)


def _tile(m, pref):
    return pref if m % pref == 0 else m


def kernel(x_prompt, x_sample, state_hgrn, cache_swa_k, cache_swa_v, norm_mix, w_in, hg_lb_logits, hg_norm,
           swa_sinks, w_branch_a, w_branch_b, w_out, norm_ffn, w_gate_dense, w_up_dense, w_down_dense,
           w_router, w_gate_moe, w_up_moe, w_down_moe, norm_final):
    depth = w_in.shape[0]
    bsz, seq, _ = x_prompt.shape
    nsamp = x_sample.shape[0]
    mp = bsz * seq
    wb = cache_swa_k.shape[2]

    cos_p, sin_p = _rope_tables(jnp.arange(seq))
    cos_s, sin_s = _rope_tables(PAST_LEN + jnp.arange(1))
    perm32 = _rotate_half_matrix()
    perm = perm32.astype(BF16)
    lbl = hg_lb_logits.astype(F32)
    state32 = state_hgrn.astype(F32)
    kcache = cache_swa_k.reshape(depth, nsamp, wb, KV_W).astype(F32)
    vcache = cache_swa_v.reshape(depth, nsamp, wb, KV_W).astype(F32)

    xp = x_prompt.reshape(mp, D_MODEL)
    xs = x_sample.reshape(nsamp, D_MODEL)
    outs = {k: [] for k in ("sp", "kp", "vp", "ss", "ks", "vs")}
    nfin = norm_final.reshape(1, D_MODEL)
    normed = False

    for l in range(depth):
        w_in32 = _layout_w_in(w_in[l])
        wa32 = w_branch_a[l]
        wbb32 = _layout_w_branch_b(w_branch_b[l])
        wo32 = w_out[l]
        w_in_l = w_in32.astype(BF16)
        wa = wa32.astype(BF16)
        wbb = wbb32.astype(BF16)
        wo = wo32.astype(BF16)
        nmix = norm_mix[l].reshape(1, D_MODEL)
        nffn = norm_ffn[l].reshape(1, D_MODEL)
        hgn = hg_norm[l].reshape(1, HG_D)
        sinks = swa_sinks[l].astype(F32)

        zp = _proj(xp, nmix, w_in_l, _tile(mp, 1024), 1792)
        zp3 = zp.reshape(bsz, seq, Z_WIDTH)
        oa_p, s_p = _hgrn_prompt(zp3, lbl, hgn, l)
        ob_p, k_p, v_p = _swa_prompt(zp3, sinks, cos_p, sin_p, perm)
        x1p, hnp = _merge(xp, oa_p.reshape(mp, HG_W), ob_p.reshape(mp, SWQ_W), zp, wa, wbb, wo, nffn,
                          _tile(mp, 512))

        zs = _proj(xs, nmix, w_in32, nsamp, 896)
        oa_s, s_s = _hgrn_decode(zs, state32, lbl, hgn, l)
        q3 = zs[:, ZB_SWQ * SWQ_W:(ZB_SWQ + 1) * SWQ_W].reshape(nsamp, SW_Q_HEADS, LANES)
        kc, vc = kcache[l], vcache[l]
        sink_row = jnp.pad(sinks, (0, LANES - SW_Q_HEADS)).reshape(1, LANES)
        ob_s, k_s, v_s = _swa_decode(q3, zs, kcache, vcache, l, cos_s, sin_s, perm32, sink_row)
        x1s, hns = _merge(xs, oa_s, ob_s.reshape(nsamp, SWQ_W), zs, wa32, wbb32, wo32, nffn, nsamp)

        j = l // 2
        if l % 2 == 0:
            xp = _ffn_dense(hnp, x1p, w_gate_dense[j].astype(BF16), w_up_dense[j].astype(BF16),
                            w_down_dense[j].astype(BF16), _tile(mp, 512))
            xs = _ffn_stream(hns, x1s, w_gate_dense[j], w_up_dense[j], w_down_dense[j], 256)
        else:
            wr32 = jnp.pad(w_router[j], ((0, 0), (0, LANES - N_EXPERTS)))
            wg = w_gate_moe[j].astype(BF16)
            wu = w_up_moe[j].astype(BF16)
            wd = w_down_moe[j].astype(BF16)
            last = l == depth - 1
            xp = _moe_ffn(hnp, x1p, nfin, wr32.astype(BF16), wg, wu, wd, _tile(mp, 512), 512, last)
            xs = _moe_ffn(hns, x1s, nfin, wr32, wg, wu, wd, nsamp, 128, last)
            normed = last

        outs["sp"].append(s_p)
        outs["kp"].append(k_p.reshape(bsz, WINDOW, SW_KV_HEADS, SW_HD))
        outs["vp"].append(v_p.reshape(bsz, WINDOW, SW_KV_HEADS, SW_HD))
        outs["ss"].append(s_s.astype(state_hgrn.dtype))
        k_rows = jnp.concatenate([kc[:, 1:], k_s[:, None, :]], axis=1)
        v_rows = jnp.concatenate([vc[:, 1:], v_s[:, None, :]], axis=1)
        outs["ks"].append(k_rows.reshape(nsamp, wb, SW_KV_HEADS, SW_HD))
        outs["vs"].append(v_rows.reshape(nsamp, wb, SW_KV_HEADS, SW_HD))

    yp, ys = (xp, xs) if normed else (_norm(xp, nfin, _tile(mp, 1024)), _norm(xs, nfin, nsamp))

    return (yp.reshape(x_prompt.shape), ys.reshape(x_sample.shape),
            jnp.stack(outs["sp"]), jnp.stack(outs["kp"]), jnp.stack(outs["vp"]),
            jnp.stack(outs["ss"]), jnp.stack(outs["ks"]), jnp.stack(outs["vs"]))
```

```python
import functools

import jax
import jax.numpy as jnp
from jax import lax
from jax.experimental import pallas as pl
from jax.experimental.pallas import tpu as pltpu

F32 = jnp.float32
BF16 = jnp.bfloat16

D_MODEL = 1024
PAST_LEN = 16384
HG_HEADS = 4
HG_D = 128
HG_W = HG_HEADS * HG_D
SW_Q_HEADS = 8
SW_KV_HEADS = 2
SW_HD = 64
SW_GROUP = SW_Q_HEADS // SW_KV_HEADS
WINDOW = 128
ROPE_THETA = 10000.0
D_FF_DENSE = 2816
N_EXPERTS = 8
D_FF_EXPERT = 3584
RMS_EPS = 1e-6

LANES = 128
SUBLANES = 8
VMEM_LIMIT = 56 * 1024 * 1024

SWQ_W = SW_Q_HEADS * SW_HD
SW_PAIRS = SWQ_W // LANES
KV_W = SW_KV_HEADS * SW_HD
Z_WIDTH = 4 * HG_W + 2 * D_MODEL + SWQ_W + 2 * KV_W
ZB_Q, ZB_F, ZB_I, ZB_G = 0, 1, 2, 3
ZB_GA, ZB_GB = 2, 3
ZB_SWQ = 8
ZB_K, ZB_V = 36, 37
PROJ_CHUNKS = (1280, 1280, 1280, 1024)

HG_CHUNK = 128
HG_UNROLL = 4
SW_UNROLL = 4
SEQ_BLOCK = 512
DEC_TILE = 8


def _dot_dims(a, b, dims):
    precision = lax.Precision.HIGHEST if a.dtype == F32 else None
    return lax.dot_general(a, b, (dims, ((), ())), precision=precision, preferred_element_type=F32)


def _dot(a, b):
    return _dot_dims(a, b, ((1,), (0,)))


def _dot_nt(a, b):
    return _dot_dims(a, b, ((1,), (1,)))


def _dot_tn(a, b):
    return _dot_dims(a, b, ((0,), (0,)))


def _sigmoid(x):
    return 1.0 / (1.0 + jnp.exp(-x))


def _rms(x, w):
    ms = jnp.mean(x * x, axis=-1, keepdims=True)
    return x * lax.rsqrt(ms + RMS_EPS) * w


def _params(*sem):
    return pltpu.CompilerParams(dimension_semantics=sem, vmem_limit_bytes=VMEM_LIMIT)


def _proj_body(x_ref, nw_ref, w_ref, z_ref):
    h = _rms(x_ref[...], nw_ref[...]).astype(w_ref.dtype)
    off = 0
    for width in PROJ_CHUNKS:
        z_ref[:, off:off + width] = _dot(h, w_ref[:, off:off + width]).astype(z_ref.dtype)
        off += width


def _proj(x2d, nw, w, tm):
    m = x2d.shape[0]
    n = w.shape[1]
    return pl.pallas_call(
        _proj_body,
        grid=(m // tm,),
        in_specs=[
            pl.BlockSpec((tm, D_MODEL), lambda i: (i, 0)),
            pl.BlockSpec((1, D_MODEL), lambda i: (0, 0)),
            pl.BlockSpec((D_MODEL, n), lambda i: (0, 0), pipeline_mode=pl.Buffered(1)),
        ],
        out_specs=pl.BlockSpec((tm, n), lambda i: (i, 0)),
        out_shape=jax.ShapeDtypeStruct((m, n), w.dtype),
        compiler_params=_params("arbitrary"),
        name="proj",
    )(x2d, nw, w)


def _lower_bound(lbl, layer):
    mx = jnp.max(lbl, axis=0, keepdims=True)
    e = jnp.exp(lbl - mx)
    sm = e / jnp.sum(e, axis=0, keepdims=True)
    cum = sm[0:1, :]
    for i in range(1, layer + 1):
        cum = cum + sm[i:i + 1, :]
    return cum - sm[0:1, :]


def _split3(x):
    hi = x.astype(BF16)
    r = x - hi.astype(F32)
    mid = r.astype(BF16)
    lo = (r - mid.astype(F32)).astype(BF16)
    return hi, mid, lo


def _hgrn_prompt_body(zq_ref, zf_ref, zi_ref, zg_ref, lbl_ref, nw_ref, o_ref, s_ref, st_scr, b_all_scr, *, layer):
    c = pl.program_id(1)
    C = HG_CHUNK

    @pl.when(c == 0)
    def _():
        st_scr[...] = jnp.zeros_like(st_scr)

    lb = _lower_bound(lbl_ref[...], layer)
    row = lax.broadcasted_iota(jnp.int32, (C, C), 0)
    col = lax.broadcasted_iota(jnp.int32, (C, C), 1)
    xr = row ^ col
    tri = jnp.where(row >= col, 1.0, 0.0).astype(BF16)
    sub4 = (lax.broadcasted_iota(jnp.int32, (SUBLANES, HG_D), 0) & 4) == 0
    levels = (1, 2, 4, 8, 16, 32, 64)
    pair_level = {m: (xr >= m) & (xr < 2 * m) for m in levels}
    upper_half = {m: (row & m) != 0 for m in levels if m < SUBLANES}
    nw = nw_ref[...]

    def chunks(i, carry):
        for u in range(HG_UNROLL):
            chunk(pl.ds(pl.multiple_of((i * HG_UNROLL + u) * C, C), C), b_all_scr.at[u])
        return carry

    def chunk(rows, b_scr):
        hf = zf_ref[rows, :].astype(F32)
        fg_all = lb + (1.0 - lb) * _sigmoid(hf)
        g_all = jnp.log2(fg_all)
        k_all = 1.0 - fg_all
        g1, g2, g3 = _split3(g_all)
        b_all = _dot(tri, g1) + _dot(tri, g2) + _dot(tri, g3)
        b_scr[...] = b_all
        hq = zq_ref[rows, :].astype(F32)
        q_all = hq * _sigmoid(hq) * (HG_D ** -0.5)
        gate = zg_ref[rows, :].astype(F32)
        gate_all = gate * _sigmoid(gate)
        for h in range(HG_HEADS):
            sl = slice(h * HG_D, (h + 1) * HG_D)
            head(h, rows, b_scr, b_all[:, sl], q_all[:, sl], k_all[:, sl], fg_all[:, sl], gate_all[:, sl])

    def head(h, rows, b_scr, b, q, k, f, gate):
        sl = slice(h * HG_D, (h + 1) * HG_D)
        v = zi_ref[rows, sl]

        acc = _dot_nt(q.astype(BF16), k.astype(BF16))
        for m in levels:
            if m == 1:
                w = jnp.where(upper_half[m], q * f, k)
            elif m < SUBLANES:
                if m == 2:
                    pieces = []
                    for j in range(C // SUBLANES):
                        lo = jnp.broadcast_to(b_scr[SUBLANES * j + 1:SUBLANES * j + 2, sl], (SUBLANES, HG_D))
                        hi = jnp.broadcast_to(b_scr[SUBLANES * j + 5:SUBLANES * j + 6, sl], (SUBLANES, HG_D))
                        pieces.append(jnp.where(sub4, lo, hi))
                else:
                    pieces = [jnp.broadcast_to(b_scr[i * 2 * m + m - 1:i * 2 * m + m, sl], (2 * m, HG_D))
                              for i in range(C // (2 * m))]
                d = b - jnp.concatenate(pieces, axis=0)
                w = jnp.where(upper_half[m], q, k) * jnp.exp2(jnp.where(upper_half[m], d, -d))
            else:
                expo, qk = [], []
                for i in range(C // (2 * m)):
                    lo, mid, hi = i * 2 * m, i * 2 * m + m, (i + 1) * 2 * m
                    bref = b_scr[mid - 1:mid, sl]
                    expo += [bref - b[lo:mid], b[mid:hi] - bref]
                    qk += [k[lo:mid], q[mid:hi]]
                w = jnp.concatenate(qk, axis=0) * jnp.exp2(jnp.concatenate(expo, axis=0))
            wb = w.astype(BF16)
            acc = jnp.where(pair_level[m], _dot_nt(wb, wb), acc)
        a = jnp.where(row >= col, acc, 0.0)

        st = st_scr[h]
        o = _dot(a.astype(BF16), v) + _dot_nt((q * jnp.exp2(b)).astype(BF16), st.astype(BF16))
        b_last = b[C - 1:C, :]
        kd = k * jnp.exp2(b_last - b)
        st_scr[h] = jnp.exp2(b_last) * st + _dot_tn(v, kd.astype(BF16))

        o_ref[rows, sl] = (_rms(o, nw) * gate).astype(o_ref.dtype)

    lax.fori_loop(0, zq_ref.shape[0] // (C * HG_UNROLL), chunks, 0)

    @pl.when(c == pl.num_programs(1) - 1)
    def _():
        for h in range(HG_HEADS):
            s_ref[h] = st_scr[h].T


def _hgrn_prompt(z3, lbl, nw, layer):
    bsz, seq, _ = z3.shape
    C = HG_CHUNK
    rows = _tile(seq, SEQ_BLOCK)

    def zspec(blk):
        return pl.BlockSpec((None, rows, HG_W), lambda b, c: (b, c, blk))

    return pl.pallas_call(
        functools.partial(_hgrn_prompt_body, layer=layer),
        grid=(bsz, seq // rows),
        in_specs=[zspec(ZB_Q), zspec(ZB_F), zspec(ZB_I), zspec(ZB_G),
                  pl.BlockSpec(lbl.shape, lambda b, c: (0, 0)),
                  pl.BlockSpec((1, HG_D), lambda b, c: (0, 0))],
        out_specs=[pl.BlockSpec((None, rows, HG_W), lambda b, c: (b, c, 0)),
                   pl.BlockSpec((None, HG_HEADS, HG_D, HG_D), lambda b, c: (b, 0, 0, 0))],
        out_shape=[jax.ShapeDtypeStruct((bsz, seq, HG_W), BF16),
                   jax.ShapeDtypeStruct((bsz, HG_HEADS, HG_D, HG_D), F32)],
        scratch_shapes=[pltpu.VMEM((HG_HEADS, HG_D, HG_D), F32), pltpu.VMEM((HG_UNROLL, C, HG_W), F32)],
        compiler_params=_params("arbitrary", "arbitrary"),
        name="hgrn_prompt",
    )(z3, z3, z3, z3, lbl, nw)


def _hgrn_decode_body(zq_ref, zf_ref, zi_ref, zg_ref, lbl_ref, nw_ref, s_ref, o_ref, sn_ref, o_scr, *, layer):
    lb = _lower_bound(lbl_ref[...], layer)
    fg = lb + (1.0 - lb) * _sigmoid(zf_ref[...].astype(F32))
    kk = 1.0 - fg
    hq = zq_ref[...].astype(F32)
    q = hq * _sigmoid(hq) * (HG_D ** -0.5)
    v = zi_ref[...].astype(F32)
    eye = (lax.broadcasted_iota(jnp.int32, (HG_D, HG_D), 0)
           == lax.broadcasted_iota(jnp.int32, (HG_D, HG_D), 1))

    def column(r):
        return jnp.sum(jnp.where(eye, jnp.broadcast_to(r, (HG_D, HG_D)), 0.0), axis=1, keepdims=True)

    for i in range(DEC_TILE):
        for h in range(HG_HEADS):
            sl = slice(h * HG_D, (h + 1) * HG_D)
            sn = column(fg[i:i + 1, sl]) * s_ref[i, h] + column(kk[i:i + 1, sl]) * v[i:i + 1, sl]
            sn_ref[i, h] = sn
            o_scr[i:i + 1, sl] = jnp.sum(column(q[i:i + 1, sl]) * sn, axis=0, keepdims=True)

    gate = zg_ref[...].astype(F32)
    gate = gate * _sigmoid(gate)
    nw = nw_ref[...]
    for h in range(HG_HEADS):
        sl = slice(h * HG_D, (h + 1) * HG_D)
        o_ref[:, sl] = (_rms(o_scr[:, sl], nw) * gate[:, sl]).astype(o_ref.dtype)


def _hgrn_decode(z2, state, lbl, nw, layer):
    n = z2.shape[0]
    T = DEC_TILE

    def zspec(blk):
        return pl.BlockSpec((T, HG_W), lambda i: (i, blk))

    sblock = (T, HG_HEADS, HG_D, HG_D)
    return pl.pallas_call(
        functools.partial(_hgrn_decode_body, layer=layer),
        grid=(n // T,),
        in_specs=[zspec(ZB_Q), zspec(ZB_F), zspec(ZB_I), zspec(ZB_G),
                  pl.BlockSpec(lbl.shape, lambda i: (0, 0)),
                  pl.BlockSpec((1, HG_D), lambda i: (0, 0)),
                  pl.BlockSpec((None,) + sblock, lambda i: (layer, i, 0, 0, 0))],
        out_specs=[pl.BlockSpec((T, HG_W), lambda i: (i, 0)), pl.BlockSpec(sblock, lambda i: (i, 0, 0, 0))],
        out_shape=[jax.ShapeDtypeStruct((n, HG_W), z2.dtype),
                   jax.ShapeDtypeStruct(state.shape[1:], F32)],
        scratch_shapes=[pltpu.VMEM((T, HG_W), F32)],
        compiler_params=_params("arbitrary"),
        name="hgrn_decode",
    )(z2, z2, z2, z2, lbl, nw, state)


def _rope(x, cos, sin, perm):
    return x.astype(F32) * cos + _dot(x, perm) * sin


def _swa_prompt_body(sinks_ref, zq_ref, zk_ref, zv_ref, cos_ref, sin_ref, perm_ref,
                     o_ref, kr_ref, vr_ref, kprev, vprev):
    n = pl.program_id(1)
    W = WINDOW

    @pl.when(n == 0)
    def _():
        kprev[...] = jnp.zeros_like(kprev)
        vprev[...] = jnp.zeros_like(vprev)

    perm = perm_ref[...]
    qi = lax.broadcasted_iota(jnp.int32, (W, 2 * W), 0)
    kj = lax.broadcasted_iota(jnp.int32, (W, 2 * W), 1)
    band = (kj >= qi) & (kj <= qi + W)
    low_half = lax.broadcasted_iota(jnp.int32, (W, LANES), 1) < SW_HD
    heads = range(SW_Q_HEADS)
    scale = SW_HD ** -0.5

    def blocks(i, carry):
        for u in range(SW_UNROLL):
            block(i * SW_UNROLL + u)
        return carry

    def block(i):
        rows = pl.ds(pl.multiple_of(i * W, W), W)
        cos = cos_ref[rows, :]
        sin = sin_ref[rows, :]
        k_rot = _rope(zk_ref[rows, :], cos, sin, perm)
        kr_ref[...] = k_rot
        vr_ref[...] = zv_ref[rows, :].astype(F32)
        k_cur = k_rot.astype(BF16)
        v_cur = zv_ref[rows, :]
        kk = jnp.concatenate([kprev[...], k_cur], axis=0)
        vv = jnp.concatenate([vprev[...], v_cur], axis=0)
        valid = band & (((n > 0) | (i > 0)) | (kj >= W))

        q_in = [zq_ref[rows, j * LANES:(j + 1) * LANES] for j in range(SW_PAIRS)]
        rot = _dot(jnp.concatenate(q_in, axis=0), perm)
        pair = [(q_in[j].astype(F32) * cos + rot[j * W:(j + 1) * W] * sin) * scale for j in range(SW_PAIRS)]
        q_rot = [jnp.where(low_half if h < SW_PAIRS else ~low_half, pair[h % SW_PAIRS], 0.0).astype(BF16)
                 for h in heads]
        s_all = _dot_nt(jnp.concatenate(q_rot, axis=0), kk)
        p_all, den_all = [], []
        for h in heads:
            s = jnp.where(valid, s_all[h * W:(h + 1) * W], -jnp.inf)
            sink = sinks_ref[h]
            mx = jnp.maximum(jnp.max(s, axis=-1, keepdims=True), sink)
            p = jnp.exp(s - mx)
            den_all.append(jnp.sum(p, axis=-1, keepdims=True) + jnp.exp(sink - mx))
            p_all.append(p.astype(BF16))
        o_all = _dot(jnp.concatenate(p_all, axis=0), vv)
        o_head = [o_all[h * W:(h + 1) * W] / den_all[h] for h in heads]
        for j in range(SW_PAIRS):
            o_pair = jnp.where(low_half, o_head[j], o_head[j + SW_PAIRS])
            o_ref[rows, j * LANES:(j + 1) * LANES] = o_pair.astype(o_ref.dtype)
        kprev[...] = k_cur
        vprev[...] = v_cur

    lax.fori_loop(0, zq_ref.shape[0] // (W * SW_UNROLL), blocks, 0)


def _swa_prompt(z3, sinks, cos, sin, perm):
    bsz, seq, _ = z3.shape
    W = WINDOW
    rows = _tile(seq, SEQ_BLOCK)
    grid_spec = pltpu.PrefetchScalarGridSpec(
        num_scalar_prefetch=1,
        grid=(bsz, seq // rows),
        in_specs=[pl.BlockSpec((None, rows, SWQ_W), lambda b, n, s: (b, n, ZB_SWQ)),
                  pl.BlockSpec((None, rows, KV_W), lambda b, n, s: (b, n, ZB_K)),
                  pl.BlockSpec((None, rows, KV_W), lambda b, n, s: (b, n, ZB_V)),
                  pl.BlockSpec((rows, LANES), lambda b, n, s: (n, 0)),
                  pl.BlockSpec((rows, LANES), lambda b, n, s: (n, 0)),
                  pl.BlockSpec((LANES, LANES), lambda b, n, s: (0, 0))],
        out_specs=[pl.BlockSpec((None, rows, SWQ_W), lambda b, n, s: (b, n, 0)),
                   pl.BlockSpec((None, W, KV_W), lambda b, n, s: (b, 0, 0)),
                   pl.BlockSpec((None, W, KV_W), lambda b, n, s: (b, 0, 0))],
        scratch_shapes=[pltpu.VMEM((W, KV_W), BF16), pltpu.VMEM((W, KV_W), BF16)],
    )
    return pl.pallas_call(
        _swa_prompt_body,
        grid_spec=grid_spec,
        out_shape=[jax.ShapeDtypeStruct((bsz, seq, SWQ_W), BF16),
                   jax.ShapeDtypeStruct((bsz, W, KV_W), F32),
                   jax.ShapeDtypeStruct((bsz, W, KV_W), F32)],
        compiler_params=_params("arbitrary", "arbitrary"),
        name="swa_prompt",
    )(sinks, z3, z3, z3, cos, sin, perm)


def _swa_decode_body(q_ref, zk_ref, zv_ref, kc_ref, vc_ref, cos_ref, sin_ref, perm_ref, sink_ref,
                     o_ref, kn_ref, vn_ref):
    cos = cos_ref[...]
    sin = sin_ref[...]
    perm = perm_ref[...]
    k_new = _rope(zk_ref[...], cos, sin, perm)
    v_new = zv_ref[...].astype(F32)
    kn_ref[...] = k_new
    vn_ref[...] = v_new
    sink = sink_ref[...]
    W = kc_ref.shape[1]
    heads = range(SW_Q_HEADS)
    lane_row = lax.broadcasted_iota(jnp.int32, (1, LANES), 1)
    lane = lax.broadcasted_iota(jnp.int32, (W, LANES), 1)
    low_half = lane_row < SW_HD
    q_all = _rope(q_ref[...], cos, sin, perm) * (SW_HD ** -0.5)
    for i in range(DEC_TILE):
        keys = kc_ref[i]
        vals = vc_ref[i]
        s = jnp.zeros((W, LANES), F32)
        s_new = jnp.zeros((1, LANES), F32)
        q_heads = []
        for h in heads:
            r = i * SW_PAIRS + h % SW_PAIRS
            q_heads.append(jnp.where(low_half if h < SW_PAIRS else ~low_half, q_all[r:r + 1, :], 0.0))
        for h in heads:
            qh = q_heads[h]
            s = jnp.where(lane == h, jnp.sum(keys * qh, axis=-1, keepdims=True), s)
            s_new = jnp.where(lane_row == h, jnp.sum(k_new[i:i + 1, :] * qh, axis=-1, keepdims=True), s_new)
        mx = jnp.maximum(jnp.maximum(jnp.max(s, axis=0, keepdims=True), s_new), sink)
        p = jnp.exp(s - mx)
        p_new = jnp.exp(s_new - mx)
        den = jnp.sum(p, axis=0, keepdims=True) + p_new + jnp.exp(sink - mx)
        w_new = p_new / den
        wgt = p / den
        o_head = [jnp.sum(wgt[:, h:h + 1] * vals, axis=0, keepdims=True) + w_new[:, h:h + 1] * v_new[i:i + 1, :]
                  for h in heads]
        for j in range(SW_PAIRS):
            r = i * SW_PAIRS + j
            o_ref[r:r + 1, :] = jnp.where(low_half, o_head[j], o_head[j + SW_PAIRS]).astype(o_ref.dtype)


def _swa_decode(q2, z2, kc, vc, layer, cos, sin, perm, sink_row):
    n = z2.shape[0]
    T = DEC_TILE
    W = kc.shape[2]
    cache = pl.BlockSpec((None, T, W, KV_W), lambda i: (layer, i, 0, 0))
    return pl.pallas_call(
        _swa_decode_body,
        grid=(n // T,),
        in_specs=[pl.BlockSpec((T * SW_PAIRS, LANES), lambda i: (i, 0)),
                  pl.BlockSpec((T, KV_W), lambda i: (i, ZB_K)),
                  pl.BlockSpec((T, KV_W), lambda i: (i, ZB_V)),
                  cache, cache,
                  pl.BlockSpec((1, LANES), lambda i: (0, 0)),
                  pl.BlockSpec((1, LANES), lambda i: (0, 0)),
                  pl.BlockSpec((LANES, LANES), lambda i: (0, 0)),
                  pl.BlockSpec((1, LANES), lambda i: (0, 0))],
        out_specs=[pl.BlockSpec((T * SW_PAIRS, LANES), lambda i: (i, 0)),
                   pl.BlockSpec((T, KV_W), lambda i: (i, 0)),
                   pl.BlockSpec((T, KV_W), lambda i: (i, 0))],
        out_shape=[jax.ShapeDtypeStruct((n * SW_PAIRS, LANES), q2.dtype),
                   jax.ShapeDtypeStruct((n, KV_W), F32),
                   jax.ShapeDtypeStruct((n, KV_W), F32)],
        compiler_params=_params("arbitrary"),
        name="swa_decode",
    )(q2, z2, z2, kc, vc, cos, sin, perm, sink_row)


def _merge_body(x_ref, oa_ref, ob_ref, ga_ref, gb_ref, wa_ref, wb_ref, wo_ref, nw_ref, x1_ref, hn_ref):
    ta = _dot(oa_ref[...], wa_ref[...])
    tb = _dot(ob_ref[...], wb_ref[...])
    mg = _sigmoid(ga_ref[...].astype(F32)) * ta + _sigmoid(gb_ref[...].astype(F32)) * tb
    x1 = x_ref[...] + _dot(mg.astype(wo_ref.dtype), wo_ref[...])
    x1_ref[...] = x1
    hn_ref[...] = _rms(x1, nw_ref[...]).astype(hn_ref.dtype)


def _merge(x2d, oa, ob, z2, wa, wb, wo, nw, tm):
    m = x2d.shape[0]

    def const(shape):
        return pl.BlockSpec(shape, lambda i: (0, 0))

    return pl.pallas_call(
        _merge_body,
        grid=(m // tm,),
        in_specs=[pl.BlockSpec((tm, D_MODEL), lambda i: (i, 0)),
                  pl.BlockSpec((tm, HG_W), lambda i: (i, 0)),
                  pl.BlockSpec((tm, SWQ_W), lambda i: (i, 0)),
                  pl.BlockSpec((tm, D_MODEL), lambda i: (i, ZB_GA)),
                  pl.BlockSpec((tm, D_MODEL), lambda i: (i, ZB_GB)),
                  const(wa.shape), const(wb.shape), const(wo.shape), const((1, D_MODEL))],
        out_specs=[pl.BlockSpec((tm, D_MODEL), lambda i: (i, 0)),
                   pl.BlockSpec((tm, D_MODEL), lambda i: (i, 0))],
        out_shape=[jax.ShapeDtypeStruct((m, D_MODEL), F32),
                   jax.ShapeDtypeStruct((m, D_MODEL), wo.dtype)],
        compiler_params=_params("arbitrary"),
        name="merge",
    )(x2d, oa, ob, z2, z2, wa, wb, wo, nw)


FFN_CHUNKS = (768, 768, 768, 512)


def _ffn_body(hn_ref, x1_ref, wg_ref, wu_ref, wd_ref, o_ref, acc_ref):
    hn = hn_ref[...]
    off = 0
    for i, width in enumerate(FFN_CHUNKS):
        g = _dot(hn, wg_ref[:, off:off + width])
        u = _dot(hn, wu_ref[:, off:off + width])
        a = (g * _sigmoid(g) * u).astype(BF16)
        y = _dot(a, wd_ref[off:off + width, :])
        if i == 0:
            acc_ref[...] = x1_ref[...] + y
        else:
            acc_ref[...] += y
        off += width
    o_ref[...] = acc_ref[...]


def _ffn_dense(hn, x1, wg, wu, wd, tm):
    m = hn.shape[0]

    def const(shape):
        return pl.BlockSpec(shape, lambda i: (0, 0))

    return pl.pallas_call(
        _ffn_body,
        grid=(m // tm,),
        in_specs=[pl.BlockSpec((tm, D_MODEL), lambda i: (i, 0)),
                  pl.BlockSpec((tm, D_MODEL), lambda i: (i, 0)),
                  const(wg.shape), const(wu.shape), const(wd.shape)],
        out_specs=pl.BlockSpec((tm, D_MODEL), lambda i: (i, 0)),
        out_shape=jax.ShapeDtypeStruct((m, D_MODEL), F32),
        scratch_shapes=[pltpu.VMEM((tm, D_MODEL), F32)],
        compiler_params=_params("arbitrary"),
        name="ffn_dense",
    )(hn, x1, wg, wu, wd)


def _ffn_stream_body(hn_ref, x1_ref, wg_ref, wu_ref, wd_ref, o_ref):
    @pl.when(pl.program_id(0) == 0)
    def _():
        o_ref[...] = x1_ref[...]

    hn = hn_ref[...]
    g = _dot(hn, wg_ref[...])
    u = _dot(hn, wu_ref[...])
    o_ref[...] += _dot((g * _sigmoid(g) * u).astype(wd_ref.dtype), wd_ref[...])


def _ffn_stream(hn, x1, wg, wu, wd, tf):
    m = hn.shape[0]
    full = pl.BlockSpec((m, D_MODEL), lambda f: (0, 0))
    return pl.pallas_call(
        _ffn_stream_body,
        grid=(wg.shape[1] // tf,),
        in_specs=[full, full,
                  pl.BlockSpec((D_MODEL, tf), lambda f: (0, f)),
                  pl.BlockSpec((D_MODEL, tf), lambda f: (0, f)),
                  pl.BlockSpec((tf, D_MODEL), lambda f: (f, 0))],
        out_specs=full,
        out_shape=jax.ShapeDtypeStruct((m, D_MODEL), F32),
        compiler_params=_params("arbitrary"),
        name="ffn_stream",
    )(hn, x1, wg, wu, wd)


RT_E1, RT_E2, RT_G1, RT_G2, RT_R1, RT_R2 = range(6)
SEG_ALIGN = 16
EXPERT_CHUNKS = (1024, 1024, 1024, 512)


def _router_body(hn_ref, wr_ref, route_ref, route_t_ref, cnt_ref):
    T = hn_ref.shape[0]
    logits = _dot(hn_ref[...], wr_ref[...])
    lane = lax.broadcasted_iota(jnp.int32, logits.shape, 1).astype(F32)
    lg = jnp.where(lane < N_EXPERTS, logits, -jnp.inf)
    m1 = jnp.max(lg, axis=-1, keepdims=True)
    i1 = jnp.min(jnp.where(lg == m1, lane, float(LANES)), axis=-1, keepdims=True)
    lg2 = jnp.where(lane == i1, -jnp.inf, lg)
    m2 = jnp.max(lg2, axis=-1, keepdims=True)
    i2 = jnp.min(jnp.where(lg2 == m2, lane, float(LANES)), axis=-1, keepdims=True)
    e2 = jnp.exp(m2 - m1)
    sel = jnp.where((lane == i1) | (lane == i2), 1.0, 0.0)
    r = lax.broadcasted_iota(jnp.int32, (T, T), 0)
    c = lax.broadcasted_iota(jnp.int32, (T, T), 1)
    before = jnp.where(r > c, 1.0, 0.0).astype(BF16)
    rank = _dot(before, sel.astype(BF16))
    r1 = jnp.sum(jnp.where(lane == i1, rank, 0.0), axis=-1, keepdims=True)
    r2 = jnp.sum(jnp.where(lane == i2, rank, 0.0), axis=-1, keepdims=True)
    fields = {RT_E1: i1, RT_E2: i2, RT_G1: 1.0 / (1.0 + e2), RT_G2: e2 / (1.0 + e2), RT_R1: r1, RT_R2: r2}
    route = jnp.zeros_like(logits)
    for idx, val in fields.items():
        route = jnp.where(lane == float(idx), val, route)
    route_ref[...] = route
    route_t_ref[...] = route.T
    cnt_ref[...] = jnp.broadcast_to(jnp.sum(sel, axis=0, keepdims=True), cnt_ref.shape)


def _router(hn, wr, T):
    m = hn.shape[0]
    nt = m // T
    return pl.pallas_call(
        _router_body,
        grid=(nt,),
        in_specs=[pl.BlockSpec((T, D_MODEL), lambda i: (i, 0)),
                  pl.BlockSpec((D_MODEL, LANES), lambda i: (0, 0))],
        out_specs=[pl.BlockSpec((T, LANES), lambda i: (i, 0)),
                   pl.BlockSpec((LANES, T), lambda i: (0, i)),
                   pl.BlockSpec((None, SUBLANES, LANES), lambda i: (i, 0, 0))],
        out_shape=[jax.ShapeDtypeStruct((m, LANES), F32),
                   jax.ShapeDtypeStruct((LANES, m), F32),
                   jax.ShapeDtypeStruct((nt, SUBLANES, LANES), F32)],
        compiler_params=_params("arbitrary"),
        name="router",
    )(hn, wr)


def _segment_copies(seg_ref, so_ref, off_ref, step, make, fn, max_rows):
    sizes = []
    s = max_rows
    while s >= SEG_ALIGN:
        sizes.append(s)
        s //= 2
    for e in range(N_EXPERTS):
        n = seg_ref[step * N_EXPERTS + e]
        src = so_ref[step * N_EXPERTS + e]
        dst = off_ref[step * N_EXPERTS + e]
        for s in sizes:
            @pl.when((n & s) != 0)
            def _(s=s, n=n, src=src, dst=dst):
                done = n & (-2 * s)
                fn(make(pl.multiple_of(src + done, SEG_ALIGN), pl.multiple_of(dst + done, SEG_ALIGN), s))


def _stage_offset(so_ref, step, expert):
    out = jnp.zeros_like(expert)
    for e in range(N_EXPERTS):
        out = jnp.where(expert == float(e), so_ref[step * N_EXPERTS + e].astype(F32), out)
    return out


def _dispatch_body(seg_ref, so_ref, off_ref, hn_ref, rt_ref, init_ref, xs_ref, stage, sem):
    del init_ref
    t = pl.program_id(0)
    nt = pl.num_programs(0)
    T = hn_ref.shape[0]
    S = stage.shape[1]
    slot = t % 2

    def copies(step, slot, fn):
        def make(src, dst, rows):
            return pltpu.make_async_copy(stage.at[slot, pl.ds(src, rows)], xs_ref.at[pl.ds(dst, rows)], sem.at[slot])
        _segment_copies(seg_ref, so_ref, off_ref, step, make, fn, T)

    @pl.when(t >= 2)
    def _():
        copies(t - 2, slot, lambda c: c.wait())

    rt = rt_ref[...]
    tgt1 = _stage_offset(so_ref, t, rt[RT_E1:RT_E1 + 1, :]) + rt[RT_R1:RT_R1 + 1, :]
    tgt2 = _stage_offset(so_ref, t, rt[RT_E2:RT_E2 + 1, :]) + rt[RT_R2:RT_R2 + 1, :]
    rowi = lax.broadcasted_iota(jnp.int32, (S, T), 0).astype(F32)
    sel = jnp.where((rowi == tgt1) | (rowi == tgt2), 1.0, 0.0).astype(BF16)
    stage[slot] = _dot(sel, hn_ref[...]).astype(BF16)
    copies(t, slot, lambda c: c.start())

    @pl.when(t == nt - 1)
    def _():
        copies(t, slot, lambda c: c.wait())

        @pl.when(t >= 1)
        def _():
            copies(t - 1, 1 - slot, lambda c: c.wait())


def _dispatch(hn, route_t, plan, T, S, rows_total):
    m = hn.shape[0]
    grid_spec = pltpu.PrefetchScalarGridSpec(
        num_scalar_prefetch=3,
        grid=(m // T,),
        in_specs=[pl.BlockSpec((T, D_MODEL), lambda t, *_: (t, 0)),
                  pl.BlockSpec((LANES, T), lambda t, *_: (0, t)),
                  pl.BlockSpec(memory_space=pl.ANY)],
        out_specs=pl.BlockSpec(memory_space=pl.ANY),
        scratch_shapes=[pltpu.VMEM((2, S, D_MODEL), BF16), pltpu.SemaphoreType.DMA((2,))],
    )
    return pl.pallas_call(
        _dispatch_body,
        grid_spec=grid_spec,
        out_shape=jax.ShapeDtypeStruct((rows_total, D_MODEL), BF16),
        input_output_aliases={5: 0},
        compiler_params=_params("arbitrary"),
        name="moe_dispatch",
    )(plan["seg"], plan["so"], plan["off"], hn, route_t, jnp.zeros((rows_total, D_MODEL), BF16))


def _experts_body(owner_ref, used_ref, x_ref, wg_ref, wu_ref, wd_ref, y_ref, acc_ref):
    del owner_ref

    @pl.when(pl.program_id(0) < used_ref[0])
    def _():
        x = x_ref[...]
        off = 0
        for i, width in enumerate(EXPERT_CHUNKS):
            g = _dot(x, wg_ref[:, off:off + width])
            u = _dot(x, wu_ref[:, off:off + width])
            y = _dot((g * _sigmoid(g) * u).astype(BF16), wd_ref[off:off + width, :])
            if i == 0:
                acc_ref[...] = y
            else:
                acc_ref[...] += y
            off += width
        y_ref[...] = acc_ref[...].astype(y_ref.dtype)

    @pl.when(pl.program_id(0) >= used_ref[0])
    def _():
        y_ref[...] = jnp.zeros_like(y_ref)


def _experts(xs, plan, wg, wu, wd, bm):
    nblk = xs.shape[0] // bm

    def wspec(shape):
        return pl.BlockSpec((None,) + shape, lambda b, owner, used: (owner[b], 0, 0), pipeline_mode=pl.Buffered(1))

    grid_spec = pltpu.PrefetchScalarGridSpec(
        num_scalar_prefetch=2,
        grid=(nblk,),
        in_specs=[pl.BlockSpec((bm, D_MODEL), lambda b, owner, used: (jnp.minimum(b, used[0] - 1), 0)),
                  wspec((D_MODEL, D_FF_EXPERT)), wspec((D_MODEL, D_FF_EXPERT)), wspec((D_FF_EXPERT, D_MODEL))],
        out_specs=pl.BlockSpec((bm, D_MODEL), lambda b, owner, used: (b, 0)),
        scratch_shapes=[pltpu.VMEM((bm, D_MODEL), F32)],
    )
    return pl.pallas_call(
        _experts_body,
        grid_spec=grid_spec,
        out_shape=jax.ShapeDtypeStruct(xs.shape, BF16),
        compiler_params=_params("arbitrary"),
        name="moe_experts",
    )(plan["owner"], plan["used"], xs, wg, wu, wd)


def _combine_body(seg_ref, so_ref, off_ref, route_ref, x1_ref, nw_ref, ys_ref, o_ref, ybuf, sem, *, final_norm):
    t = pl.program_id(0)
    nt = pl.num_programs(0)
    T = route_ref.shape[0]
    S = ybuf.shape[1]
    slot = t % 2

    def copies(step, slot, fn):
        def make(stage_row, grouped_row, rows):
            return pltpu.make_async_copy(ys_ref.at[pl.ds(grouped_row, rows)], ybuf.at[slot, pl.ds(stage_row, rows)],
                                         sem.at[slot])
        _segment_copies(seg_ref, so_ref, off_ref, step, make, fn, T)

    @pl.when(t == 0)
    def _():
        ybuf[...] = jnp.zeros_like(ybuf)
        copies(0, 0, lambda c: c.start())

    @pl.when(t + 1 < nt)
    def _():
        copies(t + 1, 1 - slot, lambda c: c.start())

    copies(t, slot, lambda c: c.wait())

    route = route_ref[...]
    tgt1 = _stage_offset(so_ref, t, route[:, RT_E1:RT_E1 + 1]) + route[:, RT_R1:RT_R1 + 1]
    tgt2 = _stage_offset(so_ref, t, route[:, RT_E2:RT_E2 + 1]) + route[:, RT_R2:RT_R2 + 1]
    coli = lax.broadcasted_iota(jnp.int32, (T, S), 1).astype(F32)
    sel = (jnp.where(coli == tgt1, route[:, RT_G1:RT_G1 + 1], 0.0)
           + jnp.where(coli == tgt2, route[:, RT_G2:RT_G2 + 1], 0.0))
    x2 = x1_ref[...] + _dot(sel.astype(BF16), ybuf[slot])
    o_ref[...] = _rms(x2, nw_ref[...]) if final_norm else x2


def _combine(ys, route, x1, nw, plan, T, S, final_norm):
    m = x1.shape[0]
    grid_spec = pltpu.PrefetchScalarGridSpec(
        num_scalar_prefetch=3,
        grid=(m // T,),
        in_specs=[pl.BlockSpec((T, LANES), lambda t, *_: (t, 0)),
                  pl.BlockSpec((T, D_MODEL), lambda t, *_: (t, 0)),
                  pl.BlockSpec((1, D_MODEL), lambda t, *_: (0, 0)),
                  pl.BlockSpec(memory_space=pl.ANY)],
        out_specs=pl.BlockSpec((T, D_MODEL), lambda t, *_: (t, 0)),
        scratch_shapes=[pltpu.VMEM((2, S, D_MODEL), BF16), pltpu.SemaphoreType.DMA((2,))],
    )
    return pl.pallas_call(
        functools.partial(_combine_body, final_norm=final_norm),
        grid_spec=grid_spec,
        out_shape=jax.ShapeDtypeStruct((m, D_MODEL), F32),
        compiler_params=_params("arbitrary"),
        name="moe_combine",
    )(plan["seg"], plan["so"], plan["off"], route, x1, nw, ys)


def _moe_plan(cnt, bm, nblk):
    seg = (cnt + (SEG_ALIGN - 1)) // SEG_ALIGN * SEG_ALIGN
    so = jnp.cumsum(seg, axis=1) - seg
    blocks = (jnp.sum(seg, axis=0) + (bm - 1)) // bm
    blk_end = jnp.cumsum(blocks)
    off = (blk_end - blocks)[None, :] * bm + jnp.cumsum(seg, axis=0) - seg
    owner = jnp.minimum(jnp.sum(jnp.arange(nblk)[:, None] >= blk_end[None, :], axis=1), N_EXPERTS - 1)
    as_i32 = lambda a: a.astype(jnp.int32).reshape(-1)
    return {"seg": as_i32(seg), "so": as_i32(so), "off": as_i32(off), "owner": as_i32(owner),
            "used": as_i32(blk_end[-1:])}


def _moe_ffn(hn, x1, nw, wr, wg, wu, wd, T, bm, final_norm):
    m = hn.shape[0]
    tiles = m // T
    pad_rows = tiles * N_EXPERTS * (SEG_ALIGN - 1)
    stage_rows = -(-(2 * T + N_EXPERTS * (SEG_ALIGN - 1)) // LANES) * LANES
    nblk = -(-(2 * m + pad_rows) // bm) + N_EXPERTS
    route, route_t, cnt = _router(hn, wr, T)
    plan = _moe_plan(cnt[:, 0, :N_EXPERTS].astype(jnp.int32), bm, nblk)
    xs = _dispatch(hn.astype(BF16), route_t, plan, T, stage_rows, nblk * bm)
    ys = _experts(xs, plan, wg, wu, wd, bm)
    return _combine(ys, route, x1, nw, plan, T, stage_rows, final_norm)


def _norm_body(x_ref, nw_ref, o_ref):
    o_ref[...] = _rms(x_ref[...], nw_ref[...])


def _norm(x, nw, tm):
    m = x.shape[0]
    spec = pl.BlockSpec((tm, D_MODEL), lambda i: (i, 0))
    return pl.pallas_call(
        _norm_body, grid=(m // tm,),
        in_specs=[spec, pl.BlockSpec((1, D_MODEL), lambda i: (0, 0))],
        out_specs=spec, out_shape=jax.ShapeDtypeStruct((m, D_MODEL), F32),
        compiler_params=_params("arbitrary"), name="norm",
    )(x, nw)


def _layout_w_in(w):
    o = 0
    hg = w[:, o:o + 4 * HG_W]; o += 4 * HG_W
    sq = w[:, o:o + SWQ_W]; o += SWQ_W
    sk = w[:, o:o + KV_W]; o += KV_W
    sv = w[:, o:o + KV_W]; o += KV_W
    ga = w[:, o:o + D_MODEL]; o += D_MODEL
    gb = w[:, o:o + D_MODEL]
    pairs = [sq[:, h * SW_HD:(h + 1) * SW_HD] for j in range(SW_PAIRS) for h in (j, j + SW_GROUP)]
    return jnp.concatenate([hg, ga, gb] + pairs + [sk, sv], axis=1)


def _layout_w_branch_b(w):
    return jnp.concatenate([w[h * SW_HD:(h + 1) * SW_HD] for j in range(SW_PAIRS) for h in (j, j + SW_GROUP)], axis=0)


def _rope_tables(pos):
    half = SW_HD // 2
    inv = ROPE_THETA ** (-jnp.arange(half, dtype=F32) / half)
    ang = pos.astype(F32)[:, None] * inv[None, :]
    cos = jnp.cos(ang)
    sin = jnp.sin(ang)
    reps = LANES // SW_HD
    return jnp.tile(cos, (1, 2 * reps)), jnp.tile(jnp.concatenate([-sin, sin], axis=1), (1, reps))


def _rotate_half_matrix():
    j = jnp.arange(LANES)
    src = jnp.where((j % SW_HD) < SW_HD // 2, j + SW_HD // 2, j - SW_HD // 2)
    return (jnp.arange(LANES)[:, None] == src[None, :]).astype(F32)


def _tile(m, pref):
    return pref if m % pref == 0 else m


def kernel(x_prompt, x_sample, state_hgrn, cache_swa_k, cache_swa_v, norm_mix, w_in, hg_lb_logits, hg_norm,
           swa_sinks, w_branch_a, w_branch_b, w_out, norm_ffn, w_gate_dense, w_up_dense, w_down_dense,
           w_router, w_gate_moe, w_up_moe, w_down_moe, norm_final):
    depth = w_in.shape[0]
    bsz, seq, _ = x_prompt.shape
    nsamp = x_sample.shape[0]
    mp = bsz * seq
    wb = cache_swa_k.shape[2]

    cos_p, sin_p = _rope_tables(jnp.arange(seq))
    cos_s, sin_s = _rope_tables(PAST_LEN + jnp.arange(1))
    perm32 = _rotate_half_matrix()
    perm = perm32.astype(BF16)
    lbl = hg_lb_logits.astype(F32)
    state32 = state_hgrn.astype(F32)
    kcache = cache_swa_k.reshape(depth, nsamp, wb, KV_W).astype(F32)
    vcache = cache_swa_v.reshape(depth, nsamp, wb, KV_W).astype(F32)

    xp = x_prompt.reshape(mp, D_MODEL)
    xs = x_sample.reshape(nsamp, D_MODEL)
    outs = {k: [] for k in ("sp", "kp", "vp", "ss", "ks", "vs")}
    nfin = norm_final.reshape(1, D_MODEL)
    normed = False

    for l in range(depth):
        w_in32 = _layout_w_in(w_in[l])
        wa32 = w_branch_a[l]
        wbb32 = _layout_w_branch_b(w_branch_b[l])
        wo32 = w_out[l]
        w_in_l = w_in32.astype(BF16)
        wa = wa32.astype(BF16)
        wbb = wbb32.astype(BF16)
        wo = wo32.astype(BF16)
        nmix = norm_mix[l].reshape(1, D_MODEL)
        nffn = norm_ffn[l].reshape(1, D_MODEL)
        hgn = hg_norm[l].reshape(1, HG_D)
        sinks = swa_sinks[l].astype(F32)

        zp = _proj(xp, nmix, w_in_l, _tile(mp, 512))
        zp3 = zp.reshape(bsz, seq, Z_WIDTH)
        oa_p, s_p = _hgrn_prompt(zp3, lbl, hgn, l)
        ob_p, k_p, v_p = _swa_prompt(zp3, sinks, cos_p, sin_p, perm)
        x1p, hnp = _merge(xp, oa_p.reshape(mp, HG_W), ob_p.reshape(mp, SWQ_W), zp, wa, wbb, wo, nffn,
                          _tile(mp, 512))

        zs = _proj(xs, nmix, w_in32, nsamp)
        oa_s, s_s = _hgrn_decode(zs, state32, lbl, hgn, l)
        q2 = zs[:, ZB_SWQ * SWQ_W:(ZB_SWQ + 1) * SWQ_W].reshape(nsamp * SW_PAIRS, LANES)
        kc, vc = kcache[l], vcache[l]
        sink_row = jnp.pad(sinks, (0, LANES - SW_Q_HEADS)).reshape(1, LANES)
        ob_s, k_s, v_s = _swa_decode(q2, zs, kcache, vcache, l, cos_s, sin_s, perm32, sink_row)
        x1s, hns = _merge(xs, oa_s, ob_s.reshape(nsamp, SWQ_W), zs, wa32, wbb32, wo32, nffn, nsamp)

        j = l // 2
        if l % 2 == 0:
            xp = _ffn_dense(hnp, x1p, w_gate_dense[j].astype(BF16), w_up_dense[j].astype(BF16),
                            w_down_dense[j].astype(BF16), _tile(mp, 512))
            xs = _ffn_stream(hns, x1s, w_gate_dense[j], w_up_dense[j], w_down_dense[j], 256)
        else:
            wr32 = jnp.pad(w_router[j], ((0, 0), (0, LANES - N_EXPERTS)))
            wg = w_gate_moe[j].astype(BF16)
            wu = w_up_moe[j].astype(BF16)
            wd = w_down_moe[j].astype(BF16)
            last = l == depth - 1
            xp = _moe_ffn(hnp, x1p, nfin, wr32.astype(BF16), wg, wu, wd, _tile(mp, 512), 512, last)
            xs = _moe_ffn(hns, x1s, nfin, wr32, wg, wu, wd, nsamp, 128, last)
            normed = last

        outs["sp"].append(s_p)
        outs["kp"].append(k_p.reshape(bsz, WINDOW, SW_KV_HEADS, SW_HD))
        outs["vp"].append(v_p.reshape(bsz, WINDOW, SW_KV_HEADS, SW_HD))
        outs["ss"].append(s_s.astype(state_hgrn.dtype))
        k_rows = jnp.concatenate([kc[:, 1:], k_s[:, None, :]], axis=1)
        v_rows = jnp.concatenate([vc[:, 1:], v_s[:, None, :]], axis=1)
        outs["ks"].append(k_rows.reshape(nsamp, wb, SW_KV_HEADS, SW_HD))
        outs["vs"].append(v_rows.reshape(nsamp, wb, SW_KV_HEADS, SW_HD))

    yp, ys = (xp, xs) if normed else (_norm(xp, nfin, _tile(mp, 1024)), _norm(xs, nfin, nsamp))

    return (yp.reshape(x_prompt.shape), ys.reshape(x_sample.shape),
            jnp.stack(outs["sp"]), jnp.stack(outs["kp"]), jnp.stack(outs["vp"]),
            jnp.stack(outs["ss"]), jnp.stack(outs["ks"]), jnp.stack(outs["vs"]))
```

```python
import functools

import jax
import jax.numpy as jnp
from jax import lax
from jax.experimental import pallas as pl
from jax.experimental.pallas import tpu as pltpu

F32 = jnp.float32
BF16 = jnp.bfloat16

D_MODEL = 1024
PAST_LEN = 16384
HG_HEADS = 4
HG_D = 128
HG_W = HG_HEADS * HG_D
SW_Q_HEADS = 8
SW_KV_HEADS = 2
SW_HD = 64
SW_GROUP = SW_Q_HEADS // SW_KV_HEADS
WINDOW = 128
ROPE_THETA = 10000.0
D_FF_DENSE = 2816
N_EXPERTS = 8
D_FF_EXPERT = 3584
RMS_EPS = 1e-6

LANES = 128
SUBLANES = 8
VMEM_LIMIT = 56 * 1024 * 1024

SWQ_W = SW_Q_HEADS * SW_HD
SW_PAIRS = SWQ_W // LANES
KV_W = SW_KV_HEADS * SW_HD
Z_WIDTH = 4 * HG_W + 2 * D_MODEL + SWQ_W + 2 * KV_W
ZB_Q, ZB_F, ZB_I, ZB_G = 0, 1, 2, 3
ZB_GA, ZB_GB = 2, 3
ZB_SWQ = 8
ZB_K, ZB_V = 36, 37
PROJ_CHUNKS = (1280, 1280, 1280, 1024)

HG_CHUNK = 128
HG_UNROLL = 4
SW_UNROLL = 4
SEQ_BLOCK = 512
DEC_TILE = 8


def _dot_dims(a, b, dims):
    precision = lax.Precision.HIGHEST if a.dtype == F32 else None
    return lax.dot_general(a, b, (dims, ((), ())), precision=precision, preferred_element_type=F32)


def _dot(a, b):
    return _dot_dims(a, b, ((1,), (0,)))


def _dot_nt(a, b):
    return _dot_dims(a, b, ((1,), (1,)))


def _dot_tn(a, b):
    return _dot_dims(a, b, ((0,), (0,)))


def _sigmoid(x):
    return 1.0 / (1.0 + jnp.exp(-x))


def _rms(x, w):
    ms = jnp.mean(x * x, axis=-1, keepdims=True)
    return x * lax.rsqrt(ms + RMS_EPS) * w


def _params(*sem):
    return pltpu.CompilerParams(dimension_semantics=sem, vmem_limit_bytes=VMEM_LIMIT)


def _proj_body(x_ref, nw_ref, w_ref, z_ref):
    h = _rms(x_ref[...], nw_ref[...]).astype(w_ref.dtype)
    off = 0
    for width in PROJ_CHUNKS:
        z_ref[:, off:off + width] = _dot(h, w_ref[:, off:off + width]).astype(z_ref.dtype)
        off += width


def _proj(x2d, nw, w, tm):
    m = x2d.shape[0]
    n = w.shape[1]
    return pl.pallas_call(
        _proj_body,
        grid=(m // tm,),
        in_specs=[
            pl.BlockSpec((tm, D_MODEL), lambda i: (i, 0)),
            pl.BlockSpec((1, D_MODEL), lambda i: (0, 0)),
            pl.BlockSpec((D_MODEL, n), lambda i: (0, 0), pipeline_mode=pl.Buffered(1)),
        ],
        out_specs=pl.BlockSpec((tm, n), lambda i: (i, 0)),
        out_shape=jax.ShapeDtypeStruct((m, n), w.dtype),
        compiler_params=_params("arbitrary"),
        name="proj",
    )(x2d, nw, w)


def _lower_bound(lbl, layer):
    mx = jnp.max(lbl, axis=0, keepdims=True)
    e = jnp.exp(lbl - mx)
    sm = e / jnp.sum(e, axis=0, keepdims=True)
    cum = sm[0:1, :]
    for i in range(1, layer + 1):
        cum = cum + sm[i:i + 1, :]
    return cum - sm[0:1, :]


def _split3(x):
    hi = x.astype(BF16)
    r = x - hi.astype(F32)
    mid = r.astype(BF16)
    lo = (r - mid.astype(F32)).astype(BF16)
    return hi, mid, lo


def _hgrn_prompt_body(zq_ref, zf_ref, zi_ref, zg_ref, lbl_ref, nw_ref, o_ref, s_ref, st_scr, b_all_scr, *, layer):
    c = pl.program_id(1)
    C = HG_CHUNK

    @pl.when(c == 0)
    def _():
        st_scr[...] = jnp.zeros_like(st_scr)

    lb = _lower_bound(lbl_ref[...], layer)
    row = lax.broadcasted_iota(jnp.int32, (C, C), 0)
    col = lax.broadcasted_iota(jnp.int32, (C, C), 1)
    xr = row ^ col
    tri = jnp.where(row >= col, 1.0, 0.0).astype(BF16)
    sub4 = (lax.broadcasted_iota(jnp.int32, (SUBLANES, HG_D), 0) & 4) == 0
    levels = (1, 2, 4, 8, 16, 32, 64)
    pair_level = {m: (xr >= m) & (xr < 2 * m) for m in levels}
    upper_half = {m: (row & m) != 0 for m in levels if m < SUBLANES}
    nw = nw_ref[...]

    def chunks(i, carry):
        for u in range(HG_UNROLL):
            chunk(pl.ds(pl.multiple_of((i * HG_UNROLL + u) * C, C), C), b_all_scr.at[u])
        return carry

    def chunk(rows, b_scr):
        hf = zf_ref[rows, :].astype(F32)
        fg_all = lb + (1.0 - lb) * _sigmoid(hf)
        g_all = jnp.log2(fg_all)
        k_all = 1.0 - fg_all
        g1, g2, g3 = _split3(g_all)
        b_all = _dot(tri, g1) + _dot(tri, g2) + _dot(tri, g3)
        b_scr[...] = b_all
        hq = zq_ref[rows, :].astype(F32)
        q_all = hq * _sigmoid(hq) * (HG_D ** -0.5)
        gate = zg_ref[rows, :].astype(F32)
        gate_all = gate * _sigmoid(gate)
        for h in range(HG_HEADS):
            sl = slice(h * HG_D, (h + 1) * HG_D)
            head(h, rows, b_scr, b_all[:, sl], q_all[:, sl], k_all[:, sl], fg_all[:, sl], gate_all[:, sl])

    def head(h, rows, b_scr, b, q, k, f, gate):
        sl = slice(h * HG_D, (h + 1) * HG_D)
        v = zi_ref[rows, sl]

        acc = _dot_nt(q.astype(BF16), k.astype(BF16))
        for m in levels:
            if m == 1:
                w = jnp.where(upper_half[m], q * f, k)
            elif m < SUBLANES:
                if m == 2:
                    pieces = []
                    for j in range(C // SUBLANES):
                        lo = jnp.broadcast_to(b_scr[SUBLANES * j + 1:SUBLANES * j + 2, sl], (SUBLANES, HG_D))
                        hi = jnp.broadcast_to(b_scr[SUBLANES * j + 5:SUBLANES * j + 6, sl], (SUBLANES, HG_D))
                        pieces.append(jnp.where(sub4, lo, hi))
                else:
                    pieces = [jnp.broadcast_to(b_scr[i * 2 * m + m - 1:i * 2 * m + m, sl], (2 * m, HG_D))
                              for i in range(C // (2 * m))]
                d = b - jnp.concatenate(pieces, axis=0)
                w = jnp.where(upper_half[m], q, k) * jnp.exp2(jnp.where(upper_half[m], d, -d))
            else:
                expo, qk = [], []
                for i in range(C // (2 * m)):
                    lo, mid, hi = i * 2 * m, i * 2 * m + m, (i + 1) * 2 * m
                    bref = b_scr[mid - 1:mid, sl]
                    expo += [bref - b[lo:mid], b[mid:hi] - bref]
                    qk += [k[lo:mid], q[mid:hi]]
                w = jnp.concatenate(qk, axis=0) * jnp.exp2(jnp.concatenate(expo, axis=0))
            wb = w.astype(BF16)
            acc = jnp.where(pair_level[m], _dot_nt(wb, wb), acc)
        a = jnp.where(row >= col, acc, 0.0)

        st = st_scr[h]
        o = _dot(a.astype(BF16), v) + _dot_nt((q * jnp.exp2(b)).astype(BF16), st.astype(BF16))
        b_last = b[C - 1:C, :]
        kd = k * jnp.exp2(b_last - b)
        st_scr[h] = jnp.exp2(b_last) * st + _dot_tn(v, kd.astype(BF16))

        o_ref[rows, sl] = (_rms(o, nw) * gate).astype(o_ref.dtype)

    lax.fori_loop(0, zq_ref.shape[0] // (C * HG_UNROLL), chunks, 0)

    @pl.when(c == pl.num_programs(1) - 1)
    def _():
        for h in range(HG_HEADS):
            s_ref[h] = st_scr[h].T


def _hgrn_prompt(z3, lbl, nw, layer):
    bsz, seq, _ = z3.shape
    C = HG_CHUNK
    rows = _tile(seq, SEQ_BLOCK)

    def zspec(blk):
        return pl.BlockSpec((None, rows, HG_W), lambda b, c: (b, c, blk))

    return pl.pallas_call(
        functools.partial(_hgrn_prompt_body, layer=layer),
        grid=(bsz, seq // rows),
        in_specs=[zspec(ZB_Q), zspec(ZB_F), zspec(ZB_I), zspec(ZB_G),
                  pl.BlockSpec(lbl.shape, lambda b, c: (0, 0)),
                  pl.BlockSpec((1, HG_D), lambda b, c: (0, 0))],
        out_specs=[pl.BlockSpec((None, rows, HG_W), lambda b, c: (b, c, 0)),
                   pl.BlockSpec((None, HG_HEADS, HG_D, HG_D), lambda b, c: (b, 0, 0, 0))],
        out_shape=[jax.ShapeDtypeStruct((bsz, seq, HG_W), BF16),
                   jax.ShapeDtypeStruct((bsz, HG_HEADS, HG_D, HG_D), F32)],
        scratch_shapes=[pltpu.VMEM((HG_HEADS, HG_D, HG_D), F32), pltpu.VMEM((HG_UNROLL, C, HG_W), F32)],
        compiler_params=_params("arbitrary", "arbitrary"),
        name="hgrn_prompt",
    )(z3, z3, z3, z3, lbl, nw)


def _hgrn_decode_body(zq_ref, zf_ref, zi_ref, zg_ref, lbl_ref, nw_ref, s_ref, o_ref, sn_ref, o_scr, *, layer):
    lb = _lower_bound(lbl_ref[...], layer)
    fg = lb + (1.0 - lb) * _sigmoid(zf_ref[...].astype(F32))
    kk = 1.0 - fg
    hq = zq_ref[...].astype(F32)
    q = hq * _sigmoid(hq) * (HG_D ** -0.5)
    v = zi_ref[...].astype(F32)
    eye = (lax.broadcasted_iota(jnp.int32, (HG_D, HG_D), 0)
           == lax.broadcasted_iota(jnp.int32, (HG_D, HG_D), 1))

    def column(r):
        return jnp.sum(jnp.where(eye, jnp.broadcast_to(r, (HG_D, HG_D)), 0.0), axis=1, keepdims=True)

    for i in range(DEC_TILE):
        for h in range(HG_HEADS):
            sl = slice(h * HG_D, (h + 1) * HG_D)
            sn = column(fg[i:i + 1, sl]) * s_ref[i, h] + column(kk[i:i + 1, sl]) * v[i:i + 1, sl]
            sn_ref[i, h] = sn
            o_scr[i:i + 1, sl] = jnp.sum(column(q[i:i + 1, sl]) * sn, axis=0, keepdims=True)

    gate = zg_ref[...].astype(F32)
    gate = gate * _sigmoid(gate)
    nw = nw_ref[...]
    for h in range(HG_HEADS):
        sl = slice(h * HG_D, (h + 1) * HG_D)
        o_ref[:, sl] = (_rms(o_scr[:, sl], nw) * gate[:, sl]).astype(o_ref.dtype)


def _hgrn_decode(z2, state, lbl, nw, layer):
    n = z2.shape[0]
    T = DEC_TILE

    def zspec(blk):
        return pl.BlockSpec((T, HG_W), lambda i: (i, blk))

    sblock = (T, HG_HEADS, HG_D, HG_D)
    return pl.pallas_call(
        functools.partial(_hgrn_decode_body, layer=layer),
        grid=(n // T,),
        in_specs=[zspec(ZB_Q), zspec(ZB_F), zspec(ZB_I), zspec(ZB_G),
                  pl.BlockSpec(lbl.shape, lambda i: (0, 0)),
                  pl.BlockSpec((1, HG_D), lambda i: (0, 0)),
                  pl.BlockSpec((None,) + sblock, lambda i: (layer, i, 0, 0, 0))],
        out_specs=[pl.BlockSpec((T, HG_W), lambda i: (i, 0)), pl.BlockSpec(sblock, lambda i: (i, 0, 0, 0))],
        out_shape=[jax.ShapeDtypeStruct((n, HG_W), z2.dtype),
                   jax.ShapeDtypeStruct(state.shape[1:], F32)],
        scratch_shapes=[pltpu.VMEM((T, HG_W), F32)],
        compiler_params=_params("arbitrary"),
        name="hgrn_decode",
    )(z2, z2, z2, z2, lbl, nw, state)


def _rope(x, cos, sin, perm):
    return x.astype(F32) * cos + _dot(x, perm) * sin


def _swa_prompt_body(sinks_ref, zq_ref, zk_ref, zv_ref, cos_ref, sin_ref, perm_ref,
                     o_ref, kr_ref, vr_ref, kprev, vprev):
    n = pl.program_id(1)
    W = WINDOW

    @pl.when(n == 0)
    def _():
        kprev[...] = jnp.zeros_like(kprev)
        vprev[...] = jnp.zeros_like(vprev)

    perm = perm_ref[...]
    qi = lax.broadcasted_iota(jnp.int32, (W, 2 * W), 0)
    kj = lax.broadcasted_iota(jnp.int32, (W, 2 * W), 1)
    band = (kj >= qi) & (kj <= qi + W)
    low_half = lax.broadcasted_iota(jnp.int32, (W, LANES), 1) < SW_HD
    heads = range(SW_Q_HEADS)
    scale = SW_HD ** -0.5

    def blocks(i, carry):
        for u in range(SW_UNROLL):
            block(i * SW_UNROLL + u)
        return carry

    def block(i):
        rows = pl.ds(pl.multiple_of(i * W, W), W)
        cos = cos_ref[rows, :]
        sin = sin_ref[rows, :]
        k_rot = _rope(zk_ref[rows, :], cos, sin, perm)
        kr_ref[...] = k_rot
        vr_ref[...] = zv_ref[rows, :].astype(F32)
        k_cur = k_rot.astype(BF16)
        v_cur = zv_ref[rows, :]
        kk = jnp.concatenate([kprev[...], k_cur], axis=0)
        vv = jnp.concatenate([vprev[...], v_cur], axis=0)
        valid = band & (((n > 0) | (i > 0)) | (kj >= W))

        q_in = [zq_ref[rows, j * LANES:(j + 1) * LANES] for j in range(SW_PAIRS)]
        rot = _dot(jnp.concatenate(q_in, axis=0), perm)
        pair = [(q_in[j].astype(F32) * cos + rot[j * W:(j + 1) * W] * sin) * scale for j in range(SW_PAIRS)]
        q_rot = [jnp.where(low_half if h < SW_PAIRS else ~low_half, pair[h % SW_PAIRS], 0.0).astype(BF16)
                 for h in heads]
        s_all = _dot_nt(jnp.concatenate(q_rot, axis=0), kk)
        p_all, den_all = [], []
        for h in heads:
            s = jnp.where(valid, s_all[h * W:(h + 1) * W], -jnp.inf)
            sink = sinks_ref[h]
            mx = jnp.maximum(jnp.max(s, axis=-1, keepdims=True), sink)
            p = jnp.exp(s - mx)
            den_all.append(jnp.sum(p, axis=-1, keepdims=True) + jnp.exp(sink - mx))
            p_all.append(p.astype(BF16))
        o_all = _dot(jnp.concatenate(p_all, axis=0), vv)
        o_head = [o_all[h * W:(h + 1) * W] / den_all[h] for h in heads]
        for j in range(SW_PAIRS):
            o_pair = jnp.where(low_half, o_head[j], o_head[j + SW_PAIRS])
            o_ref[rows, j * LANES:(j + 1) * LANES] = o_pair.astype(o_ref.dtype)
        kprev[...] = k_cur
        vprev[...] = v_cur

    lax.fori_loop(0, zq_ref.shape[0] // (W * SW_UNROLL), blocks, 0)


def _swa_prompt(z3, sinks, cos, sin, perm):
    bsz, seq, _ = z3.shape
    W = WINDOW
    rows = _tile(seq, SEQ_BLOCK)
    grid_spec = pltpu.PrefetchScalarGridSpec(
        num_scalar_prefetch=1,
        grid=(bsz, seq // rows),
        in_specs=[pl.BlockSpec((None, rows, SWQ_W), lambda b, n, s: (b, n, ZB_SWQ)),
                  pl.BlockSpec((None, rows, KV_W), lambda b, n, s: (b, n, ZB_K)),
                  pl.BlockSpec((None, rows, KV_W), lambda b, n, s: (b, n, ZB_V)),
                  pl.BlockSpec((rows, LANES), lambda b, n, s: (n, 0)),
                  pl.BlockSpec((rows, LANES), lambda b, n, s: (n, 0)),
                  pl.BlockSpec((LANES, LANES), lambda b, n, s: (0, 0))],
        out_specs=[pl.BlockSpec((None, rows, SWQ_W), lambda b, n, s: (b, n, 0)),
                   pl.BlockSpec((None, W, KV_W), lambda b, n, s: (b, 0, 0)),
                   pl.BlockSpec((None, W, KV_W), lambda b, n, s: (b, 0, 0))],
        scratch_shapes=[pltpu.VMEM((W, KV_W), BF16), pltpu.VMEM((W, KV_W), BF16)],
    )
    return pl.pallas_call(
        _swa_prompt_body,
        grid_spec=grid_spec,
        out_shape=[jax.ShapeDtypeStruct((bsz, seq, SWQ_W), BF16),
                   jax.ShapeDtypeStruct((bsz, W, KV_W), F32),
                   jax.ShapeDtypeStruct((bsz, W, KV_W), F32)],
        compiler_params=_params("arbitrary", "arbitrary"),
        name="swa_prompt",
    )(sinks, z3, z3, z3, cos, sin, perm)


def _swa_decode_body(q_ref, zk_ref, zv_ref, kc_ref, vc_ref, cos_ref, sin_ref, perm_ref, sink_ref,
                     o_ref, kn_ref, vn_ref):
    cos = cos_ref[...]
    sin = sin_ref[...]
    perm = perm_ref[...]
    k_new = _rope(zk_ref[...], cos, sin, perm)
    v_new = zv_ref[...].astype(F32)
    sink = sink_ref[...]
    W = kc_ref.shape[1]
    for i in range(DEC_TILE):
        kn_ref[i, 0:W - 1, :] = kc_ref[i, 1:W, :]
        vn_ref[i, 0:W - 1, :] = vc_ref[i, 1:W, :]
        kn_ref[i, W - 1:W, :] = k_new[i:i + 1, :]
        vn_ref[i, W - 1:W, :] = v_new[i:i + 1, :]
    heads = range(SW_Q_HEADS)
    lane_row = lax.broadcasted_iota(jnp.int32, (1, LANES), 1)
    lane = lax.broadcasted_iota(jnp.int32, (W, LANES), 1)
    low_half = lane_row < SW_HD
    q_all = _rope(q_ref[...], cos, sin, perm) * (SW_HD ** -0.5)
    for i in range(DEC_TILE):
        keys = kc_ref[i]
        vals = vc_ref[i]
        s = jnp.zeros((W, LANES), F32)
        s_new = jnp.zeros((1, LANES), F32)
        q_heads = []
        for h in heads:
            r = i * SW_PAIRS + h % SW_PAIRS
            q_heads.append(jnp.where(low_half if h < SW_PAIRS else ~low_half, q_all[r:r + 1, :], 0.0))
        for h in heads:
            qh = q_heads[h]
            s = jnp.where(lane == h, jnp.sum(keys * qh, axis=-1, keepdims=True), s)
            s_new = jnp.where(lane_row == h, jnp.sum(k_new[i:i + 1, :] * qh, axis=-1, keepdims=True), s_new)
        mx = jnp.maximum(jnp.maximum(jnp.max(s, axis=0, keepdims=True), s_new), sink)
        p = jnp.exp(s - mx)
        p_new = jnp.exp(s_new - mx)
        den = jnp.sum(p, axis=0, keepdims=True) + p_new + jnp.exp(sink - mx)
        w_new = p_new / den
        wgt = p / den
        o_head = [jnp.sum(wgt[:, h:h + 1] * vals, axis=0, keepdims=True) + w_new[:, h:h + 1] * v_new[i:i + 1, :]
                  for h in heads]
        for j in range(SW_PAIRS):
            r = i * SW_PAIRS + j
            o_ref[r:r + 1, :] = jnp.where(low_half, o_head[j], o_head[j + SW_PAIRS]).astype(o_ref.dtype)


def _swa_decode(q2, z2, kc, vc, layer, cos, sin, perm, sink_row):
    n = z2.shape[0]
    T = DEC_TILE
    W = kc.shape[2]
    cache = pl.BlockSpec((None, T, W, KV_W), lambda i: (layer, i, 0, 0))
    return pl.pallas_call(
        _swa_decode_body,
        grid=(n // T,),
        in_specs=[pl.BlockSpec((T * SW_PAIRS, LANES), lambda i: (i, 0)),
                  pl.BlockSpec((T, KV_W), lambda i: (i, ZB_K)),
                  pl.BlockSpec((T, KV_W), lambda i: (i, ZB_V)),
                  cache, cache,
                  pl.BlockSpec((1, LANES), lambda i: (0, 0)),
                  pl.BlockSpec((1, LANES), lambda i: (0, 0)),
                  pl.BlockSpec((LANES, LANES), lambda i: (0, 0)),
                  pl.BlockSpec((1, LANES), lambda i: (0, 0))],
        out_specs=[pl.BlockSpec((T * SW_PAIRS, LANES), lambda i: (i, 0)),
                   pl.BlockSpec((T, W, KV_W), lambda i: (i, 0, 0)),
                   pl.BlockSpec((T, W, KV_W), lambda i: (i, 0, 0))],
        out_shape=[jax.ShapeDtypeStruct((n * SW_PAIRS, LANES), q2.dtype),
                   jax.ShapeDtypeStruct((n, W, KV_W), F32),
                   jax.ShapeDtypeStruct((n, W, KV_W), F32)],
        compiler_params=_params("arbitrary"),
        name="swa_decode",
    )(q2, z2, z2, kc, vc, cos, sin, perm, sink_row)


def _merge_value(x_ref, oa_ref, ob_ref, ga_ref, gb_ref, wa_ref, wb_ref, wo_ref):
    ta = _dot(oa_ref[...], wa_ref[...])
    tb = _dot(ob_ref[...], wb_ref[...])
    mg = _sigmoid(ga_ref[...].astype(F32)) * ta + _sigmoid(gb_ref[...].astype(F32)) * tb
    return x_ref[...] + _dot(mg.astype(wo_ref.dtype), wo_ref[...])


def _merge_body(x_ref, oa_ref, ob_ref, ga_ref, gb_ref, wa_ref, wb_ref, wo_ref, nw_ref, x1_ref, hn_ref):
    x1 = _merge_value(x_ref, oa_ref, ob_ref, ga_ref, gb_ref, wa_ref, wb_ref, wo_ref)
    x1_ref[...] = x1
    hn_ref[...] = _rms(x1, nw_ref[...]).astype(hn_ref.dtype)


def _resident(shape):
    return pl.BlockSpec(shape, lambda i: (0,) * len(shape), pipeline_mode=pl.Buffered(1))


def _merge_in_specs(tm, wa, wb, wo):
    return [pl.BlockSpec((tm, D_MODEL), lambda i: (i, 0)),
            pl.BlockSpec((tm, HG_W), lambda i: (i, 0)),
            pl.BlockSpec((tm, SWQ_W), lambda i: (i, 0)),
            pl.BlockSpec((tm, D_MODEL), lambda i: (i, ZB_GA)),
            pl.BlockSpec((tm, D_MODEL), lambda i: (i, ZB_GB)),
            _resident(wa.shape), _resident(wb.shape), _resident(wo.shape), _resident((1, D_MODEL))]


def _merge(x2d, oa, ob, z2, wa, wb, wo, nw, tm):
    m = x2d.shape[0]
    return pl.pallas_call(
        _merge_body,
        grid=(m // tm,),
        in_specs=_merge_in_specs(tm, wa, wb, wo),
        out_specs=[pl.BlockSpec((tm, D_MODEL), lambda i: (i, 0)),
                   pl.BlockSpec((tm, D_MODEL), lambda i: (i, 0))],
        out_shape=[jax.ShapeDtypeStruct((m, D_MODEL), F32),
                   jax.ShapeDtypeStruct((m, D_MODEL), wo.dtype)],
        compiler_params=_params("arbitrary"),
        name="merge",
    )(x2d, oa, ob, z2, z2, wa, wb, wo, nw)


FFN_CHUNKS = (768, 768, 768, 512)


def _merge_ffn_body(x_ref, oa_ref, ob_ref, ga_ref, gb_ref, wa_ref, wb_ref, wo_ref, nw_ref,
                    wg_ref, wu_ref, wd_ref, o_ref, acc_ref):
    x1 = _merge_value(x_ref, oa_ref, ob_ref, ga_ref, gb_ref, wa_ref, wb_ref, wo_ref)
    acc_ref[...] = x1
    hn = _rms(x1, nw_ref[...]).astype(wg_ref.dtype)
    off = 0
    for width in FFN_CHUNKS:
        g = _dot(hn, wg_ref[:, off:off + width])
        u = _dot(hn, wu_ref[:, off:off + width])
        acc_ref[...] += _dot((g * _sigmoid(g) * u).astype(wd_ref.dtype), wd_ref[off:off + width, :])
        off += width
    o_ref[...] = acc_ref[...]


def _merge_ffn(x2d, oa, ob, z2, wa, wb, wo, nw, wg, wu, wd, tm):
    m = x2d.shape[0]
    return pl.pallas_call(
        _merge_ffn_body,
        grid=(m // tm,),
        in_specs=_merge_in_specs(tm, wa, wb, wo) + [_resident(wg.shape), _resident(wu.shape), _resident(wd.shape)],
        out_specs=pl.BlockSpec((tm, D_MODEL), lambda i: (i, 0)),
        out_shape=jax.ShapeDtypeStruct((m, D_MODEL), F32),
        scratch_shapes=[pltpu.VMEM((tm, D_MODEL), F32)],
        compiler_params=_params("arbitrary"),
        name="merge_ffn",
    )(x2d, oa, ob, z2, z2, wa, wb, wo, nw, wg, wu, wd)


def _ffn_stream_body(hn_ref, x1_ref, wg_ref, wu_ref, wd_ref, o_ref):
    @pl.when(pl.program_id(0) == 0)
    def _():
        o_ref[...] = x1_ref[...]

    hn = hn_ref[...]
    g = _dot(hn, wg_ref[...])
    u = _dot(hn, wu_ref[...])
    o_ref[...] += _dot((g * _sigmoid(g) * u).astype(wd_ref.dtype), wd_ref[...])


def _ffn_stream(hn, x1, wg, wu, wd, tf):
    m = hn.shape[0]
    full = pl.BlockSpec((m, D_MODEL), lambda f: (0, 0))
    return pl.pallas_call(
        _ffn_stream_body,
        grid=(wg.shape[1] // tf,),
        in_specs=[full, full,
                  pl.BlockSpec((D_MODEL, tf), lambda f: (0, f)),
                  pl.BlockSpec((D_MODEL, tf), lambda f: (0, f)),
                  pl.BlockSpec((tf, D_MODEL), lambda f: (f, 0))],
        out_specs=full,
        out_shape=jax.ShapeDtypeStruct((m, D_MODEL), F32),
        compiler_params=_params("arbitrary"),
        name="ffn_stream",
    )(hn, x1, wg, wu, wd)


RT_E1, RT_E2, RT_G1, RT_G2, RT_R1, RT_R2 = range(6)
SEG_ALIGN = 16
EXPERT_CHUNKS = (1024, 1024, 1024, 512)


def _router_body(hn_ref, wr_ref, route_ref, route_t_ref, cnt_ref):
    T = hn_ref.shape[0]
    logits = _dot(hn_ref[...], wr_ref[...])
    lane = lax.broadcasted_iota(jnp.int32, logits.shape, 1).astype(F32)
    lg = jnp.where(lane < N_EXPERTS, logits, -jnp.inf)
    m1 = jnp.max(lg, axis=-1, keepdims=True)
    i1 = jnp.min(jnp.where(lg == m1, lane, float(LANES)), axis=-1, keepdims=True)
    lg2 = jnp.where(lane == i1, -jnp.inf, lg)
    m2 = jnp.max(lg2, axis=-1, keepdims=True)
    i2 = jnp.min(jnp.where(lg2 == m2, lane, float(LANES)), axis=-1, keepdims=True)
    e2 = jnp.exp(m2 - m1)
    sel = jnp.where((lane == i1) | (lane == i2), 1.0, 0.0)
    r = lax.broadcasted_iota(jnp.int32, (T, T), 0)
    c = lax.broadcasted_iota(jnp.int32, (T, T), 1)
    before = jnp.where(r > c, 1.0, 0.0).astype(BF16)
    rank = _dot(before, sel.astype(BF16))
    r1 = jnp.sum(jnp.where(lane == i1, rank, 0.0), axis=-1, keepdims=True)
    r2 = jnp.sum(jnp.where(lane == i2, rank, 0.0), axis=-1, keepdims=True)
    fields = {RT_E1: i1, RT_E2: i2, RT_G1: 1.0 / (1.0 + e2), RT_G2: e2 / (1.0 + e2), RT_R1: r1, RT_R2: r2}
    route = jnp.zeros_like(logits)
    for idx, val in fields.items():
        route = jnp.where(lane == float(idx), val, route)
    route_ref[...] = route
    route_t_ref[...] = route.T
    cnt_ref[...] = jnp.broadcast_to(jnp.sum(sel, axis=0, keepdims=True), cnt_ref.shape)


def _router(hn, wr, T):
    m = hn.shape[0]
    nt = m // T
    return pl.pallas_call(
        _router_body,
        grid=(nt,),
        in_specs=[pl.BlockSpec((T, D_MODEL), lambda i: (i, 0)),
                  pl.BlockSpec((D_MODEL, LANES), lambda i: (0, 0))],
        out_specs=[pl.BlockSpec((T, LANES), lambda i: (i, 0)),
                   pl.BlockSpec((LANES, T), lambda i: (0, i)),
                   pl.BlockSpec((None, SUBLANES, LANES), lambda i: (i, 0, 0))],
        out_shape=[jax.ShapeDtypeStruct((m, LANES), F32),
                   jax.ShapeDtypeStruct((LANES, m), F32),
                   jax.ShapeDtypeStruct((nt, SUBLANES, LANES), F32)],
        compiler_params=_params("arbitrary"),
        name="router",
    )(hn, wr)


def _segment_copies(seg_ref, so_ref, off_ref, step, make, fn, max_rows):
    sizes = []
    s = max_rows
    while s >= SEG_ALIGN:
        sizes.append(s)
        s //= 2
    for e in range(N_EXPERTS):
        n = seg_ref[step * N_EXPERTS + e]
        src = so_ref[step * N_EXPERTS + e]
        dst = off_ref[step * N_EXPERTS + e]
        for s in sizes:
            @pl.when((n & s) != 0)
            def _(s=s, n=n, src=src, dst=dst):
                done = n & (-2 * s)
                fn(make(pl.multiple_of(src + done, SEG_ALIGN), pl.multiple_of(dst + done, SEG_ALIGN), s))


def _stage_offset(so_ref, step, expert):
    out = jnp.zeros_like(expert)
    for e in range(N_EXPERTS):
        out = jnp.where(expert == float(e), so_ref[step * N_EXPERTS + e].astype(F32), out)
    return out


def _dispatch_body(seg_ref, so_ref, off_ref, hn_ref, rt_ref, init_ref, xs_ref, stage, sem):
    del init_ref
    t = pl.program_id(0)
    nt = pl.num_programs(0)
    T = hn_ref.shape[0]
    S = stage.shape[1]
    slot = t % 2

    def copies(step, slot, fn):
        def make(src, dst, rows):
            return pltpu.make_async_copy(stage.at[slot, pl.ds(src, rows)], xs_ref.at[pl.ds(dst, rows)], sem.at[slot])
        _segment_copies(seg_ref, so_ref, off_ref, step, make, fn, T)

    @pl.when(t >= 2)
    def _():
        copies(t - 2, slot, lambda c: c.wait())

    rt = rt_ref[...]
    tgt1 = _stage_offset(so_ref, t, rt[RT_E1:RT_E1 + 1, :]) + rt[RT_R1:RT_R1 + 1, :]
    tgt2 = _stage_offset(so_ref, t, rt[RT_E2:RT_E2 + 1, :]) + rt[RT_R2:RT_R2 + 1, :]
    rowi = lax.broadcasted_iota(jnp.int32, (S, T), 0).astype(F32)
    sel = jnp.where((rowi == tgt1) | (rowi == tgt2), 1.0, 0.0).astype(BF16)
    stage[slot] = _dot(sel, hn_ref[...]).astype(BF16)
    copies(t, slot, lambda c: c.start())

    @pl.when(t == nt - 1)
    def _():
        copies(t, slot, lambda c: c.wait())

        @pl.when(t >= 1)
        def _():
            copies(t - 1, 1 - slot, lambda c: c.wait())


def _dispatch(hn, route_t, plan, T, S, rows_total):
    m = hn.shape[0]
    grid_spec = pltpu.PrefetchScalarGridSpec(
        num_scalar_prefetch=3,
        grid=(m // T,),
        in_specs=[pl.BlockSpec((T, D_MODEL), lambda t, *_: (t, 0)),
                  pl.BlockSpec((LANES, T), lambda t, *_: (0, t)),
                  pl.BlockSpec(memory_space=pl.ANY)],
        out_specs=pl.BlockSpec(memory_space=pl.ANY),
        scratch_shapes=[pltpu.VMEM((2, S, D_MODEL), BF16), pltpu.SemaphoreType.DMA((2,))],
    )
    return pl.pallas_call(
        _dispatch_body,
        grid_spec=grid_spec,
        out_shape=jax.ShapeDtypeStruct((rows_total, D_MODEL), BF16),
        input_output_aliases={5: 0},
        compiler_params=_params("arbitrary"),
        name="moe_dispatch",
    )(plan["seg"], plan["so"], plan["off"], hn, route_t, jnp.zeros((rows_total, D_MODEL), BF16))


def _experts_body(owner_ref, used_ref, x_ref, wg_ref, wu_ref, wd_ref, y_ref, acc_ref):
    del owner_ref

    @pl.when(pl.program_id(0) < used_ref[0])
    def _():
        x = x_ref[...]
        off = 0
        for i, width in enumerate(EXPERT_CHUNKS):
            g = _dot(x, wg_ref[:, off:off + width])
            u = _dot(x, wu_ref[:, off:off + width])
            y = _dot((g * _sigmoid(g) * u).astype(BF16), wd_ref[off:off + width, :])
            if i == 0:
                acc_ref[...] = y
            else:
                acc_ref[...] += y
            off += width
        y_ref[...] = acc_ref[...].astype(y_ref.dtype)

    @pl.when(pl.program_id(0) >= used_ref[0])
    def _():
        y_ref[...] = jnp.zeros_like(y_ref)


def _experts(xs, plan, wg, wu, wd, bm):
    nblk = xs.shape[0] // bm
    buffers = 1 if bm >= 256 else 2

    def wspec(shape):
        return pl.BlockSpec((None,) + shape, lambda b, owner, used: (owner[b], 0, 0),
                            pipeline_mode=pl.Buffered(buffers))

    grid_spec = pltpu.PrefetchScalarGridSpec(
        num_scalar_prefetch=2,
        grid=(nblk,),
        in_specs=[pl.BlockSpec((bm, D_MODEL), lambda b, owner, used: (jnp.minimum(b, used[0] - 1), 0)),
                  wspec((D_MODEL, D_FF_EXPERT)), wspec((D_MODEL, D_FF_EXPERT)), wspec((D_FF_EXPERT, D_MODEL))],
        out_specs=pl.BlockSpec((bm, D_MODEL), lambda b, owner, used: (b, 0)),
        scratch_shapes=[pltpu.VMEM((bm, D_MODEL), F32)],
    )
    return pl.pallas_call(
        _experts_body,
        grid_spec=grid_spec,
        out_shape=jax.ShapeDtypeStruct(xs.shape, BF16),
        compiler_params=_params("arbitrary"),
        name="moe_experts",
    )(plan["owner"], plan["used"], xs, wg, wu, wd)


def _combine_body(seg_ref, so_ref, off_ref, route_ref, x1_ref, nw_ref, ys_ref, o_ref, ybuf, sem, *, final_norm):
    t = pl.program_id(0)
    nt = pl.num_programs(0)
    T = route_ref.shape[0]
    S = ybuf.shape[1]
    slot = t % 2

    def copies(step, slot, fn):
        def make(stage_row, grouped_row, rows):
            return pltpu.make_async_copy(ys_ref.at[pl.ds(grouped_row, rows)], ybuf.at[slot, pl.ds(stage_row, rows)],
                                         sem.at[slot])
        _segment_copies(seg_ref, so_ref, off_ref, step, make, fn, T)

    @pl.when(t == 0)
    def _():
        ybuf[...] = jnp.zeros_like(ybuf)
        copies(0, 0, lambda c: c.start())

    @pl.when(t + 1 < nt)
    def _():
        copies(t + 1, 1 - slot, lambda c: c.start())

    copies(t, slot, lambda c: c.wait())

    route = route_ref[...]
    tgt1 = _stage_offset(so_ref, t, route[:, RT_E1:RT_E1 + 1]) + route[:, RT_R1:RT_R1 + 1]
    tgt2 = _stage_offset(so_ref, t, route[:, RT_E2:RT_E2 + 1]) + route[:, RT_R2:RT_R2 + 1]
    coli = lax.broadcasted_iota(jnp.int32, (T, S), 1).astype(F32)
    sel = (jnp.where(coli == tgt1, route[:, RT_G1:RT_G1 + 1], 0.0)
           + jnp.where(coli == tgt2, route[:, RT_G2:RT_G2 + 1], 0.0))
    x2 = x1_ref[...] + _dot(sel.astype(BF16), ybuf[slot])
    o_ref[...] = _rms(x2, nw_ref[...]) if final_norm else x2


def _combine(ys, route, x1, nw, plan, T, S, final_norm):
    m = x1.shape[0]
    grid_spec = pltpu.PrefetchScalarGridSpec(
        num_scalar_prefetch=3,
        grid=(m // T,),
        in_specs=[pl.BlockSpec((T, LANES), lambda t, *_: (t, 0)),
                  pl.BlockSpec((T, D_MODEL), lambda t, *_: (t, 0)),
                  pl.BlockSpec((1, D_MODEL), lambda t, *_: (0, 0)),
                  pl.BlockSpec(memory_space=pl.ANY)],
        out_specs=pl.BlockSpec((T, D_MODEL), lambda t, *_: (t, 0)),
        scratch_shapes=[pltpu.VMEM((2, S, D_MODEL), BF16), pltpu.SemaphoreType.DMA((2,))],
    )
    return pl.pallas_call(
        functools.partial(_combine_body, final_norm=final_norm),
        grid_spec=grid_spec,
        out_shape=jax.ShapeDtypeStruct((m, D_MODEL), F32),
        compiler_params=_params("arbitrary"),
        name="moe_combine",
    )(plan["seg"], plan["so"], plan["off"], route, x1, nw, ys)


def _moe_plan(cnt, bm, nblk):
    seg = (cnt + (SEG_ALIGN - 1)) // SEG_ALIGN * SEG_ALIGN
    so = jnp.cumsum(seg, axis=1) - seg
    blocks = (jnp.sum(seg, axis=0) + (bm - 1)) // bm
    blk_end = jnp.cumsum(blocks)
    off = (blk_end - blocks)[None, :] * bm + jnp.cumsum(seg, axis=0) - seg
    owner = jnp.minimum(jnp.sum(jnp.arange(nblk)[:, None] >= blk_end[None, :], axis=1), N_EXPERTS - 1)
    as_i32 = lambda a: a.astype(jnp.int32).reshape(-1)
    return {"seg": as_i32(seg), "so": as_i32(so), "off": as_i32(off), "owner": as_i32(owner),
            "used": as_i32(blk_end[-1:])}


def _moe_ffn(hn, x1, nw, wr, wg, wu, wd, T, bm, final_norm):
    m = hn.shape[0]
    tiles = m // T
    pad_rows = tiles * N_EXPERTS * (SEG_ALIGN - 1)
    stage_rows = -(-(2 * T + N_EXPERTS * (SEG_ALIGN - 1)) // LANES) * LANES
    nblk = -(-(2 * m + pad_rows) // bm) + N_EXPERTS
    route, route_t, cnt = _router(hn, wr, T)
    plan = _moe_plan(cnt[:, 0, :N_EXPERTS].astype(jnp.int32), bm, nblk)
    xs = _dispatch(hn.astype(BF16), route_t, plan, T, stage_rows, nblk * bm)
    ys = _experts(xs, plan, wg, wu, wd, bm)
    return _combine(ys, route, x1, nw, plan, T, stage_rows, final_norm)


def _norm_body(x_ref, nw_ref, o_ref):
    o_ref[...] = _rms(x_ref[...], nw_ref[...])


def _norm(x, nw, tm):
    m = x.shape[0]
    spec = pl.BlockSpec((tm, D_MODEL), lambda i: (i, 0))
    return pl.pallas_call(
        _norm_body, grid=(m // tm,),
        in_specs=[spec, pl.BlockSpec((1, D_MODEL), lambda i: (0, 0))],
        out_specs=spec, out_shape=jax.ShapeDtypeStruct((m, D_MODEL), F32),
        compiler_params=_params("arbitrary"), name="norm",
    )(x, nw)


def _layout_w_in(w):
    o = 0
    hg = w[:, o:o + 4 * HG_W]; o += 4 * HG_W
    sq = w[:, o:o + SWQ_W]; o += SWQ_W
    sk = w[:, o:o + KV_W]; o += KV_W
    sv = w[:, o:o + KV_W]; o += KV_W
    ga = w[:, o:o + D_MODEL]; o += D_MODEL
    gb = w[:, o:o + D_MODEL]
    pairs = [sq[:, h * SW_HD:(h + 1) * SW_HD] for j in range(SW_PAIRS) for h in (j, j + SW_GROUP)]
    return jnp.concatenate([hg, ga, gb] + pairs + [sk, sv], axis=1)


def _layout_w_branch_b(w):
    return jnp.concatenate([w[h * SW_HD:(h + 1) * SW_HD] for j in range(SW_PAIRS) for h in (j, j + SW_GROUP)], axis=0)


def _rope_tables(pos):
    half = SW_HD // 2
    inv = ROPE_THETA ** (-jnp.arange(half, dtype=F32) / half)
    ang = pos.astype(F32)[:, None] * inv[None, :]
    cos = jnp.cos(ang)
    sin = jnp.sin(ang)
    reps = LANES // SW_HD
    return jnp.tile(cos, (1, 2 * reps)), jnp.tile(jnp.concatenate([-sin, sin], axis=1), (1, reps))


def _rotate_half_matrix():
    j = jnp.arange(LANES)
    src = jnp.where((j % SW_HD) < SW_HD // 2, j + SW_HD // 2, j - SW_HD // 2)
    return (jnp.arange(LANES)[:, None] == src[None, :]).astype(F32)


def _tile(m, pref):
    return pref if m % pref == 0 else m


def kernel(x_prompt, x_sample, state_hgrn, cache_swa_k, cache_swa_v, norm_mix, w_in, hg_lb_logits, hg_norm,
           swa_sinks, w_branch_a, w_branch_b, w_out, norm_ffn, w_gate_dense, w_up_dense, w_down_dense,
           w_router, w_gate_moe, w_up_moe, w_down_moe, norm_final):
    depth = w_in.shape[0]
    bsz, seq, _ = x_prompt.shape
    nsamp = x_sample.shape[0]
    mp = bsz * seq
    wb = cache_swa_k.shape[2]

    cos_p, sin_p = _rope_tables(jnp.arange(seq))
    cos_s, sin_s = _rope_tables(PAST_LEN + jnp.arange(1))
    perm32 = _rotate_half_matrix()
    perm = perm32.astype(BF16)
    lbl = hg_lb_logits.astype(F32)
    state32 = state_hgrn.astype(F32)
    kcache = cache_swa_k.reshape(depth, nsamp, wb, KV_W).astype(F32)
    vcache = cache_swa_v.reshape(depth, nsamp, wb, KV_W).astype(F32)

    xp = x_prompt.reshape(mp, D_MODEL)
    xs = x_sample.reshape(nsamp, D_MODEL)
    outs = {k: [] for k in ("sp", "kp", "vp", "ss", "ks", "vs")}
    nfin = norm_final.reshape(1, D_MODEL)
    normed = False

    for l in range(depth):
        w_in32 = _layout_w_in(w_in[l])
        wa32 = w_branch_a[l]
        wbb32 = _layout_w_branch_b(w_branch_b[l])
        wo32 = w_out[l]
        w_in_l = w_in32.astype(BF16)
        wa = wa32.astype(BF16)
        wbb = wbb32.astype(BF16)
        wo = wo32.astype(BF16)
        nmix = norm_mix[l].reshape(1, D_MODEL)
        nffn = norm_ffn[l].reshape(1, D_MODEL)
        hgn = hg_norm[l].reshape(1, HG_D)
        sinks = swa_sinks[l].astype(F32)

        zp = _proj(xp, nmix, w_in_l, _tile(mp, 512))
        zp3 = zp.reshape(bsz, seq, Z_WIDTH)
        oa_p, s_p = _hgrn_prompt(zp3, lbl, hgn, l)
        ob_p, k_p, v_p = _swa_prompt(zp3, sinks, cos_p, sin_p, perm)
        oa_p = oa_p.reshape(mp, HG_W)
        ob_p = ob_p.reshape(mp, SWQ_W)

        zs = _proj(xs, nmix, w_in32, nsamp)
        oa_s, s_s = _hgrn_decode(zs, state32, lbl, hgn, l)
        q2 = zs[:, ZB_SWQ * SWQ_W:(ZB_SWQ + 1) * SWQ_W].reshape(nsamp * SW_PAIRS, LANES)
        sink_row = jnp.pad(sinks, (0, LANES - SW_Q_HEADS)).reshape(1, LANES)
        ob_s, k_s, v_s = _swa_decode(q2, zs, kcache, vcache, l, cos_s, sin_s, perm32, sink_row)
        x1s, hns = _merge(xs, oa_s, ob_s.reshape(nsamp, SWQ_W), zs, wa32, wbb32, wo32, nffn, nsamp)

        j = l // 2
        if l % 2 == 0:
            xp = _merge_ffn(xp, oa_p, ob_p, zp, wa, wbb, wo, nffn, w_gate_dense[j].astype(BF16),
                            w_up_dense[j].astype(BF16), w_down_dense[j].astype(BF16), _tile(mp, 512))
            xs = _ffn_stream(hns, x1s, w_gate_dense[j], w_up_dense[j], w_down_dense[j], 256)
        else:
            x1p, hnp = _merge(xp, oa_p, ob_p, zp, wa, wbb, wo, nffn, _tile(mp, 512))
            wr32 = jnp.pad(w_router[j], ((0, 0), (0, LANES - N_EXPERTS)))
            wg = w_gate_moe[j].astype(BF16)
            wu = w_up_moe[j].astype(BF16)
            wd = w_down_moe[j].astype(BF16)
            last = l == depth - 1
            xp = _moe_ffn(hnp, x1p, nfin, wr32.astype(BF16), wg, wu, wd, _tile(mp, 512), 512, last)
            xs = _moe_ffn(hns, x1s, nfin, wr32, wg, wu, wd, nsamp, 128, last)
            normed = last

        outs["sp"].append(s_p)
        outs["kp"].append(k_p.reshape(bsz, WINDOW, SW_KV_HEADS, SW_HD))
        outs["vp"].append(v_p.reshape(bsz, WINDOW, SW_KV_HEADS, SW_HD))
        outs["ss"].append(s_s.astype(state_hgrn.dtype))
        outs["ks"].append(k_s.reshape(nsamp, wb, SW_KV_HEADS, SW_HD))
        outs["vs"].append(v_s.reshape(nsamp, wb, SW_KV_HEADS, SW_HD))

    yp, ys = (xp, xs) if normed else (_norm(xp, nfin, _tile(mp, 1024)), _norm(xs, nfin, nsamp))

    return (yp.reshape(x_prompt.shape), ys.reshape(x_sample.shape),
            jnp.stack(outs["sp"]), jnp.stack(outs["kp"]), jnp.stack(outs["vp"]),
            jnp.stack(outs["ss"]), jnp.stack(outs["ks"]), jnp.stack(outs["vs"]))
```

```python
import functools

import jax
import jax.numpy as jnp
from jax import lax
from jax.experimental import pallas as pl
from jax.experimental.pallas import tpu as pltpu

F32 = jnp.float32
BF16 = jnp.bfloat16

D_MODEL = 1024
PAST_LEN = 16384
HG_HEADS = 4
HG_D = 128
HG_W = HG_HEADS * HG_D
SW_Q_HEADS = 8
SW_KV_HEADS = 2
SW_HD = 64
SW_GROUP = SW_Q_HEADS // SW_KV_HEADS
WINDOW = 128
ROPE_THETA = 10000.0
D_FF_DENSE = 2816
N_EXPERTS = 8
D_FF_EXPERT = 3584
RMS_EPS = 1e-6

LANES = 128
SUBLANES = 8
VMEM_LIMIT = 56 * 1024 * 1024

SWQ_W = SW_Q_HEADS * SW_HD
SW_PAIRS = SWQ_W // LANES
KV_W = SW_KV_HEADS * SW_HD
Z_WIDTH = 4 * HG_W + 2 * D_MODEL + SWQ_W + 2 * KV_W
ZB_Q, ZB_F, ZB_I, ZB_G = 0, 1, 2, 3
N_HG_COLS = 4 * HG_W
R_GA, R_GB, R_SWQ, R_K, R_V = 0, D_MODEL, 2 * D_MODEL, 2 * D_MODEL + SWQ_W, 2 * D_MODEL + SWQ_W + KV_W
PROJ_CHUNKS = (1024, 1024, 1024, 1024, 768)

HG_CHUNK = 128
SW_UNROLL = 4
SEQ_BLOCK = 512
DEC_TILE = 8


def _dot_dims(a, b, dims):
    precision = lax.Precision.HIGHEST if a.dtype == F32 else None
    return lax.dot_general(a, b, (dims, ((), ())), precision=precision, preferred_element_type=F32)


def _dot(a, b):
    return _dot_dims(a, b, ((1,), (0,)))


def _dot_nt(a, b):
    return _dot_dims(a, b, ((1,), (1,)))


def _dot_tn(a, b):
    return _dot_dims(a, b, ((0,), (0,)))


def _sigmoid(x):
    return 1.0 / (1.0 + jnp.exp(-x))


def _rms(x, w):
    ms = jnp.mean(x * x, axis=-1, keepdims=True)
    return x * lax.rsqrt(ms + RMS_EPS) * w


def _params(*sem):
    return pltpu.CompilerParams(dimension_semantics=sem, vmem_limit_bytes=VMEM_LIMIT)


def _proj_body(x_ref, nw_ref, w_ref, z_ref):
    h = _rms(x_ref[...], nw_ref[...]).astype(w_ref.dtype)
    off = 0
    for width in PROJ_CHUNKS:
        z_ref[:, off:off + width] = _dot(h, w_ref[:, off:off + width]).astype(z_ref.dtype)
        off += width


def _proj(x2d, nw, w, tm):
    m = x2d.shape[0]
    n = w.shape[1]
    return pl.pallas_call(
        _proj_body,
        grid=(m // tm,),
        in_specs=[
            pl.BlockSpec((tm, D_MODEL), lambda i: (i, 0)),
            pl.BlockSpec((1, D_MODEL), lambda i: (0, 0)),
            pl.BlockSpec((D_MODEL, n), lambda i: (0, 0), pipeline_mode=pl.Buffered(1)),
        ],
        out_specs=pl.BlockSpec((tm, n), lambda i: (i, 0)),
        out_shape=jax.ShapeDtypeStruct((m, n), w.dtype),
        compiler_params=_params("arbitrary"),
        name="proj",
    )(x2d, nw, w)


def _lower_bound(lbl, layer):
    mx = jnp.max(lbl, axis=0, keepdims=True)
    e = jnp.exp(lbl - mx)
    sm = e / jnp.sum(e, axis=0, keepdims=True)
    cum = sm[0:1, :]
    for i in range(1, layer + 1):
        cum = cum + sm[i:i + 1, :]
    return cum - sm[0:1, :]


def _split3(x):
    hi = x.astype(BF16)
    r = x - hi.astype(F32)
    mid = r.astype(BF16)
    lo = (r - mid.astype(F32)).astype(BF16)
    return hi, mid, lo


def _hgrn_tile(zh_ref, lb, nw, o_ref, st_scr, b_all_scr):
    C = HG_CHUNK
    col_q, col_f, col_i, col_g = (slice(b * HG_W, (b + 1) * HG_W) for b in (ZB_Q, ZB_F, ZB_I, ZB_G))
    row = lax.broadcasted_iota(jnp.int32, (C, C), 0)
    col = lax.broadcasted_iota(jnp.int32, (C, C), 1)
    xr = row ^ col
    tri = jnp.where(row >= col, 1.0, 0.0).astype(BF16)
    sub4 = (lax.broadcasted_iota(jnp.int32, (SUBLANES, HG_D), 0) & 4) == 0
    levels = (1, 2, 4, 8, 16, 32, 64)
    pair_level = {m: (xr >= m) & (xr < 2 * m) for m in levels}
    upper_half = {m: (row & m) != 0 for m in levels if m < SUBLANES}

    def chunk(rows, b_scr):
        hf = zh_ref[rows, col_f].astype(F32)
        fg_all = lb + (1.0 - lb) * _sigmoid(hf)
        g_all = jnp.log2(fg_all)
        k_all = 1.0 - fg_all
        g1, g2, g3 = _split3(g_all)
        b_all = _dot(tri, g1) + _dot(tri, g2) + _dot(tri, g3)
        b_scr[...] = b_all
        hq = zh_ref[rows, col_q].astype(F32)
        q_all = hq * _sigmoid(hq) * (HG_D ** -0.5)
        gate = zh_ref[rows, col_g].astype(F32)
        gate_all = gate * _sigmoid(gate)
        for h in range(HG_HEADS):
            sl = slice(h * HG_D, (h + 1) * HG_D)
            head(h, rows, b_scr, b_all[:, sl], q_all[:, sl], k_all[:, sl], fg_all[:, sl], gate_all[:, sl])

    def head(h, rows, b_scr, b, q, k, f, gate):
        sl = slice(h * HG_D, (h + 1) * HG_D)
        v = zh_ref[rows, col_i.start + h * HG_D:col_i.start + (h + 1) * HG_D]

        acc = _dot_nt(q.astype(BF16), k.astype(BF16))
        for m in levels:
            if m == 1:
                w = jnp.where(upper_half[m], q * f, k)
            elif m < SUBLANES:
                if m == 2:
                    pieces = []
                    for j in range(C // SUBLANES):
                        lo = jnp.broadcast_to(b_scr[SUBLANES * j + 1:SUBLANES * j + 2, sl], (SUBLANES, HG_D))
                        hi = jnp.broadcast_to(b_scr[SUBLANES * j + 5:SUBLANES * j + 6, sl], (SUBLANES, HG_D))
                        pieces.append(jnp.where(sub4, lo, hi))
                else:
                    pieces = [jnp.broadcast_to(b_scr[i * 2 * m + m - 1:i * 2 * m + m, sl], (2 * m, HG_D))
                              for i in range(C // (2 * m))]
                d = b - jnp.concatenate(pieces, axis=0)
                w = jnp.where(upper_half[m], q, k) * jnp.exp2(jnp.where(upper_half[m], d, -d))
            else:
                expo, qk = [], []
                for i in range(C // (2 * m)):
                    lo, mid, hi = i * 2 * m, i * 2 * m + m, (i + 1) * 2 * m
                    bref = b_scr[mid - 1:mid, sl]
                    expo += [bref - b[lo:mid], b[mid:hi] - bref]
                    qk += [k[lo:mid], q[mid:hi]]
                w = jnp.concatenate(qk, axis=0) * jnp.exp2(jnp.concatenate(expo, axis=0))
            wb = w.astype(BF16)
            acc = jnp.where(pair_level[m], _dot_nt(wb, wb), acc)
        a = jnp.where(row >= col, acc, 0.0)

        st = st_scr[h]
        o = _dot(a.astype(BF16), v) + _dot_nt((q * jnp.exp2(b)).astype(BF16), st.astype(BF16))
        b_last = b[C - 1:C, :]
        kd = k * jnp.exp2(b_last - b)
        st_scr[h] = jnp.exp2(b_last) * st + _dot_tn(v, kd.astype(BF16))

        o_ref[rows, sl] = (_rms(o, nw) * gate).astype(o_ref.dtype)

    for u in range(zh_ref.shape[0] // C):
        chunk(slice(u * C, (u + 1) * C), b_all_scr.at[u])


def _hgrn_prompt_body(zh_ref, lbl_ref, nw_ref, *refs, layer, n_cast):
    cast_in = refs[:n_cast]
    o_ref, s_ref = refs[n_cast:n_cast + 2]
    cast_out = refs[n_cast + 2:2 * n_cast + 2]
    st_scr, b_all_scr = refs[2 * n_cast + 2:]
    c = pl.program_id(1)

    @pl.when(c == 0)
    def _():
        st_scr[...] = jnp.zeros_like(st_scr)

    for src, dst in zip(cast_in, cast_out):
        dst[...] = src[...].astype(dst.dtype)

    _hgrn_tile(zh_ref, _lower_bound(lbl_ref[...], layer), nw_ref[...], o_ref, st_scr, b_all_scr)

    @pl.when(c == pl.num_programs(1) - 1)
    def _():
        for h in range(HG_HEADS):
            s_ref[h] = st_scr[h].T


def _hgrn_prompt(z3, lbl, nw, layer, cast=()):
    bsz, seq, _ = z3.shape
    rows = _tile(seq, SEQ_BLOCK)
    steps_per_seq = seq // rows
    steps = bsz * steps_per_seq

    def slab(a):
        assert a.shape[0] % (steps * 2 * SUBLANES) == 0, a.shape
        return pl.BlockSpec((a.shape[0] // steps, a.shape[1]), lambda b, c: (b * steps_per_seq + c, 0))

    outs = pl.pallas_call(
        functools.partial(_hgrn_prompt_body, layer=layer, n_cast=len(cast)),
        grid=(bsz, steps_per_seq),
        in_specs=[pl.BlockSpec((None, rows, N_HG_COLS), lambda b, c: (b, c, 0)),
                  pl.BlockSpec(lbl.shape, lambda b, c: (0, 0)),
                  pl.BlockSpec((1, HG_D), lambda b, c: (0, 0))] + [slab(a) for a in cast],
        out_specs=[pl.BlockSpec((None, rows, HG_W), lambda b, c: (b, c, 0)),
                   pl.BlockSpec((None, HG_HEADS, HG_D, HG_D), lambda b, c: (b, 0, 0, 0))] + [slab(a) for a in cast],
        out_shape=[jax.ShapeDtypeStruct((bsz, seq, HG_W), BF16),
                   jax.ShapeDtypeStruct((bsz, HG_HEADS, HG_D, HG_D), F32)]
        + [jax.ShapeDtypeStruct(a.shape, BF16) for a in cast],
        scratch_shapes=[pltpu.VMEM((HG_HEADS, HG_D, HG_D), F32),
                        pltpu.VMEM((rows // HG_CHUNK, HG_CHUNK, HG_W), F32)],
        compiler_params=_params("arbitrary", "arbitrary"),
        name="hgrn_prompt",
    )(z3, lbl, nw, *cast)
    return outs[0], outs[1], outs[2:]


def _hgrn_decode_body(zq_ref, zf_ref, zi_ref, zg_ref, lbl_ref, nw_ref, s_ref, o_ref, sn_ref, o_scr, *, layer):
    lb = _lower_bound(lbl_ref[...], layer)
    fg = lb + (1.0 - lb) * _sigmoid(zf_ref[...].astype(F32))
    kk = 1.0 - fg
    hq = zq_ref[...].astype(F32)
    q = hq * _sigmoid(hq) * (HG_D ** -0.5)
    v = zi_ref[...].astype(F32)
    eye = (lax.broadcasted_iota(jnp.int32, (HG_D, HG_D), 0)
           == lax.broadcasted_iota(jnp.int32, (HG_D, HG_D), 1))

    def column(r):
        return jnp.sum(jnp.where(eye, jnp.broadcast_to(r, (HG_D, HG_D)), 0.0), axis=1, keepdims=True)

    for i in range(DEC_TILE):
        for h in range(HG_HEADS):
            sl = slice(h * HG_D, (h + 1) * HG_D)
            sn = column(fg[i:i + 1, sl]) * s_ref[i, h] + column(kk[i:i + 1, sl]) * v[i:i + 1, sl]
            sn_ref[i, h] = sn
            o_scr[i:i + 1, sl] = jnp.sum(column(q[i:i + 1, sl]) * sn, axis=0, keepdims=True)

    gate = zg_ref[...].astype(F32)
    gate = gate * _sigmoid(gate)
    nw = nw_ref[...]
    for h in range(HG_HEADS):
        sl = slice(h * HG_D, (h + 1) * HG_D)
        o_ref[:, sl] = (_rms(o_scr[:, sl], nw) * gate[:, sl]).astype(o_ref.dtype)


def _hgrn_decode(z2, state, lbl, nw, layer):
    n = z2.shape[0]
    T = DEC_TILE

    def zspec(blk):
        return pl.BlockSpec((T, HG_W), lambda i: (i, blk))

    sblock = (T, HG_HEADS, HG_D, HG_D)
    return pl.pallas_call(
        functools.partial(_hgrn_decode_body, layer=layer),
        grid=(n // T,),
        in_specs=[zspec(ZB_Q), zspec(ZB_F), zspec(ZB_I), zspec(ZB_G),
                  pl.BlockSpec(lbl.shape, lambda i: (0, 0)),
                  pl.BlockSpec((1, HG_D), lambda i: (0, 0)),
                  pl.BlockSpec((None,) + sblock, lambda i: (layer, i, 0, 0, 0))],
        out_specs=[pl.BlockSpec((T, HG_W), lambda i: (i, 0)), pl.BlockSpec(sblock, lambda i: (i, 0, 0, 0))],
        out_shape=[jax.ShapeDtypeStruct((n, HG_W), z2.dtype),
                   jax.ShapeDtypeStruct(state.shape[1:], F32)],
        scratch_shapes=[pltpu.VMEM((T, HG_W), F32)],
        compiler_params=_params("arbitrary"),
        name="hgrn_decode",
    )(z2, z2, z2, z2, lbl, nw, state)


def _rope(x, cos, sin, perm):
    return x.astype(F32) * cos + _dot(x, perm) * sin


def _swa_prompt_body(sinks_ref, zq_ref, zk_ref, zv_ref, cos_ref, sin_ref, perm_ref,
                     o_ref, kr_ref, vr_ref, kprev, vprev):
    n = pl.program_id(1)
    W = WINDOW

    @pl.when(n == 0)
    def _():
        kprev[...] = jnp.zeros_like(kprev)
        vprev[...] = jnp.zeros_like(vprev)

    perm = perm_ref[...]
    qi = lax.broadcasted_iota(jnp.int32, (W, 2 * W), 0)
    kj = lax.broadcasted_iota(jnp.int32, (W, 2 * W), 1)
    band = (kj >= qi) & (kj <= qi + W)
    low_half = lax.broadcasted_iota(jnp.int32, (W, LANES), 1) < SW_HD
    heads = range(SW_Q_HEADS)
    scale = SW_HD ** -0.5

    def blocks(i, carry):
        for u in range(SW_UNROLL):
            block(i * SW_UNROLL + u)
        return carry

    def block(i):
        rows = pl.ds(pl.multiple_of(i * W, W), W)
        cos = cos_ref[rows, :]
        sin = sin_ref[rows, :]
        k_rot = _rope(zk_ref[rows, :], cos, sin, perm)
        kr_ref[...] = k_rot
        vr_ref[...] = zv_ref[rows, :].astype(F32)
        k_cur = k_rot.astype(BF16)
        v_cur = zv_ref[rows, :]
        kk = jnp.concatenate([kprev[...], k_cur], axis=0)
        vv = jnp.concatenate([vprev[...], v_cur], axis=0)
        valid = band & (((n > 0) | (i > 0)) | (kj >= W))

        q_in = [zq_ref[rows, j * LANES:(j + 1) * LANES] for j in range(SW_PAIRS)]
        rot = _dot(jnp.concatenate(q_in, axis=0), perm)
        pair = [(q_in[j].astype(F32) * cos + rot[j * W:(j + 1) * W] * sin) * scale for j in range(SW_PAIRS)]
        q_rot = [jnp.where(low_half if h < SW_PAIRS else ~low_half, pair[h % SW_PAIRS], 0.0).astype(BF16)
                 for h in heads]
        s_all = _dot_nt(jnp.concatenate(q_rot, axis=0), kk)
        p_all, den_all = [], []
        for h in heads:
            s = jnp.where(valid, s_all[h * W:(h + 1) * W], -jnp.inf)
            sink = sinks_ref[h]
            mx = jnp.maximum(jnp.max(s, axis=-1, keepdims=True), sink)
            p = jnp.exp(s - mx)
            den_all.append(jnp.sum(p, axis=-1, keepdims=True) + jnp.exp(sink - mx))
            p_all.append(p.astype(BF16))
        o_all = _dot(jnp.concatenate(p_all, axis=0), vv)
        o_head = [o_all[h * W:(h + 1) * W] / den_all[h] for h in heads]
        for j in range(SW_PAIRS):
            o_pair = jnp.where(low_half, o_head[j], o_head[j + SW_PAIRS])
            o_ref[rows, j * LANES:(j + 1) * LANES] = o_pair.astype(o_ref.dtype)
        kprev[...] = k_cur
        vprev[...] = v_cur

    lax.fori_loop(0, zq_ref.shape[0] // (W * SW_UNROLL), blocks, 0)


def _swa_prompt(z3, sinks, cos, sin, perm):
    bsz, seq, _ = z3.shape
    W = WINDOW
    rows = _tile(seq, SEQ_BLOCK)
    grid_spec = pltpu.PrefetchScalarGridSpec(
        num_scalar_prefetch=1,
        grid=(bsz, seq // rows),
        in_specs=[pl.BlockSpec((None, rows, SWQ_W), lambda b, n, s: (b, n, (N_HG_COLS + R_SWQ) // SWQ_W)),
                  pl.BlockSpec((None, rows, KV_W), lambda b, n, s: (b, n, (N_HG_COLS + R_K) // KV_W)),
                  pl.BlockSpec((None, rows, KV_W), lambda b, n, s: (b, n, (N_HG_COLS + R_V) // KV_W)),
                  pl.BlockSpec((rows, LANES), lambda b, n, s: (n, 0)),
                  pl.BlockSpec((rows, LANES), lambda b, n, s: (n, 0)),
                  pl.BlockSpec((LANES, LANES), lambda b, n, s: (0, 0))],
        out_specs=[pl.BlockSpec((None, rows, SWQ_W), lambda b, n, s: (b, n, 0)),
                   pl.BlockSpec((None, W, KV_W), lambda b, n, s: (b, 0, 0)),
                   pl.BlockSpec((None, W, KV_W), lambda b, n, s: (b, 0, 0))],
        scratch_shapes=[pltpu.VMEM((W, KV_W), BF16), pltpu.VMEM((W, KV_W), BF16)],
    )
    return pl.pallas_call(
        _swa_prompt_body,
        grid_spec=grid_spec,
        out_shape=[jax.ShapeDtypeStruct((bsz, seq, SWQ_W), BF16),
                   jax.ShapeDtypeStruct((bsz, W, KV_W), F32),
                   jax.ShapeDtypeStruct((bsz, W, KV_W), F32)],
        compiler_params=_params("arbitrary", "arbitrary"),
        name="swa_prompt",
    )(sinks, z3, z3, z3, cos, sin, perm)


def _swa_decode_body(q_ref, zk_ref, zv_ref, kc_ref, vc_ref, cos_ref, sin_ref, perm_ref, sink_ref,
                     o_ref, kn_ref, vn_ref):
    cos = cos_ref[...]
    sin = sin_ref[...]
    perm = perm_ref[...]
    k_new = _rope(zk_ref[...], cos, sin, perm)
    v_new = zv_ref[...].astype(F32)
    sink = sink_ref[...]
    W = kc_ref.shape[1]
    for i in range(DEC_TILE):
        kn_ref[i, 0:W - 1, :] = kc_ref[i, 1:W, :]
        vn_ref[i, 0:W - 1, :] = vc_ref[i, 1:W, :]
        kn_ref[i, W - 1:W, :] = k_new[i:i + 1, :]
        vn_ref[i, W - 1:W, :] = v_new[i:i + 1, :]
    heads = range(SW_Q_HEADS)
    lane_row = lax.broadcasted_iota(jnp.int32, (1, LANES), 1)
    lane = lax.broadcasted_iota(jnp.int32, (W, LANES), 1)
    low_half = lane_row < SW_HD
    q_all = _rope(q_ref[...], cos, sin, perm) * (SW_HD ** -0.5)
    for i in range(DEC_TILE):
        keys = kc_ref[i]
        vals = vc_ref[i]
        s = jnp.zeros((W, LANES), F32)
        s_new = jnp.zeros((1, LANES), F32)
        q_heads = []
        for h in heads:
            r = i * SW_PAIRS + h % SW_PAIRS
            q_heads.append(jnp.where(low_half if h < SW_PAIRS else ~low_half, q_all[r:r + 1, :], 0.0))
        for h in heads:
            qh = q_heads[h]
            s = jnp.where(lane == h, jnp.sum(keys * qh, axis=-1, keepdims=True), s)
            s_new = jnp.where(lane_row == h, jnp.sum(k_new[i:i + 1, :] * qh, axis=-1, keepdims=True), s_new)
        mx = jnp.maximum(jnp.maximum(jnp.max(s, axis=0, keepdims=True), s_new), sink)
        p = jnp.exp(s - mx)
        p_new = jnp.exp(s_new - mx)
        den = jnp.sum(p, axis=0, keepdims=True) + p_new + jnp.exp(sink - mx)
        w_new = p_new / den
        wgt = p / den
        o_head = [jnp.sum(wgt[:, h:h + 1] * vals, axis=0, keepdims=True) + w_new[:, h:h + 1] * v_new[i:i + 1, :]
                  for h in heads]
        for j in range(SW_PAIRS):
            r = i * SW_PAIRS + j
            o_ref[r:r + 1, :] = jnp.where(low_half, o_head[j], o_head[j + SW_PAIRS]).astype(o_ref.dtype)


def _swa_decode(q2, z2, kc, vc, layer, cos, sin, perm, sink_row):
    n = z2.shape[0]
    T = DEC_TILE
    W = kc.shape[2]
    cache = pl.BlockSpec((None, T, W, KV_W), lambda i: (layer, i, 0, 0))
    return pl.pallas_call(
        _swa_decode_body,
        grid=(n // T,),
        in_specs=[pl.BlockSpec((T * SW_PAIRS, LANES), lambda i: (i, 0)),
                  pl.BlockSpec((T, KV_W), lambda i: (i, (N_HG_COLS + R_K) // KV_W)),
                  pl.BlockSpec((T, KV_W), lambda i: (i, (N_HG_COLS + R_V) // KV_W)),
                  cache, cache,
                  pl.BlockSpec((1, LANES), lambda i: (0, 0)),
                  pl.BlockSpec((1, LANES), lambda i: (0, 0)),
                  pl.BlockSpec((LANES, LANES), lambda i: (0, 0)),
                  pl.BlockSpec((1, LANES), lambda i: (0, 0))],
        out_specs=[pl.BlockSpec((T * SW_PAIRS, LANES), lambda i: (i, 0)),
                   pl.BlockSpec((T, W, KV_W), lambda i: (i, 0, 0)),
                   pl.BlockSpec((T, W, KV_W), lambda i: (i, 0, 0))],
        out_shape=[jax.ShapeDtypeStruct((n * SW_PAIRS, LANES), q2.dtype),
                   jax.ShapeDtypeStruct((n, W, KV_W), F32),
                   jax.ShapeDtypeStruct((n, W, KV_W), F32)],
        compiler_params=_params("arbitrary"),
        name="swa_decode",
    )(q2, z2, z2, kc, vc, cos, sin, perm, sink_row)


def _merge_value(x_ref, oa_ref, ob_ref, ga_ref, gb_ref, wa_ref, wb_ref, wo_ref):
    ta = _dot(oa_ref[...], wa_ref[...])
    tb = _dot(ob_ref[...], wb_ref[...])
    mg = _sigmoid(ga_ref[...].astype(F32)) * ta + _sigmoid(gb_ref[...].astype(F32)) * tb
    return x_ref[...] + _dot(mg.astype(wo_ref.dtype), wo_ref[...])


def _merge_body(x_ref, oa_ref, ob_ref, ga_ref, gb_ref, wa_ref, wb_ref, wo_ref, nw_ref, x1_ref, hn_ref):
    x1 = _merge_value(x_ref, oa_ref, ob_ref, ga_ref, gb_ref, wa_ref, wb_ref, wo_ref)
    x1_ref[...] = x1
    hn_ref[...] = _rms(x1, nw_ref[...]).astype(hn_ref.dtype)


def _resident(shape):
    return pl.BlockSpec(shape, lambda i: (0,) * len(shape), pipeline_mode=pl.Buffered(1))


def _merge_in_specs(tm, z2, wa, wb, wo):
    z_off = z2.shape[1] - (Z_WIDTH - N_HG_COLS)
    return [pl.BlockSpec((tm, D_MODEL), lambda i: (i, 0)),
            pl.BlockSpec((tm, HG_W), lambda i: (i, 0)),
            pl.BlockSpec((tm, SWQ_W), lambda i: (i, 0)),
            pl.BlockSpec((tm, D_MODEL), lambda i: (i, (z_off + R_GA) // D_MODEL)),
            pl.BlockSpec((tm, D_MODEL), lambda i: (i, (z_off + R_GB) // D_MODEL)),
            _resident(wa.shape), _resident(wb.shape), _resident(wo.shape), _resident((1, D_MODEL))]


def _merge(x2d, oa, ob, z2, wa, wb, wo, nw, tm):
    m = x2d.shape[0]
    return pl.pallas_call(
        _merge_body,
        grid=(m // tm,),
        in_specs=_merge_in_specs(tm, z2, wa, wb, wo),
        out_specs=[pl.BlockSpec((tm, D_MODEL), lambda i: (i, 0)),
                   pl.BlockSpec((tm, D_MODEL), lambda i: (i, 0))],
        out_shape=[jax.ShapeDtypeStruct((m, D_MODEL), F32),
                   jax.ShapeDtypeStruct((m, D_MODEL), wo.dtype)],
        compiler_params=_params("arbitrary"),
        name="merge",
    )(x2d, oa, ob, z2, z2, wa, wb, wo, nw)


FFN_CHUNKS = (768, 768, 768, 512)


def _merge_ffn_body(x_ref, oa_ref, ob_ref, ga_ref, gb_ref, wa_ref, wb_ref, wo_ref, nw_ref,
                    wg_ref, wu_ref, wd_ref, o_ref, acc_ref):
    x1 = _merge_value(x_ref, oa_ref, ob_ref, ga_ref, gb_ref, wa_ref, wb_ref, wo_ref)
    acc_ref[...] = x1
    hn = _rms(x1, nw_ref[...]).astype(wg_ref.dtype)
    off = 0
    for width in FFN_CHUNKS:
        g = _dot(hn, wg_ref[:, off:off + width])
        u = _dot(hn, wu_ref[:, off:off + width])
        acc_ref[...] += _dot((g * _sigmoid(g) * u).astype(wd_ref.dtype), wd_ref[off:off + width, :])
        off += width
    o_ref[...] = acc_ref[...]


def _merge_ffn(x2d, oa, ob, z2, wa, wb, wo, nw, wg, wu, wd, tm):
    m = x2d.shape[0]
    return pl.pallas_call(
        _merge_ffn_body,
        grid=(m // tm,),
        in_specs=(_merge_in_specs(tm, z2, wa, wb, wo)
                  + [_resident(wg.shape), _resident(wu.shape), _resident(wd.shape)]),
        out_specs=pl.BlockSpec((tm, D_MODEL), lambda i: (i, 0)),
        out_shape=jax.ShapeDtypeStruct((m, D_MODEL), F32),
        scratch_shapes=[pltpu.VMEM((tm, D_MODEL), F32)],
        compiler_params=_params("arbitrary"),
        name="merge_ffn",
    )(x2d, oa, ob, z2, z2, wa, wb, wo, nw, wg, wu, wd)


def _ffn_stream_body(hn_ref, x1_ref, wg_ref, wu_ref, wd_ref, o_ref):
    @pl.when(pl.program_id(0) == 0)
    def _():
        o_ref[...] = x1_ref[...]

    hn = hn_ref[...]
    g = _dot(hn, wg_ref[...])
    u = _dot(hn, wu_ref[...])
    o_ref[...] += _dot((g * _sigmoid(g) * u).astype(wd_ref.dtype), wd_ref[...])


def _ffn_stream(hn, x1, wg, wu, wd, tf):
    m = hn.shape[0]
    full = pl.BlockSpec((m, D_MODEL), lambda f: (0, 0))
    return pl.pallas_call(
        _ffn_stream_body,
        grid=(wg.shape[1] // tf,),
        in_specs=[full, full,
                  pl.BlockSpec((D_MODEL, tf), lambda f: (0, f)),
                  pl.BlockSpec((D_MODEL, tf), lambda f: (0, f)),
                  pl.BlockSpec((tf, D_MODEL), lambda f: (f, 0))],
        out_specs=full,
        out_shape=jax.ShapeDtypeStruct((m, D_MODEL), F32),
        compiler_params=_params("arbitrary"),
        name="ffn_stream",
    )(hn, x1, wg, wu, wd)


RT_E1, RT_E2, RT_G1, RT_G2, RT_R1, RT_R2 = range(6)
SEG_ALIGN = 16
EXPERT_CHUNKS = (1024, 1024, 1024, 512)


def _router_body(hn_ref, wr_ref, route_ref, route_t_ref, cnt_ref):
    T = hn_ref.shape[0]
    logits = _dot(hn_ref[...], wr_ref[...])
    lane = lax.broadcasted_iota(jnp.int32, logits.shape, 1).astype(F32)
    lg = jnp.where(lane < N_EXPERTS, logits, -jnp.inf)
    m1 = jnp.max(lg, axis=-1, keepdims=True)
    i1 = jnp.min(jnp.where(lg == m1, lane, float(LANES)), axis=-1, keepdims=True)
    lg2 = jnp.where(lane == i1, -jnp.inf, lg)
    m2 = jnp.max(lg2, axis=-1, keepdims=True)
    i2 = jnp.min(jnp.where(lg2 == m2, lane, float(LANES)), axis=-1, keepdims=True)
    e2 = jnp.exp(m2 - m1)
    sel = jnp.where((lane == i1) | (lane == i2), 1.0, 0.0)
    r = lax.broadcasted_iota(jnp.int32, (T, T), 0)
    c = lax.broadcasted_iota(jnp.int32, (T, T), 1)
    before = jnp.where(r > c, 1.0, 0.0).astype(BF16)
    rank = _dot(before, sel.astype(BF16))
    r1 = jnp.sum(jnp.where(lane == i1, rank, 0.0), axis=-1, keepdims=True)
    r2 = jnp.sum(jnp.where(lane == i2, rank, 0.0), axis=-1, keepdims=True)
    fields = {RT_E1: i1, RT_E2: i2, RT_G1: 1.0 / (1.0 + e2), RT_G2: e2 / (1.0 + e2), RT_R1: r1, RT_R2: r2}
    route = jnp.zeros_like(logits)
    for idx, val in fields.items():
        route = jnp.where(lane == float(idx), val, route)
    route_ref[...] = route
    route_t_ref[...] = route.T
    cnt_ref[...] = jnp.broadcast_to(jnp.sum(sel, axis=0, keepdims=True), cnt_ref.shape)


def _router(hn, wr, T):
    m = hn.shape[0]
    nt = m // T
    return pl.pallas_call(
        _router_body,
        grid=(nt,),
        in_specs=[pl.BlockSpec((T, D_MODEL), lambda i: (i, 0)),
                  pl.BlockSpec((D_MODEL, LANES), lambda i: (0, 0))],
        out_specs=[pl.BlockSpec((T, LANES), lambda i: (i, 0)),
                   pl.BlockSpec((LANES, T), lambda i: (0, i)),
                   pl.BlockSpec((None, SUBLANES, LANES), lambda i: (i, 0, 0))],
        out_shape=[jax.ShapeDtypeStruct((m, LANES), F32),
                   jax.ShapeDtypeStruct((LANES, m), F32),
                   jax.ShapeDtypeStruct((nt, SUBLANES, LANES), F32)],
        compiler_params=_params("arbitrary"),
        name="router",
    )(hn, wr)


def _segment_copies(seg_ref, so_ref, off_ref, step, make, fn, max_rows):
    sizes = []
    s = max_rows
    while s >= SEG_ALIGN:
        sizes.append(s)
        s //= 2
    for e in range(N_EXPERTS):
        n = seg_ref[step * N_EXPERTS + e]
        src = so_ref[step * N_EXPERTS + e]
        dst = off_ref[step * N_EXPERTS + e]
        for s in sizes:
            @pl.when((n & s) != 0)
            def _(s=s, n=n, src=src, dst=dst):
                done = n & (-2 * s)
                fn(make(pl.multiple_of(src + done, SEG_ALIGN), pl.multiple_of(dst + done, SEG_ALIGN), s))


def _stage_offset(so_ref, step, expert):
    out = jnp.zeros_like(expert)
    for e in range(N_EXPERTS):
        out = jnp.where(expert == float(e), so_ref[step * N_EXPERTS + e].astype(F32), out)
    return out


def _dispatch_body(seg_ref, so_ref, off_ref, hn_ref, rt_ref, init_ref, xs_ref, stage, sem):
    del init_ref
    t = pl.program_id(0)
    nt = pl.num_programs(0)
    T = hn_ref.shape[0]
    S = stage.shape[1]
    slot = t % 2

    def copies(step, slot, fn):
        def make(src, dst, rows):
            return pltpu.make_async_copy(stage.at[slot, pl.ds(src, rows)], xs_ref.at[pl.ds(dst, rows)], sem.at[slot])
        _segment_copies(seg_ref, so_ref, off_ref, step, make, fn, T)

    @pl.when(t >= 2)
    def _():
        copies(t - 2, slot, lambda c: c.wait())

    rt = rt_ref[...]
    tgt1 = _stage_offset(so_ref, t, rt[RT_E1:RT_E1 + 1, :]) + rt[RT_R1:RT_R1 + 1, :]
    tgt2 = _stage_offset(so_ref, t, rt[RT_E2:RT_E2 + 1, :]) + rt[RT_R2:RT_R2 + 1, :]
    rowi = lax.broadcasted_iota(jnp.int32, (S, T), 0).astype(F32)
    sel = jnp.where((rowi == tgt1) | (rowi == tgt2), 1.0, 0.0).astype(BF16)
    stage[slot] = _dot(sel, hn_ref[...]).astype(BF16)
    copies(t, slot, lambda c: c.start())

    @pl.when(t == nt - 1)
    def _():
        copies(t, slot, lambda c: c.wait())

        @pl.when(t >= 1)
        def _():
            copies(t - 1, 1 - slot, lambda c: c.wait())


def _dispatch(hn, route_t, plan, T, S, rows_total):
    m = hn.shape[0]
    grid_spec = pltpu.PrefetchScalarGridSpec(
        num_scalar_prefetch=3,
        grid=(m // T,),
        in_specs=[pl.BlockSpec((T, D_MODEL), lambda t, *_: (t, 0)),
                  pl.BlockSpec((LANES, T), lambda t, *_: (0, t)),
                  pl.BlockSpec(memory_space=pl.ANY)],
        out_specs=pl.BlockSpec(memory_space=pl.ANY),
        scratch_shapes=[pltpu.VMEM((2, S, D_MODEL), BF16), pltpu.SemaphoreType.DMA((2,))],
    )
    return pl.pallas_call(
        _dispatch_body,
        grid_spec=grid_spec,
        out_shape=jax.ShapeDtypeStruct((rows_total, D_MODEL), BF16),
        input_output_aliases={5: 0},
        compiler_params=_params("arbitrary"),
        name="moe_dispatch",
    )(plan["seg"], plan["so"], plan["off"], hn, route_t, jnp.zeros((rows_total, D_MODEL), BF16))


def _experts_body(owner_ref, used_ref, x_ref, wg_ref, wu_ref, wd_ref, y_ref, acc_ref):
    del owner_ref

    @pl.when(pl.program_id(0) < used_ref[0])
    def _():
        x = x_ref[...]
        off = 0
        for i, width in enumerate(EXPERT_CHUNKS):
            g = _dot(x, wg_ref[:, off:off + width])
            u = _dot(x, wu_ref[:, off:off + width])
            y = _dot((g * _sigmoid(g) * u).astype(BF16), wd_ref[off:off + width, :])
            if i == 0:
                acc_ref[...] = y
            else:
                acc_ref[...] += y
            off += width
        y_ref[...] = acc_ref[...].astype(y_ref.dtype)

    @pl.when(pl.program_id(0) >= used_ref[0])
    def _():
        y_ref[...] = jnp.zeros_like(y_ref)


def _experts(xs, plan, wg, wu, wd, bm):
    nblk = xs.shape[0] // bm
    buffers = 1 if bm >= 256 else 2

    def wspec(shape):
        return pl.BlockSpec((None,) + shape, lambda b, owner, used: (owner[b], 0, 0),
                            pipeline_mode=pl.Buffered(buffers))

    grid_spec = pltpu.PrefetchScalarGridSpec(
        num_scalar_prefetch=2,
        grid=(nblk,),
        in_specs=[pl.BlockSpec((bm, D_MODEL), lambda b, owner, used: (jnp.minimum(b, used[0] - 1), 0)),
                  wspec((D_MODEL, D_FF_EXPERT)), wspec((D_MODEL, D_FF_EXPERT)), wspec((D_FF_EXPERT, D_MODEL))],
        out_specs=pl.BlockSpec((bm, D_MODEL), lambda b, owner, used: (b, 0)),
        scratch_shapes=[pltpu.VMEM((bm, D_MODEL), F32)],
    )
    return pl.pallas_call(
        _experts_body,
        grid_spec=grid_spec,
        out_shape=jax.ShapeDtypeStruct(xs.shape, BF16),
        compiler_params=_params("arbitrary"),
        name="moe_experts",
    )(plan["owner"], plan["used"], xs, wg, wu, wd)


def _combine_body(seg_ref, so_ref, off_ref, route_ref, x1_ref, nw_ref, ys_ref, o_ref, ybuf, sem, *, final_norm):
    t = pl.program_id(0)
    nt = pl.num_programs(0)
    T = route_ref.shape[0]
    S = ybuf.shape[1]
    slot = t % 2

    def copies(step, slot, fn):
        def make(stage_row, grouped_row, rows):
            return pltpu.make_async_copy(ys_ref.at[pl.ds(grouped_row, rows)], ybuf.at[slot, pl.ds(stage_row, rows)],
                                         sem.at[slot])
        _segment_copies(seg_ref, so_ref, off_ref, step, make, fn, T)

    @pl.when(t == 0)
    def _():
        ybuf[...] = jnp.zeros_like(ybuf)
        copies(0, 0, lambda c: c.start())

    @pl.when(t + 1 < nt)
    def _():
        copies(t + 1, 1 - slot, lambda c: c.start())

    copies(t, slot, lambda c: c.wait())

    route = route_ref[...]
    tgt1 = _stage_offset(so_ref, t, route[:, RT_E1:RT_E1 + 1]) + route[:, RT_R1:RT_R1 + 1]
    tgt2 = _stage_offset(so_ref, t, route[:, RT_E2:RT_E2 + 1]) + route[:, RT_R2:RT_R2 + 1]
    coli = lax.broadcasted_iota(jnp.int32, (T, S), 1).astype(F32)
    sel = (jnp.where(coli == tgt1, route[:, RT_G1:RT_G1 + 1], 0.0)
           + jnp.where(coli == tgt2, route[:, RT_G2:RT_G2 + 1], 0.0))
    x2 = x1_ref[...] + _dot(sel.astype(BF16), ybuf[slot])
    o_ref[...] = _rms(x2, nw_ref[...]) if final_norm else x2


def _combine(ys, route, x1, nw, plan, T, S, final_norm):
    m = x1.shape[0]
    grid_spec = pltpu.PrefetchScalarGridSpec(
        num_scalar_prefetch=3,
        grid=(m // T,),
        in_specs=[pl.BlockSpec((T, LANES), lambda t, *_: (t, 0)),
                  pl.BlockSpec((T, D_MODEL), lambda t, *_: (t, 0)),
                  pl.BlockSpec((1, D_MODEL), lambda t, *_: (0, 0)),
                  pl.BlockSpec(memory_space=pl.ANY)],
        out_specs=pl.BlockSpec((T, D_MODEL), lambda t, *_: (t, 0)),
        scratch_shapes=[pltpu.VMEM((2, S, D_MODEL), BF16), pltpu.SemaphoreType.DMA((2,))],
    )
    return pl.pallas_call(
        functools.partial(_combine_body, final_norm=final_norm),
        grid_spec=grid_spec,
        out_shape=jax.ShapeDtypeStruct((m, D_MODEL), F32),
        compiler_params=_params("arbitrary"),
        name="moe_combine",
    )(plan["seg"], plan["so"], plan["off"], route, x1, nw, ys)


def _moe_plan(cnt, bm, nblk):
    seg = (cnt + (SEG_ALIGN - 1)) // SEG_ALIGN * SEG_ALIGN
    so = jnp.cumsum(seg, axis=1) - seg
    blocks = (jnp.sum(seg, axis=0) + (bm - 1)) // bm
    blk_end = jnp.cumsum(blocks)
    off = (blk_end - blocks)[None, :] * bm + jnp.cumsum(seg, axis=0) - seg
    owner = jnp.minimum(jnp.sum(jnp.arange(nblk)[:, None] >= blk_end[None, :], axis=1), N_EXPERTS - 1)
    as_i32 = lambda a: a.astype(jnp.int32).reshape(-1)
    return {"seg": as_i32(seg), "so": as_i32(so), "off": as_i32(off), "owner": as_i32(owner),
            "used": as_i32(blk_end[-1:])}


def _moe_ffn(hn, x1, nw, wr, wg, wu, wd, T, bm, final_norm):
    m = hn.shape[0]
    tiles = m // T
    pad_rows = tiles * N_EXPERTS * (SEG_ALIGN - 1)
    stage_rows = -(-(2 * T + N_EXPERTS * (SEG_ALIGN - 1)) // LANES) * LANES
    nblk = -(-(2 * m + pad_rows) // bm) + N_EXPERTS
    route, route_t, cnt = _router(hn, wr, T)
    plan = _moe_plan(cnt[:, 0, :N_EXPERTS].astype(jnp.int32), bm, nblk)
    xs = _dispatch(hn.astype(BF16), route_t, plan, T, stage_rows, nblk * bm)
    ys = _experts(xs, plan, wg, wu, wd, bm)
    return _combine(ys, route, x1, nw, plan, T, stage_rows, final_norm)


def _norm_body(x_ref, nw_ref, o_ref):
    o_ref[...] = _rms(x_ref[...], nw_ref[...])


def _norm(x, nw, tm):
    m = x.shape[0]
    spec = pl.BlockSpec((tm, D_MODEL), lambda i: (i, 0))
    return pl.pallas_call(
        _norm_body, grid=(m // tm,),
        in_specs=[spec, pl.BlockSpec((1, D_MODEL), lambda i: (0, 0))],
        out_specs=spec, out_shape=jax.ShapeDtypeStruct((m, D_MODEL), F32),
        compiler_params=_params("arbitrary"), name="norm",
    )(x, nw)


def _layout_w_in(w):
    o = 0
    hg = w[:, o:o + 4 * HG_W]; o += 4 * HG_W
    sq = w[:, o:o + SWQ_W]; o += SWQ_W
    sk = w[:, o:o + KV_W]; o += KV_W
    sv = w[:, o:o + KV_W]; o += KV_W
    ga = w[:, o:o + D_MODEL]; o += D_MODEL
    gb = w[:, o:o + D_MODEL]
    pairs = [sq[:, h * SW_HD:(h + 1) * SW_HD] for j in range(SW_PAIRS) for h in (j, j + SW_GROUP)]
    return jnp.concatenate([hg, ga, gb] + pairs + [sk, sv], axis=1)


def _layout_w_branch_b(w):
    return jnp.concatenate([w[h * SW_HD:(h + 1) * SW_HD] for j in range(SW_PAIRS) for h in (j, j + SW_GROUP)], axis=0)


def _rope_tables(pos):
    half = SW_HD // 2
    inv = ROPE_THETA ** (-jnp.arange(half, dtype=F32) / half)
    ang = pos.astype(F32)[:, None] * inv[None, :]
    cos = jnp.cos(ang)
    sin = jnp.sin(ang)
    reps = LANES // SW_HD
    return jnp.tile(cos, (1, 2 * reps)), jnp.tile(jnp.concatenate([-sin, sin], axis=1), (1, reps))


def _rotate_half_matrix():
    j = jnp.arange(LANES)
    src = jnp.where((j % SW_HD) < SW_HD // 2, j + SW_HD // 2, j - SW_HD // 2)
    return (jnp.arange(LANES)[:, None] == src[None, :]).astype(F32)


def _tile(m, pref):
    return pref if m % pref == 0 else m


def kernel(x_prompt, x_sample, state_hgrn, cache_swa_k, cache_swa_v, norm_mix, w_in, hg_lb_logits, hg_norm,
           swa_sinks, w_branch_a, w_branch_b, w_out, norm_ffn, w_gate_dense, w_up_dense, w_down_dense,
           w_router, w_gate_moe, w_up_moe, w_down_moe, norm_final):
    depth = w_in.shape[0]
    bsz, seq, _ = x_prompt.shape
    nsamp = x_sample.shape[0]
    mp = bsz * seq
    wb = cache_swa_k.shape[2]

    cos_p, sin_p = _rope_tables(jnp.arange(seq))
    cos_s, sin_s = _rope_tables(PAST_LEN + jnp.arange(1))
    perm32 = _rotate_half_matrix()
    perm = perm32.astype(BF16)
    lbl = hg_lb_logits.astype(F32)
    state32 = state_hgrn.astype(F32)
    kcache = cache_swa_k.reshape(depth, nsamp, wb, KV_W).astype(F32)
    vcache = cache_swa_v.reshape(depth, nsamp, wb, KV_W).astype(F32)

    xp = x_prompt.reshape(mp, D_MODEL)
    xs = x_sample.reshape(nsamp, D_MODEL)
    outs = {k: [] for k in ("sp", "kp", "vp", "ss", "ks", "vs")}
    nfin = norm_final.reshape(1, D_MODEL)
    normed = False

    for l in range(depth):
        w_in32 = _layout_w_in(w_in[l])
        wa32 = w_branch_a[l]
        wbb32 = _layout_w_branch_b(w_branch_b[l])
        wo32 = w_out[l]
        w_in_l = w_in32.astype(BF16)
        wa = wa32.astype(BF16)
        wbb = wbb32.astype(BF16)
        wo = wo32.astype(BF16)
        nmix = norm_mix[l].reshape(1, D_MODEL)
        nffn = norm_ffn[l].reshape(1, D_MODEL)
        hgn = hg_norm[l].reshape(1, HG_D)
        sinks = swa_sinks[l].astype(F32)

        zp = _proj(xp, nmix, w_in_l, _tile(mp, 512))
        zp3 = zp.reshape(bsz, seq, Z_WIDTH)
        moe_next = (w_gate_moe, w_up_moe, w_down_moe) if (l + 1 < depth and l % 2 == 0) else ()
        oa_p, s_p, cast = _hgrn_prompt(zp3, lbl, hgn, l, tuple(a[(l + 1) // 2].reshape(-1, a.shape[-1])
                                                               for a in moe_next))
        if moe_next:
            moe_bf16 = tuple(c.reshape(a.shape[1:]) for c, a in zip(cast, moe_next))
        ob_p, k_p, v_p = _swa_prompt(zp3, sinks, cos_p, sin_p, perm)
        oa_p = oa_p.reshape(mp, HG_W)
        ob_p = ob_p.reshape(mp, SWQ_W)

        zs = _proj(xs, nmix, w_in32, nsamp)
        oa_s, s_s = _hgrn_decode(zs, state32, lbl, hgn, l)
        q2 = zs[:, N_HG_COLS + R_SWQ:N_HG_COLS + R_SWQ + SWQ_W].reshape(nsamp * SW_PAIRS, LANES)
        sink_row = jnp.pad(sinks, (0, LANES - SW_Q_HEADS)).reshape(1, LANES)
        ob_s, k_s, v_s = _swa_decode(q2, zs, kcache, vcache, l, cos_s, sin_s, perm32, sink_row)
        x1s, hns = _merge(xs, oa_s, ob_s.reshape(nsamp, SWQ_W), zs, wa32, wbb32, wo32, nffn, nsamp)

        j = l // 2
        if l % 2 == 0:
            xp = _merge_ffn(xp, oa_p, ob_p, zp, wa, wbb, wo, nffn, w_gate_dense[j].astype(BF16),
                            w_up_dense[j].astype(BF16), w_down_dense[j].astype(BF16), _tile(mp, 512))
            xs = _ffn_stream(hns, x1s, w_gate_dense[j], w_up_dense[j], w_down_dense[j], 256)
        else:
            x1p, hnp = _merge(xp, oa_p, ob_p, zp, wa, wbb, wo, nffn, _tile(mp, 512))
            wr32 = jnp.pad(w_router[j], ((0, 0), (0, LANES - N_EXPERTS)))
            wg, wu, wd = moe_bf16
            last = l == depth - 1
            xp = _moe_ffn(hnp, x1p, nfin, wr32.astype(BF16), wg, wu, wd, _tile(mp, 512), 512, last)
            xs = _moe_ffn(hns, x1s, nfin, wr32, wg, wu, wd, nsamp, 128, last)
            normed = last

        outs["sp"].append(s_p)
        outs["kp"].append(k_p.reshape(bsz, WINDOW, SW_KV_HEADS, SW_HD))
        outs["vp"].append(v_p.reshape(bsz, WINDOW, SW_KV_HEADS, SW_HD))
        outs["ss"].append(s_s.astype(state_hgrn.dtype))
        outs["ks"].append(k_s.reshape(nsamp, wb, SW_KV_HEADS, SW_HD))
        outs["vs"].append(v_s.reshape(nsamp, wb, SW_KV_HEADS, SW_HD))

    yp, ys = (xp, xs) if normed else (_norm(xp, nfin, _tile(mp, 1024)), _norm(xs, nfin, nsamp))

    return (yp.reshape(x_prompt.shape), ys.reshape(x_sample.shape),
            jnp.stack(outs["sp"]), jnp.stack(outs["kp"]), jnp.stack(outs["vp"]),
            jnp.stack(outs["ss"]), jnp.stack(outs["ks"]), jnp.stack(outs["vs"]))
```

```python
import functools

import jax
import jax.numpy as jnp
from jax import lax
from jax.experimental import pallas as pl
from jax.experimental.pallas import tpu as pltpu

F32 = jnp.float32
BF16 = jnp.bfloat16

D_MODEL = 1024
PAST_LEN = 16384
HG_HEADS = 4
HG_D = 128
HG_W = HG_HEADS * HG_D
SW_Q_HEADS = 8
SW_KV_HEADS = 2
SW_HD = 64
SW_GROUP = SW_Q_HEADS // SW_KV_HEADS
WINDOW = 128
ROPE_THETA = 10000.0
D_FF_DENSE = 2816
N_EXPERTS = 8
D_FF_EXPERT = 3584
RMS_EPS = 1e-6

LANES = 128
SUBLANES = 8
VMEM_LIMIT = 56 * 1024 * 1024

SWQ_W = SW_Q_HEADS * SW_HD
SW_PAIRS = SWQ_W // LANES
KV_W = SW_KV_HEADS * SW_HD
Z_WIDTH = 4 * HG_W + 2 * D_MODEL + SWQ_W + 2 * KV_W
ZB_Q, ZB_F, ZB_I, ZB_G = 0, 1, 2, 3
N_HG_COLS = 4 * HG_W
R_GA, R_GB, R_SWQ, R_K, R_V = 0, D_MODEL, 2 * D_MODEL, 2 * D_MODEL + SWQ_W, 2 * D_MODEL + SWQ_W + KV_W
PROJ_CHUNKS = (1024, 1024, 1024, 1024, 768)

HG_CHUNK = 128
SW_UNROLL = 4
SEQ_BLOCK = 512
DEC_TILE = 8


def _dot_dims(a, b, dims):
    precision = lax.Precision.HIGHEST if a.dtype == F32 else None
    return lax.dot_general(a, b, (dims, ((), ())), precision=precision, preferred_element_type=F32)


def _dot(a, b):
    return _dot_dims(a, b, ((1,), (0,)))


def _dot_nt(a, b):
    return _dot_dims(a, b, ((1,), (1,)))


def _dot_tn(a, b):
    return _dot_dims(a, b, ((0,), (0,)))


def _sigmoid(x):
    return 1.0 / (1.0 + jnp.exp(-x))


def _rms(x, w):
    ms = jnp.mean(x * x, axis=-1, keepdims=True)
    return x * lax.rsqrt(ms + RMS_EPS) * w


def _params(*sem):
    return pltpu.CompilerParams(dimension_semantics=sem, vmem_limit_bytes=VMEM_LIMIT)


def _proj_body(x_ref, nw_ref, w_ref, z_ref):
    h = _rms(x_ref[...], nw_ref[...]).astype(w_ref.dtype)
    off = 0
    for width in PROJ_CHUNKS:
        z_ref[:, off:off + width] = _dot(h, w_ref[:, off:off + width]).astype(z_ref.dtype)
        off += width


def _proj(x2d, nw, w, tm):
    m = x2d.shape[0]
    n = w.shape[1]
    return pl.pallas_call(
        _proj_body,
        grid=(m // tm,),
        in_specs=[
            pl.BlockSpec((tm, D_MODEL), lambda i: (i, 0)),
            pl.BlockSpec((1, D_MODEL), lambda i: (0, 0)),
            pl.BlockSpec((D_MODEL, n), lambda i: (0, 0), pipeline_mode=pl.Buffered(1)),
        ],
        out_specs=pl.BlockSpec((tm, n), lambda i: (i, 0)),
        out_shape=jax.ShapeDtypeStruct((m, n), w.dtype),
        compiler_params=_params("arbitrary"),
        name="proj",
    )(x2d, nw, w)


def _lower_bound(lbl, layer):
    mx = jnp.max(lbl, axis=0, keepdims=True)
    e = jnp.exp(lbl - mx)
    sm = e / jnp.sum(e, axis=0, keepdims=True)
    cum = sm[0:1, :]
    for i in range(1, layer + 1):
        cum = cum + sm[i:i + 1, :]
    return cum - sm[0:1, :]


def _split3(x):
    hi = x.astype(BF16)
    r = x - hi.astype(F32)
    mid = r.astype(BF16)
    lo = (r - mid.astype(F32)).astype(BF16)
    return hi, mid, lo


def _hgrn_tile(zh_ref, lb, nw, o_ref, st_scr, b_all_scr):
    C = HG_CHUNK
    col_q, col_f, col_i, col_g = (slice(b * HG_W, (b + 1) * HG_W) for b in (ZB_Q, ZB_F, ZB_I, ZB_G))
    row = lax.broadcasted_iota(jnp.int32, (C, C), 0)
    col = lax.broadcasted_iota(jnp.int32, (C, C), 1)
    xr = row ^ col
    tri = jnp.where(row >= col, 1.0, 0.0).astype(BF16)
    sub4 = (lax.broadcasted_iota(jnp.int32, (SUBLANES, HG_D), 0) & 4) == 0
    levels = (1, 2, 4, 8, 16, 32, 64)
    pair_level = {m: (xr >= m) & (xr < 2 * m) for m in levels}
    upper_half = {m: (row & m) != 0 for m in levels if m < SUBLANES}

    def chunk(rows, b_scr):
        hf = zh_ref[rows, col_f].astype(F32)
        fg_all = lb + (1.0 - lb) * _sigmoid(hf)
        g_all = jnp.log2(fg_all)
        k_all = 1.0 - fg_all
        g1, g2, g3 = _split3(g_all)
        b_all = _dot(tri, g1) + _dot(tri, g2) + _dot(tri, g3)
        b_scr[...] = b_all
        hq = zh_ref[rows, col_q].astype(F32)
        q_all = hq * _sigmoid(hq) * (HG_D ** -0.5)
        gate = zh_ref[rows, col_g].astype(F32)
        gate_all = gate * _sigmoid(gate)
        for h in range(HG_HEADS):
            sl = slice(h * HG_D, (h + 1) * HG_D)
            head(h, rows, b_scr, b_all[:, sl], q_all[:, sl], k_all[:, sl], fg_all[:, sl], gate_all[:, sl])

    def head(h, rows, b_scr, b, q, k, f, gate):
        sl = slice(h * HG_D, (h + 1) * HG_D)
        v = zh_ref[rows, col_i.start + h * HG_D:col_i.start + (h + 1) * HG_D]

        acc = _dot_nt(q.astype(BF16), k.astype(BF16))
        for m in levels:
            if m == 1:
                w = jnp.where(upper_half[m], q * f, k)
            elif m < SUBLANES:
                if m == 2:
                    pieces = []
                    for j in range(C // SUBLANES):
                        lo = jnp.broadcast_to(b_scr[SUBLANES * j + 1:SUBLANES * j + 2, sl], (SUBLANES, HG_D))
                        hi = jnp.broadcast_to(b_scr[SUBLANES * j + 5:SUBLANES * j + 6, sl], (SUBLANES, HG_D))
                        pieces.append(jnp.where(sub4, lo, hi))
                else:
                    pieces = [jnp.broadcast_to(b_scr[i * 2 * m + m - 1:i * 2 * m + m, sl], (2 * m, HG_D))
                              for i in range(C // (2 * m))]
                d = b - jnp.concatenate(pieces, axis=0)
                w = jnp.where(upper_half[m], q, k) * jnp.exp2(jnp.where(upper_half[m], d, -d))
            else:
                expo, qk = [], []
                for i in range(C // (2 * m)):
                    lo, mid, hi = i * 2 * m, i * 2 * m + m, (i + 1) * 2 * m
                    bref = b_scr[mid - 1:mid, sl]
                    expo += [bref - b[lo:mid], b[mid:hi] - bref]
                    qk += [k[lo:mid], q[mid:hi]]
                w = jnp.concatenate(qk, axis=0) * jnp.exp2(jnp.concatenate(expo, axis=0))
            wb = w.astype(BF16)
            acc = jnp.where(pair_level[m], _dot_nt(wb, wb), acc)
        a = jnp.where(row >= col, acc, 0.0)

        st = st_scr[h]
        o = _dot(a.astype(BF16), v) + _dot_nt((q * jnp.exp2(b)).astype(BF16), st.astype(BF16))
        b_last = b[C - 1:C, :]
        kd = k * jnp.exp2(b_last - b)
        st_scr[h] = jnp.exp2(b_last) * st + _dot_tn(v, kd.astype(BF16))

        o_ref[rows, sl] = (_rms(o, nw) * gate).astype(o_ref.dtype)

    for u in range(zh_ref.shape[0] // C):
        chunk(slice(u * C, (u + 1) * C), b_all_scr.at[u])


def _hgrn_prompt_body(zh_ref, lbl_ref, nw_ref, *refs, layer, n_cast):
    cast_in = refs[:n_cast]
    o_ref, s_ref = refs[n_cast:n_cast + 2]
    cast_out = refs[n_cast + 2:2 * n_cast + 2]
    st_scr, b_all_scr = refs[2 * n_cast + 2:]
    c = pl.program_id(1)

    @pl.when(c == 0)
    def _():
        st_scr[...] = jnp.zeros_like(st_scr)

    for src, dst in zip(cast_in, cast_out):
        dst[...] = src[...].astype(dst.dtype)

    _hgrn_tile(zh_ref, _lower_bound(lbl_ref[...], layer), nw_ref[...], o_ref, st_scr, b_all_scr)

    @pl.when(c == pl.num_programs(1) - 1)
    def _():
        for h in range(HG_HEADS):
            s_ref[h] = st_scr[h].T


def _hgrn_prompt(z3, lbl, nw, layer, cast=()):
    bsz, seq, _ = z3.shape
    rows = _tile(seq, SEQ_BLOCK)
    steps_per_seq = seq // rows
    steps = bsz * steps_per_seq

    def slab(a):
        assert a.shape[0] % (steps * 2 * SUBLANES) == 0, a.shape
        return pl.BlockSpec((a.shape[0] // steps, a.shape[1]), lambda b, c: (b * steps_per_seq + c, 0))

    outs = pl.pallas_call(
        functools.partial(_hgrn_prompt_body, layer=layer, n_cast=len(cast)),
        grid=(bsz, steps_per_seq),
        in_specs=[pl.BlockSpec((None, rows, N_HG_COLS), lambda b, c: (b, c, 0)),
                  pl.BlockSpec(lbl.shape, lambda b, c: (0, 0)),
                  pl.BlockSpec((1, HG_D), lambda b, c: (0, 0))] + [slab(a) for a in cast],
        out_specs=[pl.BlockSpec((None, rows, HG_W), lambda b, c: (b, c, 0)),
                   pl.BlockSpec((None, HG_HEADS, HG_D, HG_D), lambda b, c: (b, 0, 0, 0))] + [slab(a) for a in cast],
        out_shape=[jax.ShapeDtypeStruct((bsz, seq, HG_W), BF16),
                   jax.ShapeDtypeStruct((bsz, HG_HEADS, HG_D, HG_D), F32)]
        + [jax.ShapeDtypeStruct(a.shape, BF16) for a in cast],
        scratch_shapes=[pltpu.VMEM((HG_HEADS, HG_D, HG_D), F32),
                        pltpu.VMEM((rows // HG_CHUNK, HG_CHUNK, HG_W), F32)],
        compiler_params=_params("arbitrary", "arbitrary"),
        name="hgrn_prompt",
    )(z3, lbl, nw, *cast)
    return outs[0], outs[1], outs[2:]


def _hgrn_decode_body(zq_ref, zf_ref, zi_ref, zg_ref, lbl_ref, nw_ref, s_ref, o_ref, sn_ref, o_scr, *, layer):
    lb = _lower_bound(lbl_ref[...], layer)
    fg = lb + (1.0 - lb) * _sigmoid(zf_ref[...].astype(F32))
    kk = 1.0 - fg
    hq = zq_ref[...].astype(F32)
    q = hq * _sigmoid(hq) * (HG_D ** -0.5)
    v = zi_ref[...].astype(F32)
    eye = (lax.broadcasted_iota(jnp.int32, (HG_D, HG_D), 0)
           == lax.broadcasted_iota(jnp.int32, (HG_D, HG_D), 1))

    def column(r):
        return jnp.sum(jnp.where(eye, jnp.broadcast_to(r, (HG_D, HG_D)), 0.0), axis=1, keepdims=True)

    for i in range(DEC_TILE):
        for h in range(HG_HEADS):
            sl = slice(h * HG_D, (h + 1) * HG_D)
            sn = column(fg[i:i + 1, sl]) * s_ref[i, h] + column(kk[i:i + 1, sl]) * v[i:i + 1, sl]
            sn_ref[i, h] = sn
            o_scr[i:i + 1, sl] = jnp.sum(column(q[i:i + 1, sl]) * sn, axis=0, keepdims=True)

    gate = zg_ref[...].astype(F32)
    gate = gate * _sigmoid(gate)
    nw = nw_ref[...]
    for h in range(HG_HEADS):
        sl = slice(h * HG_D, (h + 1) * HG_D)
        o_ref[:, sl] = (_rms(o_scr[:, sl], nw) * gate[:, sl]).astype(o_ref.dtype)


def _hgrn_decode(z2, state, lbl, nw, layer):
    n = z2.shape[0]
    T = DEC_TILE

    def zspec(blk):
        return pl.BlockSpec((T, HG_W), lambda i: (i, blk))

    sblock = (T, HG_HEADS, HG_D, HG_D)
    return pl.pallas_call(
        functools.partial(_hgrn_decode_body, layer=layer),
        grid=(n // T,),
        in_specs=[zspec(ZB_Q), zspec(ZB_F), zspec(ZB_I), zspec(ZB_G),
                  pl.BlockSpec(lbl.shape, lambda i: (0, 0)),
                  pl.BlockSpec((1, HG_D), lambda i: (0, 0)),
                  pl.BlockSpec((None,) + sblock, lambda i: (layer, i, 0, 0, 0))],
        out_specs=[pl.BlockSpec((T, HG_W), lambda i: (i, 0)), pl.BlockSpec(sblock, lambda i: (i, 0, 0, 0))],
        out_shape=[jax.ShapeDtypeStruct((n, HG_W), z2.dtype),
                   jax.ShapeDtypeStruct(state.shape[1:], F32)],
        scratch_shapes=[pltpu.VMEM((T, HG_W), F32)],
        compiler_params=_params("arbitrary"),
        name="hgrn_decode",
    )(z2, z2, z2, z2, lbl, nw, state)


def _rope(x, cos, sin, perm):
    return x.astype(F32) * cos + _dot(x, perm) * sin


def _swa_prompt_body(sinks_ref, zq_ref, zk_ref, zv_ref, cos_ref, sin_ref, perm_ref,
                     o_ref, kr_ref, vr_ref, kprev, vprev):
    n = pl.program_id(1)
    W = WINDOW

    @pl.when(n == 0)
    def _():
        kprev[...] = jnp.zeros_like(kprev)
        vprev[...] = jnp.zeros_like(vprev)

    perm = perm_ref[...]
    qi = lax.broadcasted_iota(jnp.int32, (W, 2 * W), 0)
    kj = lax.broadcasted_iota(jnp.int32, (W, 2 * W), 1)
    band = (kj >= qi) & (kj <= qi + W)
    low_half = lax.broadcasted_iota(jnp.int32, (W, LANES), 1) < SW_HD
    heads = range(SW_Q_HEADS)
    scale = SW_HD ** -0.5

    def blocks(i, carry):
        for u in range(SW_UNROLL):
            block(i * SW_UNROLL + u)
        return carry

    def block(i):
        rows = pl.ds(pl.multiple_of(i * W, W), W)
        cos = cos_ref[rows, :]
        sin = sin_ref[rows, :]
        k_rot = _rope(zk_ref[rows, :], cos, sin, perm)
        kr_ref[...] = k_rot
        vr_ref[...] = zv_ref[rows, :].astype(F32)
        k_cur = k_rot.astype(BF16)
        v_cur = zv_ref[rows, :]
        kk = jnp.concatenate([kprev[...], k_cur], axis=0)
        vv = jnp.concatenate([vprev[...], v_cur], axis=0)
        valid = band & (((n > 0) | (i > 0)) | (kj >= W))

        q_in = [zq_ref[rows, j * LANES:(j + 1) * LANES] for j in range(SW_PAIRS)]
        rot = _dot(jnp.concatenate(q_in, axis=0), perm)
        pair = [(q_in[j].astype(F32) * cos + rot[j * W:(j + 1) * W] * sin) * scale for j in range(SW_PAIRS)]
        q_rot = [jnp.where(low_half if h < SW_PAIRS else ~low_half, pair[h % SW_PAIRS], 0.0).astype(BF16)
                 for h in heads]
        s_all = _dot_nt(jnp.concatenate(q_rot, axis=0), kk)
        p_all, den_all = [], []
        for h in heads:
            s = jnp.where(valid, s_all[h * W:(h + 1) * W], -jnp.inf)
            sink = sinks_ref[h]
            mx = jnp.maximum(jnp.max(s, axis=-1, keepdims=True), sink)
            p = jnp.exp(s - mx)
            den_all.append(jnp.sum(p, axis=-1, keepdims=True) + jnp.exp(sink - mx))
            p_all.append(p.astype(BF16))
        o_all = _dot(jnp.concatenate(p_all, axis=0), vv)
        o_head = [o_all[h * W:(h + 1) * W] / den_all[h] for h in heads]
        for j in range(SW_PAIRS):
            o_pair = jnp.where(low_half, o_head[j], o_head[j + SW_PAIRS])
            o_ref[rows, j * LANES:(j + 1) * LANES] = o_pair.astype(o_ref.dtype)
        kprev[...] = k_cur
        vprev[...] = v_cur

    lax.fori_loop(0, zq_ref.shape[0] // (W * SW_UNROLL), blocks, 0)


def _swa_prompt(z3, sinks, cos, sin, perm):
    bsz, seq, _ = z3.shape
    W = WINDOW
    rows = _tile(seq, SEQ_BLOCK)
    grid_spec = pltpu.PrefetchScalarGridSpec(
        num_scalar_prefetch=1,
        grid=(bsz, seq // rows),
        in_specs=[pl.BlockSpec((None, rows, SWQ_W), lambda b, n, s: (b, n, (N_HG_COLS + R_SWQ) // SWQ_W)),
                  pl.BlockSpec((None, rows, KV_W), lambda b, n, s: (b, n, (N_HG_COLS + R_K) // KV_W)),
                  pl.BlockSpec((None, rows, KV_W), lambda b, n, s: (b, n, (N_HG_COLS + R_V) // KV_W)),
                  pl.BlockSpec((rows, LANES), lambda b, n, s: (n, 0)),
                  pl.BlockSpec((rows, LANES), lambda b, n, s: (n, 0)),
                  pl.BlockSpec((LANES, LANES), lambda b, n, s: (0, 0))],
        out_specs=[pl.BlockSpec((None, rows, SWQ_W), lambda b, n, s: (b, n, 0)),
                   pl.BlockSpec((None, W, KV_W), lambda b, n, s: (b, 0, 0)),
                   pl.BlockSpec((None, W, KV_W), lambda b, n, s: (b, 0, 0))],
        scratch_shapes=[pltpu.VMEM((W, KV_W), BF16), pltpu.VMEM((W, KV_W), BF16)],
    )
    return pl.pallas_call(
        _swa_prompt_body,
        grid_spec=grid_spec,
        out_shape=[jax.ShapeDtypeStruct((bsz, seq, SWQ_W), BF16),
                   jax.ShapeDtypeStruct((bsz, W, KV_W), F32),
                   jax.ShapeDtypeStruct((bsz, W, KV_W), F32)],
        compiler_params=_params("arbitrary", "arbitrary"),
        name="swa_prompt",
    )(sinks, z3, z3, z3, cos, sin, perm)


def _swa_decode_body(q_ref, zk_ref, zv_ref, kc_ref, vc_ref, cos_ref, sin_ref, perm_ref, sink_ref,
                     o_ref, kn_ref, vn_ref):
    cos = cos_ref[...]
    sin = sin_ref[...]
    perm = perm_ref[...]
    k_new = _rope(zk_ref[...], cos, sin, perm)
    v_new = zv_ref[...].astype(F32)
    sink = sink_ref[...]
    W = kc_ref.shape[1]
    for i in range(DEC_TILE):
        kn_ref[i, 0:W - 1, :] = kc_ref[i, 1:W, :]
        vn_ref[i, 0:W - 1, :] = vc_ref[i, 1:W, :]
        kn_ref[i, W - 1:W, :] = k_new[i:i + 1, :]
        vn_ref[i, W - 1:W, :] = v_new[i:i + 1, :]
    heads = range(SW_Q_HEADS)
    lane_row = lax.broadcasted_iota(jnp.int32, (1, LANES), 1)
    lane = lax.broadcasted_iota(jnp.int32, (W, LANES), 1)
    low_half = lane_row < SW_HD
    q_all = _rope(q_ref[...], cos, sin, perm) * (SW_HD ** -0.5)
    for i in range(DEC_TILE):
        keys = kc_ref[i]
        vals = vc_ref[i]
        s = jnp.zeros((W, LANES), F32)
        s_new = jnp.zeros((1, LANES), F32)
        q_heads = []
        for h in heads:
            r = i * SW_PAIRS + h % SW_PAIRS
            q_heads.append(jnp.where(low_half if h < SW_PAIRS else ~low_half, q_all[r:r + 1, :], 0.0))
        for h in heads:
            qh = q_heads[h]
            s = jnp.where(lane == h, jnp.sum(keys * qh, axis=-1, keepdims=True), s)
            s_new = jnp.where(lane_row == h, jnp.sum(k_new[i:i + 1, :] * qh, axis=-1, keepdims=True), s_new)
        mx = jnp.maximum(jnp.maximum(jnp.max(s, axis=0, keepdims=True), s_new), sink)
        p = jnp.exp(s - mx)
        p_new = jnp.exp(s_new - mx)
        den = jnp.sum(p, axis=0, keepdims=True) + p_new + jnp.exp(sink - mx)
        w_new = p_new / den
        wgt = p / den
        o_head = [jnp.sum(wgt[:, h:h + 1] * vals, axis=0, keepdims=True) + w_new[:, h:h + 1] * v_new[i:i + 1, :]
                  for h in heads]
        for j in range(SW_PAIRS):
            r = i * SW_PAIRS + j
            o_ref[r:r + 1, :] = jnp.where(low_half, o_head[j], o_head[j + SW_PAIRS]).astype(o_ref.dtype)


def _swa_decode(q2, z2, kc, vc, layer, cos, sin, perm, sink_row):
    n = z2.shape[0]
    T = DEC_TILE
    W = kc.shape[2]
    cache = pl.BlockSpec((None, T, W, KV_W), lambda i: (layer, i, 0, 0))
    return pl.pallas_call(
        _swa_decode_body,
        grid=(n // T,),
        in_specs=[pl.BlockSpec((T * SW_PAIRS, LANES), lambda i: (i, 0)),
                  pl.BlockSpec((T, KV_W), lambda i: (i, (N_HG_COLS + R_K) // KV_W)),
                  pl.BlockSpec((T, KV_W), lambda i: (i, (N_HG_COLS + R_V) // KV_W)),
                  cache, cache,
                  pl.BlockSpec((1, LANES), lambda i: (0, 0)),
                  pl.BlockSpec((1, LANES), lambda i: (0, 0)),
                  pl.BlockSpec((LANES, LANES), lambda i: (0, 0)),
                  pl.BlockSpec((1, LANES), lambda i: (0, 0))],
        out_specs=[pl.BlockSpec((T * SW_PAIRS, LANES), lambda i: (i, 0)),
                   pl.BlockSpec((T, W, KV_W), lambda i: (i, 0, 0)),
                   pl.BlockSpec((T, W, KV_W), lambda i: (i, 0, 0))],
        out_shape=[jax.ShapeDtypeStruct((n * SW_PAIRS, LANES), q2.dtype),
                   jax.ShapeDtypeStruct((n, W, KV_W), F32),
                   jax.ShapeDtypeStruct((n, W, KV_W), F32)],
        compiler_params=_params("arbitrary"),
        name="swa_decode",
    )(q2, z2, z2, kc, vc, cos, sin, perm, sink_row)


def _merge_value(x_ref, oa_ref, ob_ref, ga_ref, gb_ref, wa_ref, wb_ref, wo_ref):
    ta = _dot(oa_ref[...], wa_ref[...])
    tb = _dot(ob_ref[...], wb_ref[...])
    mg = _sigmoid(ga_ref[...].astype(F32)) * ta + _sigmoid(gb_ref[...].astype(F32)) * tb
    return x_ref[...] + _dot(mg.astype(wo_ref.dtype), wo_ref[...])


def _merge_body(x_ref, oa_ref, ob_ref, ga_ref, gb_ref, wa_ref, wb_ref, wo_ref, nw_ref, x1_ref, hn_ref):
    x1 = _merge_value(x_ref, oa_ref, ob_ref, ga_ref, gb_ref, wa_ref, wb_ref, wo_ref)
    x1_ref[...] = x1
    hn_ref[...] = _rms(x1, nw_ref[...]).astype(hn_ref.dtype)


def _resident(shape):
    return pl.BlockSpec(shape, lambda i: (0,) * len(shape), pipeline_mode=pl.Buffered(1))


def _merge_in_specs(tm, z2, wa, wb, wo):
    z_off = z2.shape[1] - (Z_WIDTH - N_HG_COLS)
    return [pl.BlockSpec((tm, D_MODEL), lambda i: (i, 0)),
            pl.BlockSpec((tm, HG_W), lambda i: (i, 0)),
            pl.BlockSpec((tm, SWQ_W), lambda i: (i, 0)),
            pl.BlockSpec((tm, D_MODEL), lambda i: (i, (z_off + R_GA) // D_MODEL)),
            pl.BlockSpec((tm, D_MODEL), lambda i: (i, (z_off + R_GB) // D_MODEL)),
            _resident(wa.shape), _resident(wb.shape), _resident(wo.shape), _resident((1, D_MODEL))]


def _merge(x2d, oa, ob, z2, wa, wb, wo, nw, tm):
    m = x2d.shape[0]
    return pl.pallas_call(
        _merge_body,
        grid=(m // tm,),
        in_specs=_merge_in_specs(tm, z2, wa, wb, wo),
        out_specs=[pl.BlockSpec((tm, D_MODEL), lambda i: (i, 0)),
                   pl.BlockSpec((tm, D_MODEL), lambda i: (i, 0))],
        out_shape=[jax.ShapeDtypeStruct((m, D_MODEL), F32),
                   jax.ShapeDtypeStruct((m, D_MODEL), wo.dtype)],
        compiler_params=_params("arbitrary"),
        name="merge",
    )(x2d, oa, ob, z2, z2, wa, wb, wo, nw)


FFN_CHUNKS = (768, 768, 768, 512)


def _merge_ffn_body(x_ref, oa_ref, ob_ref, ga_ref, gb_ref, wa_ref, wb_ref, wo_ref, nw_ref,
                    wg_ref, wu_ref, wd_ref, o_ref, acc_ref):
    x1 = _merge_value(x_ref, oa_ref, ob_ref, ga_ref, gb_ref, wa_ref, wb_ref, wo_ref)
    acc_ref[...] = x1
    hn = _rms(x1, nw_ref[...]).astype(wg_ref.dtype)
    off = 0
    for width in FFN_CHUNKS:
        g = _dot(hn, wg_ref[:, off:off + width])
        u = _dot(hn, wu_ref[:, off:off + width])
        acc_ref[...] += _dot((g * _sigmoid(g) * u).astype(wd_ref.dtype), wd_ref[off:off + width, :])
        off += width
    o_ref[...] = acc_ref[...]


def _merge_ffn(x2d, oa, ob, z2, wa, wb, wo, nw, wg, wu, wd, tm):
    m = x2d.shape[0]
    return pl.pallas_call(
        _merge_ffn_body,
        grid=(m // tm,),
        in_specs=(_merge_in_specs(tm, z2, wa, wb, wo)
                  + [_resident(wg.shape), _resident(wu.shape), _resident(wd.shape)]),
        out_specs=pl.BlockSpec((tm, D_MODEL), lambda i: (i, 0)),
        out_shape=jax.ShapeDtypeStruct((m, D_MODEL), F32),
        scratch_shapes=[pltpu.VMEM((tm, D_MODEL), F32)],
        compiler_params=_params("arbitrary"),
        name="merge_ffn",
    )(x2d, oa, ob, z2, z2, wa, wb, wo, nw, wg, wu, wd)


def _ffn_stream_body(hn_ref, x1_ref, wg_ref, wu_ref, wd_ref, o_ref):
    @pl.when(pl.program_id(0) == 0)
    def _():
        o_ref[...] = x1_ref[...]

    hn = hn_ref[...]
    g = _dot(hn, wg_ref[...])
    u = _dot(hn, wu_ref[...])
    o_ref[...] += _dot((g * _sigmoid(g) * u).astype(wd_ref.dtype), wd_ref[...])


def _ffn_stream(hn, x1, wg, wu, wd, tf):
    m = hn.shape[0]
    full = pl.BlockSpec((m, D_MODEL), lambda f: (0, 0))
    return pl.pallas_call(
        _ffn_stream_body,
        grid=(wg.shape[1] // tf,),
        in_specs=[full, full,
                  pl.BlockSpec((D_MODEL, tf), lambda f: (0, f)),
                  pl.BlockSpec((D_MODEL, tf), lambda f: (0, f)),
                  pl.BlockSpec((tf, D_MODEL), lambda f: (f, 0))],
        out_specs=full,
        out_shape=jax.ShapeDtypeStruct((m, D_MODEL), F32),
        compiler_params=_params("arbitrary"),
        name="ffn_stream",
    )(hn, x1, wg, wu, wd)


RT_E1, RT_E2, RT_G1, RT_G2, RT_R1, RT_R2 = range(6)
SEG_ALIGN = 16
EXPERT_CHUNKS = (1024, 1024, 1024, 512)


def _router_body(hn_ref, wr_ref, route_ref, route_t_ref, cnt_ref):
    T = hn_ref.shape[0]
    logits = _dot(hn_ref[...], wr_ref[...])
    lane = lax.broadcasted_iota(jnp.int32, logits.shape, 1).astype(F32)
    lg = jnp.where(lane < N_EXPERTS, logits, -jnp.inf)
    m1 = jnp.max(lg, axis=-1, keepdims=True)
    i1 = jnp.min(jnp.where(lg == m1, lane, float(LANES)), axis=-1, keepdims=True)
    lg2 = jnp.where(lane == i1, -jnp.inf, lg)
    m2 = jnp.max(lg2, axis=-1, keepdims=True)
    i2 = jnp.min(jnp.where(lg2 == m2, lane, float(LANES)), axis=-1, keepdims=True)
    e2 = jnp.exp(m2 - m1)
    sel = jnp.where((lane == i1) | (lane == i2), 1.0, 0.0)
    r = lax.broadcasted_iota(jnp.int32, (T, T), 0)
    c = lax.broadcasted_iota(jnp.int32, (T, T), 1)
    before = jnp.where(r > c, 1.0, 0.0).astype(BF16)
    rank = _dot(before, sel.astype(BF16))
    r1 = jnp.sum(jnp.where(lane == i1, rank, 0.0), axis=-1, keepdims=True)
    r2 = jnp.sum(jnp.where(lane == i2, rank, 0.0), axis=-1, keepdims=True)
    fields = {RT_E1: i1, RT_E2: i2, RT_G1: 1.0 / (1.0 + e2), RT_G2: e2 / (1.0 + e2), RT_R1: r1, RT_R2: r2}
    route = jnp.zeros_like(logits)
    for idx, val in fields.items():
        route = jnp.where(lane == float(idx), val, route)
    route_ref[...] = route
    route_t_ref[...] = route.T
    cnt_ref[...] = jnp.broadcast_to(jnp.sum(sel, axis=0, keepdims=True), cnt_ref.shape)


def _router(hn, wr, T):
    m = hn.shape[0]
    nt = m // T
    return pl.pallas_call(
        _router_body,
        grid=(nt,),
        in_specs=[pl.BlockSpec((T, D_MODEL), lambda i: (i, 0)),
                  pl.BlockSpec((D_MODEL, LANES), lambda i: (0, 0))],
        out_specs=[pl.BlockSpec((T, LANES), lambda i: (i, 0)),
                   pl.BlockSpec((LANES, T), lambda i: (0, i)),
                   pl.BlockSpec((None, SUBLANES, LANES), lambda i: (i, 0, 0))],
        out_shape=[jax.ShapeDtypeStruct((m, LANES), F32),
                   jax.ShapeDtypeStruct((LANES, m), F32),
                   jax.ShapeDtypeStruct((nt, SUBLANES, LANES), F32)],
        compiler_params=_params("arbitrary"),
        name="router",
    )(hn, wr)


def _segment_copies(seg_ref, so_ref, off_ref, step, make, fn, max_rows):
    sizes = []
    s = max_rows
    while s >= SEG_ALIGN:
        sizes.append(s)
        s //= 2
    for e in range(N_EXPERTS):
        n = seg_ref[step * N_EXPERTS + e]
        src = so_ref[step * N_EXPERTS + e]
        dst = off_ref[step * N_EXPERTS + e]
        for s in sizes:
            @pl.when((n & s) != 0)
            def _(s=s, n=n, src=src, dst=dst):
                done = n & (-2 * s)
                fn(make(pl.multiple_of(src + done, SEG_ALIGN), pl.multiple_of(dst + done, SEG_ALIGN), s))


def _stage_offset(so_ref, step, expert):
    out = jnp.zeros_like(expert)
    for e in range(N_EXPERTS):
        out = jnp.where(expert == float(e), so_ref[step * N_EXPERTS + e].astype(F32), out)
    return out


def _dispatch_body(seg_ref, so_ref, off_ref, tail_ref, tail_src_ref, tail_off_ref, used_ref, hn_ref, rt_ref,
                   xs_ref, stage, zero, sem, zero_sem):
    t = pl.program_id(0)
    nt = pl.num_programs(0)
    T = hn_ref.shape[0]
    S = stage.shape[1]
    slot = t % 2

    @pl.when(t == 0)
    def _():
        bm = zero.shape[0]
        zero[...] = jnp.zeros_like(zero)

        def tail_copy(src, dst, rows):
            return pltpu.make_async_copy(zero.at[pl.ds(src, rows)], xs_ref.at[pl.ds(dst, rows)], zero_sem.at[0])

        def block_copy(b):
            return pltpu.make_async_copy(zero, xs_ref.at[pl.ds(pl.multiple_of(b * bm, bm), bm)], zero_sem.at[0])

        for fn in (lambda c: c.start(), lambda c: c.wait()):
            _segment_copies(tail_ref, tail_src_ref, tail_off_ref, 0, tail_copy, fn, bm // 2)

            def blocks(b, carry, fn=fn):
                fn(block_copy(b))
                return carry
            lax.fori_loop(used_ref[0], xs_ref.shape[0] // bm, blocks, 0)

    def copies(step, slot, fn):
        def make(src, dst, rows):
            return pltpu.make_async_copy(stage.at[slot, pl.ds(src, rows)], xs_ref.at[pl.ds(dst, rows)], sem.at[slot])
        _segment_copies(seg_ref, so_ref, off_ref, step, make, fn, T)

    @pl.when(t >= 2)
    def _():
        copies(t - 2, slot, lambda c: c.wait())

    rt = rt_ref[...]
    tgt1 = _stage_offset(so_ref, t, rt[RT_E1:RT_E1 + 1, :]) + rt[RT_R1:RT_R1 + 1, :]
    tgt2 = _stage_offset(so_ref, t, rt[RT_E2:RT_E2 + 1, :]) + rt[RT_R2:RT_R2 + 1, :]
    rowi = lax.broadcasted_iota(jnp.int32, (S, T), 0).astype(F32)
    sel = jnp.where((rowi == tgt1) | (rowi == tgt2), 1.0, 0.0).astype(BF16)
    stage[slot] = _dot(sel, hn_ref[...]).astype(BF16)
    copies(t, slot, lambda c: c.start())

    @pl.when(t == nt - 1)
    def _():
        copies(t, slot, lambda c: c.wait())

        @pl.when(t >= 1)
        def _():
            copies(t - 1, 1 - slot, lambda c: c.wait())


def _dispatch(hn, route_t, plan, T, S, bm, nblk):
    m = hn.shape[0]
    grid_spec = pltpu.PrefetchScalarGridSpec(
        num_scalar_prefetch=7,
        grid=(m // T,),
        in_specs=[pl.BlockSpec((T, D_MODEL), lambda t, *_: (t, 0)),
                  pl.BlockSpec((LANES, T), lambda t, *_: (0, t))],
        out_specs=pl.BlockSpec(memory_space=pl.ANY),
        scratch_shapes=[pltpu.VMEM((2, S, D_MODEL), BF16), pltpu.VMEM((bm, D_MODEL), BF16),
                        pltpu.SemaphoreType.DMA((2,)), pltpu.SemaphoreType.DMA((1,))],
    )
    return pl.pallas_call(
        _dispatch_body,
        grid_spec=grid_spec,
        out_shape=jax.ShapeDtypeStruct((nblk * bm, D_MODEL), BF16),
        compiler_params=_params("arbitrary"),
        name="moe_dispatch",
    )(plan["seg"], plan["so"], plan["off"], plan["tail"], plan["tail_src"], plan["tail_off"], plan["used"],
      hn, route_t)


def _experts_body(owner_ref, used_ref, x_ref, wg_ref, wu_ref, wd_ref, y_ref, acc_ref):
    del owner_ref

    @pl.when(pl.program_id(0) < used_ref[0])
    def _():
        x = x_ref[...]
        off = 0
        for i, width in enumerate(EXPERT_CHUNKS):
            g = _dot(x, wg_ref[:, off:off + width])
            u = _dot(x, wu_ref[:, off:off + width])
            y = _dot((g * _sigmoid(g) * u).astype(BF16), wd_ref[off:off + width, :])
            if i == 0:
                acc_ref[...] = y
            else:
                acc_ref[...] += y
            off += width
        y_ref[...] = acc_ref[...].astype(y_ref.dtype)

    @pl.when(pl.program_id(0) >= used_ref[0])
    def _():
        y_ref[...] = jnp.zeros_like(y_ref)


def _experts(xs, plan, wg, wu, wd, bm):
    nblk = xs.shape[0] // bm
    buffers = 1 if bm >= 256 else 2

    def wspec(shape):
        return pl.BlockSpec((None,) + shape, lambda b, owner, used: (owner[b], 0, 0),
                            pipeline_mode=pl.Buffered(buffers))

    grid_spec = pltpu.PrefetchScalarGridSpec(
        num_scalar_prefetch=2,
        grid=(nblk,),
        in_specs=[pl.BlockSpec((bm, D_MODEL), lambda b, owner, used: (jnp.minimum(b, used[0] - 1), 0)),
                  wspec((D_MODEL, D_FF_EXPERT)), wspec((D_MODEL, D_FF_EXPERT)), wspec((D_FF_EXPERT, D_MODEL))],
        out_specs=pl.BlockSpec((bm, D_MODEL), lambda b, owner, used: (b, 0)),
        scratch_shapes=[pltpu.VMEM((bm, D_MODEL), F32)],
    )
    return pl.pallas_call(
        _experts_body,
        grid_spec=grid_spec,
        out_shape=jax.ShapeDtypeStruct(xs.shape, BF16),
        compiler_params=_params("arbitrary"),
        name="moe_experts",
    )(plan["owner"], plan["used"], xs, wg, wu, wd)


def _combine_body(seg_ref, so_ref, off_ref, route_ref, x1_ref, nw_ref, ys_ref, o_ref, ybuf, sem, *, final_norm):
    t = pl.program_id(0)
    nt = pl.num_programs(0)
    T = route_ref.shape[0]
    S = ybuf.shape[1]
    slot = t % 2

    def copies(step, slot, fn):
        def make(stage_row, grouped_row, rows):
            return pltpu.make_async_copy(ys_ref.at[pl.ds(grouped_row, rows)], ybuf.at[slot, pl.ds(stage_row, rows)],
                                         sem.at[slot])
        _segment_copies(seg_ref, so_ref, off_ref, step, make, fn, T)

    @pl.when(t == 0)
    def _():
        ybuf[...] = jnp.zeros_like(ybuf)
        copies(0, 0, lambda c: c.start())

    @pl.when(t + 1 < nt)
    def _():
        copies(t + 1, 1 - slot, lambda c: c.start())

    copies(t, slot, lambda c: c.wait())

    route = route_ref[...]
    tgt1 = _stage_offset(so_ref, t, route[:, RT_E1:RT_E1 + 1]) + route[:, RT_R1:RT_R1 + 1]
    tgt2 = _stage_offset(so_ref, t, route[:, RT_E2:RT_E2 + 1]) + route[:, RT_R2:RT_R2 + 1]
    coli = lax.broadcasted_iota(jnp.int32, (T, S), 1).astype(F32)
    sel = (jnp.where(coli == tgt1, route[:, RT_G1:RT_G1 + 1], 0.0)
           + jnp.where(coli == tgt2, route[:, RT_G2:RT_G2 + 1], 0.0))
    x2 = x1_ref[...] + _dot(sel.astype(BF16), ybuf[slot])
    o_ref[...] = _rms(x2, nw_ref[...]) if final_norm else x2


def _combine(ys, route, x1, nw, plan, T, S, final_norm):
    m = x1.shape[0]
    grid_spec = pltpu.PrefetchScalarGridSpec(
        num_scalar_prefetch=3,
        grid=(m // T,),
        in_specs=[pl.BlockSpec((T, LANES), lambda t, *_: (t, 0)),
                  pl.BlockSpec((T, D_MODEL), lambda t, *_: (t, 0)),
                  pl.BlockSpec((1, D_MODEL), lambda t, *_: (0, 0)),
                  pl.BlockSpec(memory_space=pl.ANY)],
        out_specs=pl.BlockSpec((T, D_MODEL), lambda t, *_: (t, 0)),
        scratch_shapes=[pltpu.VMEM((2, S, D_MODEL), BF16), pltpu.SemaphoreType.DMA((2,))],
    )
    return pl.pallas_call(
        functools.partial(_combine_body, final_norm=final_norm),
        grid_spec=grid_spec,
        out_shape=jax.ShapeDtypeStruct((m, D_MODEL), F32),
        compiler_params=_params("arbitrary"),
        name="moe_combine",
    )(plan["seg"], plan["so"], plan["off"], route, x1, nw, ys)


def _moe_plan(cnt, bm, nblk):
    seg = (cnt + (SEG_ALIGN - 1)) // SEG_ALIGN * SEG_ALIGN
    so = jnp.cumsum(seg, axis=1) - seg
    rows = jnp.sum(seg, axis=0)
    blocks = (rows + (bm - 1)) // bm
    blk_end = jnp.cumsum(blocks)
    start = (blk_end - blocks) * bm
    off = start[None, :] + jnp.cumsum(seg, axis=0) - seg
    owner = jnp.minimum(jnp.sum(jnp.arange(nblk)[:, None] >= blk_end[None, :], axis=1), N_EXPERTS - 1)
    as_i32 = lambda a: a.astype(jnp.int32).reshape(-1)
    return {"seg": as_i32(seg), "so": as_i32(so), "off": as_i32(off), "owner": as_i32(owner),
            "used": as_i32(blk_end[-1:]),
            "tail": as_i32(blocks * bm - rows), "tail_src": as_i32(jnp.zeros_like(rows)), "tail_off": as_i32(start + rows)}


def _moe_ffn(hn, x1, nw, wr, wg, wu, wd, T, bm, final_norm):
    m = hn.shape[0]
    tiles = m // T
    pad_rows = tiles * N_EXPERTS * (SEG_ALIGN - 1)
    stage_rows = -(-(2 * T + N_EXPERTS * (SEG_ALIGN - 1)) // LANES) * LANES
    nblk = -(-(2 * m + pad_rows) // bm) + N_EXPERTS
    route, route_t, cnt = _router(hn, wr, T)
    plan = _moe_plan(cnt[:, 0, :N_EXPERTS].astype(jnp.int32), bm, nblk)
    xs = _dispatch(hn.astype(BF16), route_t, plan, T, stage_rows, bm, nblk)
    ys = _experts(xs, plan, wg, wu, wd, bm)
    return _combine(ys, route, x1, nw, plan, T, stage_rows, final_norm)


def _norm_body(x_ref, nw_ref, o_ref):
    o_ref[...] = _rms(x_ref[...], nw_ref[...])


def _norm(x, nw, tm):
    m = x.shape[0]
    spec = pl.BlockSpec((tm, D_MODEL), lambda i: (i, 0))
    return pl.pallas_call(
        _norm_body, grid=(m // tm,),
        in_specs=[spec, pl.BlockSpec((1, D_MODEL), lambda i: (0, 0))],
        out_specs=spec, out_shape=jax.ShapeDtypeStruct((m, D_MODEL), F32),
        compiler_params=_params("arbitrary"), name="norm",
    )(x, nw)


def _layout_w_in(w):
    o = 0
    hg = w[:, o:o + 4 * HG_W]; o += 4 * HG_W
    sq = w[:, o:o + SWQ_W]; o += SWQ_W
    sk = w[:, o:o + KV_W]; o += KV_W
    sv = w[:, o:o + KV_W]; o += KV_W
    ga = w[:, o:o + D_MODEL]; o += D_MODEL
    gb = w[:, o:o + D_MODEL]
    pairs = [sq[:, h * SW_HD:(h + 1) * SW_HD] for j in range(SW_PAIRS) for h in (j, j + SW_GROUP)]
    return jnp.concatenate([hg, ga, gb] + pairs + [sk, sv], axis=1)


def _layout_w_branch_b(w):
    return jnp.concatenate([w[h * SW_HD:(h + 1) * SW_HD] for j in range(SW_PAIRS) for h in (j, j + SW_GROUP)], axis=0)


def _rope_tables(pos):
    half = SW_HD // 2
    inv = ROPE_THETA ** (-jnp.arange(half, dtype=F32) / half)
    ang = pos.astype(F32)[:, None] * inv[None, :]
    cos = jnp.cos(ang)
    sin = jnp.sin(ang)
    reps = LANES // SW_HD
    return jnp.tile(cos, (1, 2 * reps)), jnp.tile(jnp.concatenate([-sin, sin], axis=1), (1, reps))


def _rotate_half_matrix():
    j = jnp.arange(LANES)
    src = jnp.where((j % SW_HD) < SW_HD // 2, j + SW_HD // 2, j - SW_HD // 2)
    return (jnp.arange(LANES)[:, None] == src[None, :]).astype(F32)


def _tile(m, pref):
    return pref if m % pref == 0 else m


def kernel(x_prompt, x_sample, state_hgrn, cache_swa_k, cache_swa_v, norm_mix, w_in, hg_lb_logits, hg_norm,
           swa_sinks, w_branch_a, w_branch_b, w_out, norm_ffn, w_gate_dense, w_up_dense, w_down_dense,
           w_router, w_gate_moe, w_up_moe, w_down_moe, norm_final):
    depth = w_in.shape[0]
    bsz, seq, _ = x_prompt.shape
    nsamp = x_sample.shape[0]
    mp = bsz * seq
    wb = cache_swa_k.shape[2]

    cos_p, sin_p = _rope_tables(jnp.arange(seq))
    cos_s, sin_s = _rope_tables(PAST_LEN + jnp.arange(1))
    perm32 = _rotate_half_matrix()
    perm = perm32.astype(BF16)
    lbl = hg_lb_logits.astype(F32)
    state32 = state_hgrn.astype(F32)
    kcache = cache_swa_k.reshape(depth, nsamp, wb, KV_W).astype(F32)
    vcache = cache_swa_v.reshape(depth, nsamp, wb, KV_W).astype(F32)

    xp = x_prompt.reshape(mp, D_MODEL)
    xs = x_sample.reshape(nsamp, D_MODEL)
    outs = {k: [] for k in ("sp", "kp", "vp", "ss", "ks", "vs")}
    nfin = norm_final.reshape(1, D_MODEL)
    normed = False

    for l in range(depth):
        w_in32 = _layout_w_in(w_in[l])
        wa32 = w_branch_a[l]
        wbb32 = _layout_w_branch_b(w_branch_b[l])
        wo32 = w_out[l]
        w_in_l = w_in32.astype(BF16)
        wa = wa32.astype(BF16)
        wbb = wbb32.astype(BF16)
        wo = wo32.astype(BF16)
        nmix = norm_mix[l].reshape(1, D_MODEL)
        nffn = norm_ffn[l].reshape(1, D_MODEL)
        hgn = hg_norm[l].reshape(1, HG_D)
        sinks = swa_sinks[l].astype(F32)

        zp = _proj(xp, nmix, w_in_l, _tile(mp, 512))
        zp3 = zp.reshape(bsz, seq, Z_WIDTH)
        dense = (w_gate_dense, w_up_dense, w_down_dense) if l % 2 == 0 else ()
        moe_next = (w_gate_moe, w_up_moe, w_down_moe) if (l + 1 < depth and l % 2 == 0) else ()
        views = ([a[l // 2].reshape(D_MODEL, D_FF_DENSE) for a in dense]
                 + [a[(l + 1) // 2].reshape(-1, a.shape[-1]) for a in moe_next])
        oa_p, s_p, cast = _hgrn_prompt(zp3, lbl, hgn, l, tuple(views))
        if dense:
            dense_bf16 = tuple(c.reshape(a.shape[1:]) for c, a in zip(cast, dense))
        if moe_next:
            moe_bf16 = tuple(c.reshape(a.shape[1:]) for c, a in zip(cast[len(dense):], moe_next))
        ob_p, k_p, v_p = _swa_prompt(zp3, sinks, cos_p, sin_p, perm)
        oa_p = oa_p.reshape(mp, HG_W)
        ob_p = ob_p.reshape(mp, SWQ_W)

        zs = _proj(xs, nmix, w_in32, nsamp)
        oa_s, s_s = _hgrn_decode(zs, state32, lbl, hgn, l)
        q2 = zs[:, N_HG_COLS + R_SWQ:N_HG_COLS + R_SWQ + SWQ_W].reshape(nsamp * SW_PAIRS, LANES)
        sink_row = jnp.pad(sinks, (0, LANES - SW_Q_HEADS)).reshape(1, LANES)
        ob_s, k_s, v_s = _swa_decode(q2, zs, kcache, vcache, l, cos_s, sin_s, perm32, sink_row)
        x1s, hns = _merge(xs, oa_s, ob_s.reshape(nsamp, SWQ_W), zs, wa32, wbb32, wo32, nffn, nsamp)

        j = l // 2
        if l % 2 == 0:
            xp = _merge_ffn(xp, oa_p, ob_p, zp, wa, wbb, wo, nffn, *dense_bf16, _tile(mp, 512))
            xs = _ffn_stream(hns, x1s, w_gate_dense[j], w_up_dense[j], w_down_dense[j], 256)
        else:
            x1p, hnp = _merge(xp, oa_p, ob_p, zp, wa, wbb, wo, nffn, _tile(mp, 512))
            wr32 = jnp.pad(w_router[j], ((0, 0), (0, LANES - N_EXPERTS)))
            wg, wu, wd = moe_bf16
            last = l == depth - 1
            xp = _moe_ffn(hnp, x1p, nfin, wr32.astype(BF16), wg, wu, wd, _tile(mp, 512), 512, last)
            xs = _moe_ffn(hns, x1s, nfin, wr32, wg, wu, wd, nsamp, 128, last)
            normed = last

        outs["sp"].append(s_p)
        outs["kp"].append(k_p.reshape(bsz, WINDOW, SW_KV_HEADS, SW_HD))
        outs["vp"].append(v_p.reshape(bsz, WINDOW, SW_KV_HEADS, SW_HD))
        outs["ss"].append(s_s.astype(state_hgrn.dtype))
        outs["ks"].append(k_s.reshape(nsamp, wb, SW_KV_HEADS, SW_HD))
        outs["vs"].append(v_s.reshape(nsamp, wb, SW_KV_HEADS, SW_HD))

    yp, ys = (xp, xs) if normed else (_norm(xp, nfin, _tile(mp, 1024)), _norm(xs, nfin, nsamp))

    return (yp.reshape(x_prompt.shape), ys.reshape(x_sample.shape),
            jnp.stack(outs["sp"]), jnp.stack(outs["kp"]), jnp.stack(outs["vp"]),
            jnp.stack(outs["ss"]), jnp.stack(outs["ks"]), jnp.stack(outs["vs"]))
```

```python
import functools

import jax
import jax.numpy as jnp
from jax import lax
from jax.experimental import pallas as pl
from jax.experimental.pallas import tpu as pltpu

F32 = jnp.float32
BF16 = jnp.bfloat16

D_MODEL = 1024
PAST_LEN = 16384
HG_HEADS = 4
HG_D = 128
HG_W = HG_HEADS * HG_D
SW_Q_HEADS = 8
SW_KV_HEADS = 2
SW_HD = 64
SW_GROUP = SW_Q_HEADS // SW_KV_HEADS
WINDOW = 128
ROPE_THETA = 10000.0
D_FF_DENSE = 2816
N_EXPERTS = 8
D_FF_EXPERT = 3584
RMS_EPS = 1e-6

LANES = 128
SUBLANES = 8
VMEM_LIMIT = 56 * 1024 * 1024

SWQ_W = SW_Q_HEADS * SW_HD
SW_PAIRS = SWQ_W // LANES
KV_W = SW_KV_HEADS * SW_HD
Z_WIDTH = 4 * HG_W + 2 * D_MODEL + SWQ_W + 2 * KV_W
ZB_Q, ZB_F, ZB_I, ZB_G = 0, 1, 2, 3
N_HG_COLS = 4 * HG_W
R_GA, R_GB, R_SWQ, R_K, R_V = 0, D_MODEL, 2 * D_MODEL, 2 * D_MODEL + SWQ_W, 2 * D_MODEL + SWQ_W + KV_W
PROJ_CHUNKS = (1024, 1024, 1024, 1024, 768)

HG_CHUNK = 128
SW_UNROLL = 4
SEQ_BLOCK = 512
DEC_TILE = 8
ROW_TILE = 512
MERGE_TILE = 1024
NORM_TILE = 1024
MOE_BLOCK = 512
DEC_MOE_BLOCK = 128


def _dot_dims(a, b, dims):
    precision = lax.Precision.HIGHEST if a.dtype == F32 else None
    return lax.dot_general(a, b, (dims, ((), ())), precision=precision, preferred_element_type=F32)


def _dot(a, b):
    return _dot_dims(a, b, ((1,), (0,)))


def _dot_nt(a, b):
    return _dot_dims(a, b, ((1,), (1,)))


def _dot_tn(a, b):
    return _dot_dims(a, b, ((0,), (0,)))


def _sigmoid(x):
    return 1.0 / (1.0 + jnp.exp(-x))


def _rms(x, w):
    ms = jnp.mean(x * x, axis=-1, keepdims=True)
    return x * lax.rsqrt(ms + RMS_EPS) * w


def _params(*sem):
    return pltpu.CompilerParams(dimension_semantics=sem, vmem_limit_bytes=VMEM_LIMIT)


def _proj_body(x_ref, nw_ref, w_ref, z_ref):
    h = _rms(x_ref[...], nw_ref[...]).astype(w_ref.dtype)
    off = 0
    for width in PROJ_CHUNKS:
        z_ref[:, off:off + width] = _dot(h, w_ref[:, off:off + width]).astype(z_ref.dtype)
        off += width


def _proj(x2d, nw, w, tm):
    m = x2d.shape[0]
    n = w.shape[1]
    return pl.pallas_call(
        _proj_body,
        grid=(m // tm,),
        in_specs=[
            pl.BlockSpec((tm, D_MODEL), lambda i: (i, 0)),
            pl.BlockSpec((1, D_MODEL), lambda i: (0, 0)),
            pl.BlockSpec((D_MODEL, n), lambda i: (0, 0), pipeline_mode=pl.Buffered(1)),
        ],
        out_specs=pl.BlockSpec((tm, n), lambda i: (i, 0)),
        out_shape=jax.ShapeDtypeStruct((m, n), w.dtype),
        compiler_params=_params("arbitrary"),
        name="proj",
    )(x2d, nw, w)


def _lower_bound(lbl, layer):
    mx = jnp.max(lbl, axis=0, keepdims=True)
    e = jnp.exp(lbl - mx)
    sm = e / jnp.sum(e, axis=0, keepdims=True)
    cum = sm[0:1, :]
    for i in range(1, layer + 1):
        cum = cum + sm[i:i + 1, :]
    return cum - sm[0:1, :]


def _split3(x):
    hi = x.astype(BF16)
    r = x - hi.astype(F32)
    mid = r.astype(BF16)
    lo = (r - mid.astype(F32)).astype(BF16)
    return hi, mid, lo


def _hgrn_tile(zh_ref, lb, nw, o_ref, st_scr, b_all_scr):
    C = HG_CHUNK
    col_q, col_f, col_i, col_g = (slice(b * HG_W, (b + 1) * HG_W) for b in (ZB_Q, ZB_F, ZB_I, ZB_G))
    row = lax.broadcasted_iota(jnp.int32, (C, C), 0)
    col = lax.broadcasted_iota(jnp.int32, (C, C), 1)
    xr = row ^ col
    tri = jnp.where(row >= col, 1.0, 0.0).astype(BF16)
    sub4 = (lax.broadcasted_iota(jnp.int32, (SUBLANES, HG_D), 0) & 4) == 0
    levels = (1, 2, 4, 8, 16, 32, 64)
    pair_level = {m: (xr >= m) & (xr < 2 * m) for m in levels}
    upper_half = {m: (row & m) != 0 for m in levels if m < SUBLANES}

    def chunk(rows, b_scr):
        hf = zh_ref[rows, col_f].astype(F32)
        fg_all = lb + (1.0 - lb) * _sigmoid(hf)
        g_all = jnp.log2(fg_all)
        k_all = 1.0 - fg_all
        g1, g2, g3 = _split3(g_all)
        b_all = _dot(tri, g1) + _dot(tri, g2) + _dot(tri, g3)
        b_scr[...] = b_all
        hq = zh_ref[rows, col_q].astype(F32)
        q_all = hq * _sigmoid(hq) * (HG_D ** -0.5)
        gate = zh_ref[rows, col_g].astype(F32)
        gate_all = gate * _sigmoid(gate)
        for h in range(HG_HEADS):
            sl = slice(h * HG_D, (h + 1) * HG_D)
            head(h, rows, b_scr, b_all[:, sl], q_all[:, sl], k_all[:, sl], fg_all[:, sl], gate_all[:, sl])

    def head(h, rows, b_scr, b, q, k, f, gate):
        sl = slice(h * HG_D, (h + 1) * HG_D)
        v = zh_ref[rows, col_i.start + h * HG_D:col_i.start + (h + 1) * HG_D]

        acc = _dot_nt(q.astype(BF16), k.astype(BF16))
        for m in levels:
            if m == 1:
                w = jnp.where(upper_half[m], q * f, k)
            elif m < SUBLANES:
                if m == 2:
                    pieces = []
                    for j in range(C // SUBLANES):
                        lo = jnp.broadcast_to(b_scr[SUBLANES * j + 1:SUBLANES * j + 2, sl], (SUBLANES, HG_D))
                        hi = jnp.broadcast_to(b_scr[SUBLANES * j + 5:SUBLANES * j + 6, sl], (SUBLANES, HG_D))
                        pieces.append(jnp.where(sub4, lo, hi))
                else:
                    pieces = [jnp.broadcast_to(b_scr[i * 2 * m + m - 1:i * 2 * m + m, sl], (2 * m, HG_D))
                              for i in range(C // (2 * m))]
                d = b - jnp.concatenate(pieces, axis=0)
                w = jnp.where(upper_half[m], q, k) * jnp.exp2(jnp.where(upper_half[m], d, -d))
            else:
                expo, qk = [], []
                for i in range(C // (2 * m)):
                    lo, mid, hi = i * 2 * m, i * 2 * m + m, (i + 1) * 2 * m
                    bref = b_scr[mid - 1:mid, sl]
                    expo += [bref - b[lo:mid], b[mid:hi] - bref]
                    qk += [k[lo:mid], q[mid:hi]]
                w = jnp.concatenate(qk, axis=0) * jnp.exp2(jnp.concatenate(expo, axis=0))
            wb = w.astype(BF16)
            acc = jnp.where(pair_level[m], _dot_nt(wb, wb), acc)
        a = jnp.where(row >= col, acc, 0.0)

        st = st_scr[h]
        o = _dot(a.astype(BF16), v) + _dot_nt((q * jnp.exp2(b)).astype(BF16), st.astype(BF16))
        b_last = b[C - 1:C, :]
        kd = k * jnp.exp2(b_last - b)
        st_scr[h] = jnp.exp2(b_last) * st + _dot_tn(v, kd.astype(BF16))

        o_ref[rows, sl] = (_rms(o, nw) * gate).astype(o_ref.dtype)

    for u in range(zh_ref.shape[0] // C):
        chunk(slice(u * C, (u + 1) * C), b_all_scr.at[u])


def _hgrn_prompt_body(zh_ref, lbl_ref, nw_ref, *refs, layer, n_cast):
    cast_in = refs[:n_cast]
    o_ref, s_ref = refs[n_cast:n_cast + 2]
    cast_out = refs[n_cast + 2:2 * n_cast + 2]
    st_scr, b_all_scr = refs[2 * n_cast + 2:]
    c = pl.program_id(1)

    @pl.when(c == 0)
    def _():
        st_scr[...] = jnp.zeros_like(st_scr)

    for src, dst in zip(cast_in, cast_out):
        dst[...] = src[...].astype(dst.dtype)

    _hgrn_tile(zh_ref, _lower_bound(lbl_ref[...], layer), nw_ref[...], o_ref, st_scr, b_all_scr)

    @pl.when(c == pl.num_programs(1) - 1)
    def _():
        for h in range(HG_HEADS):
            s_ref[h] = st_scr[h].T


def _hgrn_prompt(z3, lbl, nw, layer, cast=()):
    bsz, seq, _ = z3.shape
    rows = _tile(seq, SEQ_BLOCK)
    steps_per_seq = seq // rows
    steps = bsz * steps_per_seq

    def slab(a):
        assert a.shape[0] % (steps * 2 * SUBLANES) == 0, a.shape
        return pl.BlockSpec((a.shape[0] // steps, a.shape[1]), lambda b, c: (b * steps_per_seq + c, 0))

    outs = pl.pallas_call(
        functools.partial(_hgrn_prompt_body, layer=layer, n_cast=len(cast)),
        grid=(bsz, steps_per_seq),
        in_specs=[pl.BlockSpec((None, rows, N_HG_COLS), lambda b, c: (b, c, 0)),
                  pl.BlockSpec(lbl.shape, lambda b, c: (0, 0)),
                  pl.BlockSpec((1, HG_D), lambda b, c: (0, 0))] + [slab(a) for a in cast],
        out_specs=[pl.BlockSpec((None, rows, HG_W), lambda b, c: (b, c, 0)),
                   pl.BlockSpec((None, HG_HEADS, HG_D, HG_D), lambda b, c: (b, 0, 0, 0))] + [slab(a) for a in cast],
        out_shape=[jax.ShapeDtypeStruct((bsz, seq, HG_W), BF16),
                   jax.ShapeDtypeStruct((bsz, HG_HEADS, HG_D, HG_D), F32)]
        + [jax.ShapeDtypeStruct(a.shape, BF16) for a in cast],
        scratch_shapes=[pltpu.VMEM((HG_HEADS, HG_D, HG_D), F32),
                        pltpu.VMEM((rows // HG_CHUNK, HG_CHUNK, HG_W), F32)],
        compiler_params=_params("arbitrary", "arbitrary"),
        name="hgrn_prompt",
    )(z3, lbl, nw, *cast)
    return outs[0], outs[1], outs[2:]


def _hgrn_decode_body(zq_ref, zf_ref, zi_ref, zg_ref, lbl_ref, nw_ref, s_ref, o_ref, sn_ref, o_scr, *, layer):
    lb = _lower_bound(lbl_ref[...], layer)
    fg = lb + (1.0 - lb) * _sigmoid(zf_ref[...].astype(F32))
    kk = 1.0 - fg
    hq = zq_ref[...].astype(F32)
    q = hq * _sigmoid(hq) * (HG_D ** -0.5)
    v = zi_ref[...].astype(F32)
    eye = (lax.broadcasted_iota(jnp.int32, (HG_D, HG_D), 0)
           == lax.broadcasted_iota(jnp.int32, (HG_D, HG_D), 1))

    def column(r):
        return jnp.sum(jnp.where(eye, jnp.broadcast_to(r, (HG_D, HG_D)), 0.0), axis=1, keepdims=True)

    for i in range(DEC_TILE):
        for h in range(HG_HEADS):
            sl = slice(h * HG_D, (h + 1) * HG_D)
            sn = column(fg[i:i + 1, sl]) * s_ref[i, h] + column(kk[i:i + 1, sl]) * v[i:i + 1, sl]
            sn_ref[i, h] = sn
            o_scr[i:i + 1, sl] = jnp.sum(column(q[i:i + 1, sl]) * sn, axis=0, keepdims=True)

    gate = zg_ref[...].astype(F32)
    gate = gate * _sigmoid(gate)
    nw = nw_ref[...]
    for h in range(HG_HEADS):
        sl = slice(h * HG_D, (h + 1) * HG_D)
        o_ref[:, sl] = (_rms(o_scr[:, sl], nw) * gate[:, sl]).astype(o_ref.dtype)


def _hgrn_decode(z2, state, lbl, nw, layer):
    n = z2.shape[0]
    T = DEC_TILE

    def zspec(blk):
        return pl.BlockSpec((T, HG_W), lambda i: (i, blk))

    sblock = (T, HG_HEADS, HG_D, HG_D)
    return pl.pallas_call(
        functools.partial(_hgrn_decode_body, layer=layer),
        grid=(n // T,),
        in_specs=[zspec(ZB_Q), zspec(ZB_F), zspec(ZB_I), zspec(ZB_G),
                  pl.BlockSpec(lbl.shape, lambda i: (0, 0)),
                  pl.BlockSpec((1, HG_D), lambda i: (0, 0)),
                  pl.BlockSpec((None,) + sblock, lambda i: (layer, i, 0, 0, 0))],
        out_specs=[pl.BlockSpec((T, HG_W), lambda i: (i, 0)), pl.BlockSpec(sblock, lambda i: (i, 0, 0, 0))],
        out_shape=[jax.ShapeDtypeStruct((n, HG_W), z2.dtype),
                   jax.ShapeDtypeStruct(state.shape[1:], F32)],
        scratch_shapes=[pltpu.VMEM((T, HG_W), F32)],
        compiler_params=_params("arbitrary"),
        name="hgrn_decode",
    )(z2, z2, z2, z2, lbl, nw, state)


def _rope(x, cos, sin, perm):
    return x.astype(F32) * cos + _dot(x, perm) * sin


def _swa_prompt_body(sinks_ref, zq_ref, zk_ref, zv_ref, cos_ref, sin_ref, perm_ref,
                     o_ref, kr_ref, vr_ref, kprev, vprev):
    n = pl.program_id(1)
    W = WINDOW

    @pl.when(n == 0)
    def _():
        kprev[...] = jnp.zeros_like(kprev)
        vprev[...] = jnp.zeros_like(vprev)

    perm = perm_ref[...]
    qi = lax.broadcasted_iota(jnp.int32, (W, 2 * W), 0)
    kj = lax.broadcasted_iota(jnp.int32, (W, 2 * W), 1)
    band = (kj >= qi) & (kj <= qi + W)
    low_half = lax.broadcasted_iota(jnp.int32, (W, LANES), 1) < SW_HD
    heads = range(SW_Q_HEADS)
    scale = SW_HD ** -0.5

    def blocks(i, carry):
        for u in range(SW_UNROLL):
            block(i * SW_UNROLL + u)
        return carry

    def block(i):
        rows = pl.ds(pl.multiple_of(i * W, W), W)
        cos = cos_ref[rows, :]
        sin = sin_ref[rows, :]
        k_rot = _rope(zk_ref[rows, :], cos, sin, perm)
        kr_ref[...] = k_rot
        vr_ref[...] = zv_ref[rows, :].astype(F32)
        k_cur = k_rot.astype(BF16)
        v_cur = zv_ref[rows, :]
        kk = jnp.concatenate([kprev[...], k_cur], axis=0)
        vv = jnp.concatenate([vprev[...], v_cur], axis=0)
        valid = band & (((n > 0) | (i > 0)) | (kj >= W))

        q_in = [zq_ref[rows, j * LANES:(j + 1) * LANES] for j in range(SW_PAIRS)]
        rot = _dot(jnp.concatenate(q_in, axis=0), perm)
        pair = [(q_in[j].astype(F32) * cos + rot[j * W:(j + 1) * W] * sin) * scale for j in range(SW_PAIRS)]
        q_rot = [jnp.where(low_half if h < SW_PAIRS else ~low_half, pair[h % SW_PAIRS], 0.0).astype(BF16)
                 for h in heads]
        s_all = _dot_nt(jnp.concatenate(q_rot, axis=0), kk)
        p_all, den_all = [], []
        for h in heads:
            s = jnp.where(valid, s_all[h * W:(h + 1) * W], -jnp.inf)
            sink = sinks_ref[h]
            mx = jnp.maximum(jnp.max(s, axis=-1, keepdims=True), sink)
            p = jnp.exp(s - mx)
            den_all.append(jnp.sum(p, axis=-1, keepdims=True) + jnp.exp(sink - mx))
            p_all.append(p.astype(BF16))
        o_all = _dot(jnp.concatenate(p_all, axis=0), vv)
        o_head = [o_all[h * W:(h + 1) * W] / den_all[h] for h in heads]
        for j in range(SW_PAIRS):
            o_pair = jnp.where(low_half, o_head[j], o_head[j + SW_PAIRS])
            o_ref[rows, j * LANES:(j + 1) * LANES] = o_pair.astype(o_ref.dtype)
        kprev[...] = k_cur
        vprev[...] = v_cur

    lax.fori_loop(0, zq_ref.shape[0] // (W * SW_UNROLL), blocks, 0)


def _swa_prompt(z3, sinks, cos, sin, perm):
    bsz, seq, _ = z3.shape
    W = WINDOW
    rows = _tile(seq, SEQ_BLOCK)
    grid_spec = pltpu.PrefetchScalarGridSpec(
        num_scalar_prefetch=1,
        grid=(bsz, seq // rows),
        in_specs=[pl.BlockSpec((None, rows, SWQ_W), lambda b, n, s: (b, n, (N_HG_COLS + R_SWQ) // SWQ_W)),
                  pl.BlockSpec((None, rows, KV_W), lambda b, n, s: (b, n, (N_HG_COLS + R_K) // KV_W)),
                  pl.BlockSpec((None, rows, KV_W), lambda b, n, s: (b, n, (N_HG_COLS + R_V) // KV_W)),
                  pl.BlockSpec((rows, LANES), lambda b, n, s: (n, 0)),
                  pl.BlockSpec((rows, LANES), lambda b, n, s: (n, 0)),
                  pl.BlockSpec((LANES, LANES), lambda b, n, s: (0, 0))],
        out_specs=[pl.BlockSpec((None, rows, SWQ_W), lambda b, n, s: (b, n, 0)),
                   pl.BlockSpec((None, W, KV_W), lambda b, n, s: (b, 0, 0)),
                   pl.BlockSpec((None, W, KV_W), lambda b, n, s: (b, 0, 0))],
        scratch_shapes=[pltpu.VMEM((W, KV_W), BF16), pltpu.VMEM((W, KV_W), BF16)],
    )
    return pl.pallas_call(
        _swa_prompt_body,
        grid_spec=grid_spec,
        out_shape=[jax.ShapeDtypeStruct((bsz, seq, SWQ_W), BF16),
                   jax.ShapeDtypeStruct((bsz, W, KV_W), F32),
                   jax.ShapeDtypeStruct((bsz, W, KV_W), F32)],
        compiler_params=_params("arbitrary", "arbitrary"),
        name="swa_prompt",
    )(sinks, z3, z3, z3, cos, sin, perm)


def _swa_decode_body(q_ref, zk_ref, zv_ref, kc_ref, vc_ref, cos_ref, sin_ref, perm_ref, sink_ref,
                     o_ref, kn_ref, vn_ref):
    cos = cos_ref[...]
    sin = sin_ref[...]
    perm = perm_ref[...]
    k_new = _rope(zk_ref[...], cos, sin, perm)
    v_new = zv_ref[...].astype(F32)
    sink = sink_ref[...]
    W = kc_ref.shape[1]
    for i in range(DEC_TILE):
        kn_ref[i, 0:W - 1, :] = kc_ref[i, 1:W, :]
        vn_ref[i, 0:W - 1, :] = vc_ref[i, 1:W, :]
        kn_ref[i, W - 1:W, :] = k_new[i:i + 1, :]
        vn_ref[i, W - 1:W, :] = v_new[i:i + 1, :]
    heads = range(SW_Q_HEADS)
    lane_row = lax.broadcasted_iota(jnp.int32, (1, LANES), 1)
    lane = lax.broadcasted_iota(jnp.int32, (W, LANES), 1)
    low_half = lane_row < SW_HD
    q_all = _rope(q_ref[...], cos, sin, perm) * (SW_HD ** -0.5)
    for i in range(DEC_TILE):
        keys = kc_ref[i]
        vals = vc_ref[i]
        s = jnp.zeros((W, LANES), F32)
        s_new = jnp.zeros((1, LANES), F32)
        q_heads = []
        for h in heads:
            r = i * SW_PAIRS + h % SW_PAIRS
            q_heads.append(jnp.where(low_half if h < SW_PAIRS else ~low_half, q_all[r:r + 1, :], 0.0))
        for h in heads:
            qh = q_heads[h]
            s = jnp.where(lane == h, jnp.sum(keys * qh, axis=-1, keepdims=True), s)
            s_new = jnp.where(lane_row == h, jnp.sum(k_new[i:i + 1, :] * qh, axis=-1, keepdims=True), s_new)
        mx = jnp.maximum(jnp.maximum(jnp.max(s, axis=0, keepdims=True), s_new), sink)
        p = jnp.exp(s - mx)
        p_new = jnp.exp(s_new - mx)
        den = jnp.sum(p, axis=0, keepdims=True) + p_new + jnp.exp(sink - mx)
        w_new = p_new / den
        wgt = p / den
        o_head = [jnp.sum(wgt[:, h:h + 1] * vals, axis=0, keepdims=True) + w_new[:, h:h + 1] * v_new[i:i + 1, :]
                  for h in heads]
        for j in range(SW_PAIRS):
            r = i * SW_PAIRS + j
            o_ref[r:r + 1, :] = jnp.where(low_half, o_head[j], o_head[j + SW_PAIRS]).astype(o_ref.dtype)


def _swa_decode(q2, z2, kc, vc, layer, cos, sin, perm, sink_row):
    n = z2.shape[0]
    T = DEC_TILE
    W = kc.shape[2]
    cache = pl.BlockSpec((None, T, W, KV_W), lambda i: (layer, i, 0, 0))
    return pl.pallas_call(
        _swa_decode_body,
        grid=(n // T,),
        in_specs=[pl.BlockSpec((T * SW_PAIRS, LANES), lambda i: (i, 0)),
                  pl.BlockSpec((T, KV_W), lambda i: (i, (N_HG_COLS + R_K) // KV_W)),
                  pl.BlockSpec((T, KV_W), lambda i: (i, (N_HG_COLS + R_V) // KV_W)),
                  cache, cache,
                  pl.BlockSpec((1, LANES), lambda i: (0, 0)),
                  pl.BlockSpec((1, LANES), lambda i: (0, 0)),
                  pl.BlockSpec((LANES, LANES), lambda i: (0, 0)),
                  pl.BlockSpec((1, LANES), lambda i: (0, 0))],
        out_specs=[pl.BlockSpec((T * SW_PAIRS, LANES), lambda i: (i, 0)),
                   pl.BlockSpec((T, W, KV_W), lambda i: (i, 0, 0)),
                   pl.BlockSpec((T, W, KV_W), lambda i: (i, 0, 0))],
        out_shape=[jax.ShapeDtypeStruct((n * SW_PAIRS, LANES), q2.dtype),
                   jax.ShapeDtypeStruct((n, W, KV_W), F32),
                   jax.ShapeDtypeStruct((n, W, KV_W), F32)],
        compiler_params=_params("arbitrary"),
        name="swa_decode",
    )(q2, z2, z2, kc, vc, cos, sin, perm, sink_row)


def _merge_value(x_ref, oa_ref, ob_ref, ga_ref, gb_ref, wa_ref, wb_ref, wo_ref):
    ta = _dot(oa_ref[...], wa_ref[...])
    tb = _dot(ob_ref[...], wb_ref[...])
    mg = _sigmoid(ga_ref[...].astype(F32)) * ta + _sigmoid(gb_ref[...].astype(F32)) * tb
    return x_ref[...] + _dot(mg.astype(wo_ref.dtype), wo_ref[...])


def _merge_body(x_ref, oa_ref, ob_ref, ga_ref, gb_ref, wa_ref, wb_ref, wo_ref, nw_ref, x1_ref, hn_ref):
    x1 = _merge_value(x_ref, oa_ref, ob_ref, ga_ref, gb_ref, wa_ref, wb_ref, wo_ref)
    x1_ref[...] = x1
    hn_ref[...] = _rms(x1, nw_ref[...]).astype(hn_ref.dtype)


def _resident(shape):
    return pl.BlockSpec(shape, lambda i: (0,) * len(shape), pipeline_mode=pl.Buffered(1))


def _merge_in_specs(tm, wa, wb, wo):
    return [pl.BlockSpec((tm, D_MODEL), lambda i: (i, 0)),
            pl.BlockSpec((tm, HG_W), lambda i: (i, 0)),
            pl.BlockSpec((tm, SWQ_W), lambda i: (i, 0)),
            pl.BlockSpec((tm, D_MODEL), lambda i: (i, (N_HG_COLS + R_GA) // D_MODEL)),
            pl.BlockSpec((tm, D_MODEL), lambda i: (i, (N_HG_COLS + R_GB) // D_MODEL)),
            _resident(wa.shape), _resident(wb.shape), _resident(wo.shape), _resident((1, D_MODEL))]


def _merge(x2d, oa, ob, z2, wa, wb, wo, nw, tm):
    m = x2d.shape[0]
    return pl.pallas_call(
        _merge_body,
        grid=(m // tm,),
        in_specs=_merge_in_specs(tm, wa, wb, wo),
        out_specs=[pl.BlockSpec((tm, D_MODEL), lambda i: (i, 0)),
                   pl.BlockSpec((tm, D_MODEL), lambda i: (i, 0))],
        out_shape=[jax.ShapeDtypeStruct((m, D_MODEL), F32),
                   jax.ShapeDtypeStruct((m, D_MODEL), wo.dtype)],
        compiler_params=_params("arbitrary"),
        name="merge",
    )(x2d, oa, ob, z2, z2, wa, wb, wo, nw)


FFN_CHUNKS = (768, 768, 768, 512)


def _merge_ffn_body(x_ref, oa_ref, ob_ref, ga_ref, gb_ref, wa_ref, wb_ref, wo_ref, nw_ref,
                    wg_ref, wu_ref, wd_ref, o_ref, acc_ref):
    x1 = _merge_value(x_ref, oa_ref, ob_ref, ga_ref, gb_ref, wa_ref, wb_ref, wo_ref)
    acc_ref[...] = x1
    hn = _rms(x1, nw_ref[...]).astype(wg_ref.dtype)
    off = 0
    for width in FFN_CHUNKS:
        g = _dot(hn, wg_ref[:, off:off + width])
        u = _dot(hn, wu_ref[:, off:off + width])
        acc_ref[...] += _dot((g * _sigmoid(g) * u).astype(wd_ref.dtype), wd_ref[off:off + width, :])
        off += width
    o_ref[...] = acc_ref[...]


def _merge_ffn(x2d, oa, ob, z2, wa, wb, wo, nw, wg, wu, wd, tm):
    m = x2d.shape[0]
    return pl.pallas_call(
        _merge_ffn_body,
        grid=(m // tm,),
        in_specs=(_merge_in_specs(tm, wa, wb, wo)
                  + [_resident(wg.shape), _resident(wu.shape), _resident(wd.shape)]),
        out_specs=pl.BlockSpec((tm, D_MODEL), lambda i: (i, 0)),
        out_shape=jax.ShapeDtypeStruct((m, D_MODEL), F32),
        scratch_shapes=[pltpu.VMEM((tm, D_MODEL), F32)],
        compiler_params=_params("arbitrary"),
        name="merge_ffn",
    )(x2d, oa, ob, z2, z2, wa, wb, wo, nw, wg, wu, wd)


def _ffn_stream_body(hn_ref, x1_ref, wg_ref, wu_ref, wd_ref, o_ref):
    @pl.when(pl.program_id(0) == 0)
    def _():
        o_ref[...] = x1_ref[...]

    hn = hn_ref[...]
    g = _dot(hn, wg_ref[...])
    u = _dot(hn, wu_ref[...])
    o_ref[...] += _dot((g * _sigmoid(g) * u).astype(wd_ref.dtype), wd_ref[...])


def _ffn_stream(hn, x1, wg, wu, wd, tf):
    m = hn.shape[0]
    full = pl.BlockSpec((m, D_MODEL), lambda f: (0, 0))
    return pl.pallas_call(
        _ffn_stream_body,
        grid=(wg.shape[1] // tf,),
        in_specs=[full, full,
                  pl.BlockSpec((D_MODEL, tf), lambda f: (0, f)),
                  pl.BlockSpec((D_MODEL, tf), lambda f: (0, f)),
                  pl.BlockSpec((tf, D_MODEL), lambda f: (f, 0))],
        out_specs=full,
        out_shape=jax.ShapeDtypeStruct((m, D_MODEL), F32),
        compiler_params=_params("arbitrary"),
        name="ffn_stream",
    )(hn, x1, wg, wu, wd)


RT_E1, RT_E2, RT_G1, RT_G2, RT_R1, RT_R2 = range(6)
SEG_ALIGN = 16
EXPERT_CHUNKS = (1024, 1024, 1024, 512)


def _router_body(hn_ref, wr_ref, route_ref, route_t_ref, cnt_ref):
    T = hn_ref.shape[0]
    logits = _dot(hn_ref[...], wr_ref[...])
    lane = lax.broadcasted_iota(jnp.int32, logits.shape, 1).astype(F32)
    lg = jnp.where(lane < N_EXPERTS, logits, -jnp.inf)
    m1 = jnp.max(lg, axis=-1, keepdims=True)
    i1 = jnp.min(jnp.where(lg == m1, lane, float(LANES)), axis=-1, keepdims=True)
    lg2 = jnp.where(lane == i1, -jnp.inf, lg)
    m2 = jnp.max(lg2, axis=-1, keepdims=True)
    i2 = jnp.min(jnp.where(lg2 == m2, lane, float(LANES)), axis=-1, keepdims=True)
    e2 = jnp.exp(m2 - m1)
    sel = jnp.where((lane == i1) | (lane == i2), 1.0, 0.0)
    r = lax.broadcasted_iota(jnp.int32, (T, T), 0)
    c = lax.broadcasted_iota(jnp.int32, (T, T), 1)
    before = jnp.where(r > c, 1.0, 0.0).astype(BF16)
    rank = _dot(before, sel.astype(BF16))
    r1 = jnp.sum(jnp.where(lane == i1, rank, 0.0), axis=-1, keepdims=True)
    r2 = jnp.sum(jnp.where(lane == i2, rank, 0.0), axis=-1, keepdims=True)
    fields = {RT_E1: i1, RT_E2: i2, RT_G1: 1.0 / (1.0 + e2), RT_G2: e2 / (1.0 + e2), RT_R1: r1, RT_R2: r2}
    route = jnp.zeros_like(logits)
    for idx, val in fields.items():
        route = jnp.where(lane == float(idx), val, route)
    route_ref[...] = route
    route_t_ref[...] = route.T
    cnt_ref[...] = jnp.broadcast_to(jnp.sum(sel, axis=0, keepdims=True), cnt_ref.shape)


def _router(hn, wr, T):
    m = hn.shape[0]
    nt = m // T
    return pl.pallas_call(
        _router_body,
        grid=(nt,),
        in_specs=[pl.BlockSpec((T, D_MODEL), lambda i: (i, 0)),
                  pl.BlockSpec((D_MODEL, LANES), lambda i: (0, 0))],
        out_specs=[pl.BlockSpec((T, LANES), lambda i: (i, 0)),
                   pl.BlockSpec((LANES, T), lambda i: (0, i)),
                   pl.BlockSpec((None, SUBLANES, LANES), lambda i: (i, 0, 0))],
        out_shape=[jax.ShapeDtypeStruct((m, LANES), F32),
                   jax.ShapeDtypeStruct((LANES, m), F32),
                   jax.ShapeDtypeStruct((nt, SUBLANES, LANES), F32)],
        compiler_params=_params("arbitrary"),
        name="router",
    )(hn, wr)


def _segment_copies(seg_ref, so_ref, off_ref, step, make, fn, max_rows):
    sizes = []
    s = max_rows
    while s >= SEG_ALIGN:
        sizes.append(s)
        s //= 2
    for e in range(N_EXPERTS):
        n = seg_ref[step * N_EXPERTS + e]
        src = so_ref[step * N_EXPERTS + e]
        dst = off_ref[step * N_EXPERTS + e]
        for s in sizes:
            @pl.when((n & s) != 0)
            def _(s=s, n=n, src=src, dst=dst):
                done = n & (-2 * s)
                fn(make(pl.multiple_of(src + done, SEG_ALIGN), pl.multiple_of(dst + done, SEG_ALIGN), s))


def _stage_offset(so_ref, step, expert):
    out = jnp.zeros_like(expert)
    for e in range(N_EXPERTS):
        out = jnp.where(expert == float(e), so_ref[step * N_EXPERTS + e].astype(F32), out)
    return out


def _dispatch_body(seg_ref, so_ref, off_ref, tail_ref, tail_src_ref, tail_off_ref, used_ref, hn_ref, rt_ref,
                   xs_ref, stage, zero, sem, zero_sem):
    t = pl.program_id(0)
    nt = pl.num_programs(0)
    T = hn_ref.shape[0]
    S = stage.shape[1]
    slot = t % 2

    @pl.when(t == 0)
    def _():
        bm = zero.shape[0]
        zero[...] = jnp.zeros_like(zero)

        def tail_copy(src, dst, rows):
            return pltpu.make_async_copy(zero.at[pl.ds(src, rows)], xs_ref.at[pl.ds(dst, rows)], zero_sem.at[0])

        def block_copy(b):
            return pltpu.make_async_copy(zero, xs_ref.at[pl.ds(pl.multiple_of(b * bm, bm), bm)], zero_sem.at[0])

        for fn in (lambda c: c.start(), lambda c: c.wait()):
            _segment_copies(tail_ref, tail_src_ref, tail_off_ref, 0, tail_copy, fn, bm // 2)

            def blocks(b, carry, fn=fn):
                fn(block_copy(b))
                return carry
            lax.fori_loop(used_ref[0], xs_ref.shape[0] // bm, blocks, 0)

    def copies(step, slot, fn):
        def make(src, dst, rows):
            return pltpu.make_async_copy(stage.at[slot, pl.ds(src, rows)], xs_ref.at[pl.ds(dst, rows)], sem.at[slot])
        _segment_copies(seg_ref, so_ref, off_ref, step, make, fn, T)

    @pl.when(t >= 2)
    def _():
        copies(t - 2, slot, lambda c: c.wait())

    rt = rt_ref[...]
    tgt1 = _stage_offset(so_ref, t, rt[RT_E1:RT_E1 + 1, :]) + rt[RT_R1:RT_R1 + 1, :]
    tgt2 = _stage_offset(so_ref, t, rt[RT_E2:RT_E2 + 1, :]) + rt[RT_R2:RT_R2 + 1, :]
    rowi = lax.broadcasted_iota(jnp.int32, (S, T), 0).astype(F32)
    sel = jnp.where((rowi == tgt1) | (rowi == tgt2), 1.0, 0.0).astype(BF16)
    stage[slot] = _dot(sel, hn_ref[...]).astype(BF16)
    copies(t, slot, lambda c: c.start())

    @pl.when(t == nt - 1)
    def _():
        copies(t, slot, lambda c: c.wait())

        @pl.when(t >= 1)
        def _():
            copies(t - 1, 1 - slot, lambda c: c.wait())


def _dispatch(hn, route_t, plan, T, S, bm, nblk):
    m = hn.shape[0]
    grid_spec = pltpu.PrefetchScalarGridSpec(
        num_scalar_prefetch=7,
        grid=(m // T,),
        in_specs=[pl.BlockSpec((T, D_MODEL), lambda t, *_: (t, 0)),
                  pl.BlockSpec((LANES, T), lambda t, *_: (0, t))],
        out_specs=pl.BlockSpec(memory_space=pl.ANY),
        scratch_shapes=[pltpu.VMEM((2, S, D_MODEL), BF16), pltpu.VMEM((bm, D_MODEL), BF16),
                        pltpu.SemaphoreType.DMA((2,)), pltpu.SemaphoreType.DMA((1,))],
    )
    return pl.pallas_call(
        _dispatch_body,
        grid_spec=grid_spec,
        out_shape=jax.ShapeDtypeStruct((nblk * bm, D_MODEL), BF16),
        compiler_params=_params("arbitrary"),
        name="moe_dispatch",
    )(plan["seg"], plan["so"], plan["off"], plan["tail"], plan["tail_src"], plan["tail_off"], plan["used"],
      hn, route_t)


def _experts_body(owner_ref, used_ref, x_ref, wg_ref, wu_ref, wd_ref, y_ref, acc_ref):
    del owner_ref

    @pl.when(pl.program_id(0) < used_ref[0])
    def _():
        x = x_ref[...]
        off = 0
        for i, width in enumerate(EXPERT_CHUNKS):
            g = _dot(x, wg_ref[:, off:off + width])
            u = _dot(x, wu_ref[:, off:off + width])
            y = _dot((g * _sigmoid(g) * u).astype(BF16), wd_ref[off:off + width, :])
            if i == 0:
                acc_ref[...] = y
            else:
                acc_ref[...] += y
            off += width
        y_ref[...] = acc_ref[...].astype(y_ref.dtype)

    @pl.when(pl.program_id(0) >= used_ref[0])
    def _():
        y_ref[...] = jnp.zeros_like(y_ref)


def _experts(xs, plan, wg, wu, wd, bm):
    nblk = xs.shape[0] // bm
    buffers = 1 if bm >= 256 else 2

    def wspec(shape):
        return pl.BlockSpec((None,) + shape, lambda b, owner, used: (owner[b], 0, 0),
                            pipeline_mode=pl.Buffered(buffers))

    grid_spec = pltpu.PrefetchScalarGridSpec(
        num_scalar_prefetch=2,
        grid=(nblk,),
        in_specs=[pl.BlockSpec((bm, D_MODEL), lambda b, owner, used: (jnp.minimum(b, used[0] - 1), 0)),
                  wspec((D_MODEL, D_FF_EXPERT)), wspec((D_MODEL, D_FF_EXPERT)), wspec((D_FF_EXPERT, D_MODEL))],
        out_specs=pl.BlockSpec((bm, D_MODEL), lambda b, owner, used: (b, 0)),
        scratch_shapes=[pltpu.VMEM((bm, D_MODEL), F32)],
    )
    return pl.pallas_call(
        _experts_body,
        grid_spec=grid_spec,
        out_shape=jax.ShapeDtypeStruct(xs.shape, BF16),
        compiler_params=_params("arbitrary"),
        name="moe_experts",
    )(plan["owner"], plan["used"], xs, wg, wu, wd)


def _combine_body(seg_ref, so_ref, off_ref, route_ref, x1_ref, nw_ref, ys_ref, o_ref, ybuf, sem, *, final_norm):
    t = pl.program_id(0)
    nt = pl.num_programs(0)
    T = route_ref.shape[0]
    S = ybuf.shape[1]
    slot = t % 2

    def copies(step, slot, fn):
        def make(stage_row, grouped_row, rows):
            return pltpu.make_async_copy(ys_ref.at[pl.ds(grouped_row, rows)], ybuf.at[slot, pl.ds(stage_row, rows)],
                                         sem.at[slot])
        _segment_copies(seg_ref, so_ref, off_ref, step, make, fn, T)

    @pl.when(t == 0)
    def _():
        ybuf[...] = jnp.zeros_like(ybuf)
        copies(0, 0, lambda c: c.start())

    @pl.when(t + 1 < nt)
    def _():
        copies(t + 1, 1 - slot, lambda c: c.start())

    copies(t, slot, lambda c: c.wait())

    route = route_ref[...]
    tgt1 = _stage_offset(so_ref, t, route[:, RT_E1:RT_E1 + 1]) + route[:, RT_R1:RT_R1 + 1]
    tgt2 = _stage_offset(so_ref, t, route[:, RT_E2:RT_E2 + 1]) + route[:, RT_R2:RT_R2 + 1]
    coli = lax.broadcasted_iota(jnp.int32, (T, S), 1).astype(F32)
    sel = (jnp.where(coli == tgt1, route[:, RT_G1:RT_G1 + 1], 0.0)
           + jnp.where(coli == tgt2, route[:, RT_G2:RT_G2 + 1], 0.0))
    x2 = x1_ref[...] + _dot(sel.astype(BF16), ybuf[slot])
    o_ref[...] = _rms(x2, nw_ref[...]) if final_norm else x2


def _combine(ys, route, x1, nw, plan, T, S, final_norm):
    m = x1.shape[0]
    grid_spec = pltpu.PrefetchScalarGridSpec(
        num_scalar_prefetch=3,
        grid=(m // T,),
        in_specs=[pl.BlockSpec((T, LANES), lambda t, *_: (t, 0)),
                  pl.BlockSpec((T, D_MODEL), lambda t, *_: (t, 0)),
                  pl.BlockSpec((1, D_MODEL), lambda t, *_: (0, 0)),
                  pl.BlockSpec(memory_space=pl.ANY)],
        out_specs=pl.BlockSpec((T, D_MODEL), lambda t, *_: (t, 0)),
        scratch_shapes=[pltpu.VMEM((2, S, D_MODEL), BF16), pltpu.SemaphoreType.DMA((2,))],
    )
    return pl.pallas_call(
        functools.partial(_combine_body, final_norm=final_norm),
        grid_spec=grid_spec,
        out_shape=jax.ShapeDtypeStruct((m, D_MODEL), F32),
        compiler_params=_params("arbitrary"),
        name="moe_combine",
    )(plan["seg"], plan["so"], plan["off"], route, x1, nw, ys)


def _moe_plan(cnt, bm, nblk):
    seg = (cnt + (SEG_ALIGN - 1)) // SEG_ALIGN * SEG_ALIGN
    so = jnp.cumsum(seg, axis=1) - seg
    rows = jnp.sum(seg, axis=0)
    blocks = (rows + (bm - 1)) // bm
    blk_end = jnp.cumsum(blocks)
    start = (blk_end - blocks) * bm
    off = start[None, :] + jnp.cumsum(seg, axis=0) - seg
    owner = jnp.minimum(jnp.sum(jnp.arange(nblk)[:, None] >= blk_end[None, :], axis=1), N_EXPERTS - 1)
    as_i32 = lambda a: a.astype(jnp.int32).reshape(-1)
    return {"seg": as_i32(seg), "so": as_i32(so), "off": as_i32(off), "owner": as_i32(owner),
            "used": as_i32(blk_end[-1:]),
            "tail": as_i32(blocks * bm - rows), "tail_src": as_i32(jnp.zeros_like(rows)), "tail_off": as_i32(start + rows)}


def _moe_ffn(hn, x1, nw, wr, wg, wu, wd, T, bm, final_norm):
    m = hn.shape[0]
    tiles = m // T
    pad_rows = tiles * N_EXPERTS * (SEG_ALIGN - 1)
    stage_rows = -(-(2 * T + N_EXPERTS * (SEG_ALIGN - 1)) // LANES) * LANES
    nblk = -(-(2 * m + pad_rows) // bm) + N_EXPERTS
    route, route_t, cnt = _router(hn, wr, T)
    plan = _moe_plan(cnt[:, 0, :N_EXPERTS].astype(jnp.int32), bm, nblk)
    xs = _dispatch(hn.astype(BF16), route_t, plan, T, stage_rows, bm, nblk)
    ys = _experts(xs, plan, wg, wu, wd, bm)
    return _combine(ys, route, x1, nw, plan, T, stage_rows, final_norm)


def _norm_body(x_ref, nw_ref, o_ref):
    o_ref[...] = _rms(x_ref[...], nw_ref[...])


def _norm(x, nw, tm):
    m = x.shape[0]
    spec = pl.BlockSpec((tm, D_MODEL), lambda i: (i, 0))
    return pl.pallas_call(
        _norm_body, grid=(m // tm,),
        in_specs=[spec, pl.BlockSpec((1, D_MODEL), lambda i: (0, 0))],
        out_specs=spec, out_shape=jax.ShapeDtypeStruct((m, D_MODEL), F32),
        compiler_params=_params("arbitrary"), name="norm",
    )(x, nw)


def _layout_w_in(w):
    o = 0
    hg = w[:, o:o + 4 * HG_W]; o += 4 * HG_W
    sq = w[:, o:o + SWQ_W]; o += SWQ_W
    sk = w[:, o:o + KV_W]; o += KV_W
    sv = w[:, o:o + KV_W]; o += KV_W
    ga = w[:, o:o + D_MODEL]; o += D_MODEL
    gb = w[:, o:o + D_MODEL]
    pairs = [sq[:, h * SW_HD:(h + 1) * SW_HD] for j in range(SW_PAIRS) for h in (j, j + SW_GROUP)]
    return jnp.concatenate([hg, ga, gb] + pairs + [sk, sv], axis=1)


def _layout_w_branch_b(w):
    return jnp.concatenate([w[h * SW_HD:(h + 1) * SW_HD] for j in range(SW_PAIRS) for h in (j, j + SW_GROUP)], axis=0)


def _rope_tables(pos):
    half = SW_HD // 2
    inv = ROPE_THETA ** (-jnp.arange(half, dtype=F32) / half)
    ang = pos.astype(F32)[:, None] * inv[None, :]
    cos = jnp.cos(ang)
    sin = jnp.sin(ang)
    reps = LANES // SW_HD
    return jnp.tile(cos, (1, 2 * reps)), jnp.tile(jnp.concatenate([-sin, sin], axis=1), (1, reps))


def _rotate_half_matrix():
    j = jnp.arange(LANES)
    src = jnp.where((j % SW_HD) < SW_HD // 2, j + SW_HD // 2, j - SW_HD // 2)
    return (jnp.arange(LANES)[:, None] == src[None, :]).astype(F32)


def _tile(m, pref):
    return pref if m % pref == 0 else m


def kernel(x_prompt, x_sample, state_hgrn, cache_swa_k, cache_swa_v, norm_mix, w_in, hg_lb_logits, hg_norm,
           swa_sinks, w_branch_a, w_branch_b, w_out, norm_ffn, w_gate_dense, w_up_dense, w_down_dense,
           w_router, w_gate_moe, w_up_moe, w_down_moe, norm_final):
    depth = w_in.shape[0]
    bsz, seq, _ = x_prompt.shape
    nsamp = x_sample.shape[0]
    mp = bsz * seq
    wb = cache_swa_k.shape[2]

    cos_p, sin_p = _rope_tables(jnp.arange(seq))
    cos_s, sin_s = _rope_tables(PAST_LEN + jnp.arange(1))
    perm32 = _rotate_half_matrix()
    perm = perm32.astype(BF16)
    lbl = hg_lb_logits.astype(F32)
    state32 = state_hgrn.astype(F32)
    kcache = cache_swa_k.reshape(depth, nsamp, wb, KV_W).astype(F32)
    vcache = cache_swa_v.reshape(depth, nsamp, wb, KV_W).astype(F32)

    xp = x_prompt.reshape(mp, D_MODEL)
    xs = x_sample.reshape(nsamp, D_MODEL)
    outs = {k: [] for k in ("sp", "kp", "vp", "ss", "ks", "vs")}
    nfin = norm_final.reshape(1, D_MODEL)
    normed = False

    for l in range(depth):
        w_in32 = _layout_w_in(w_in[l])
        wa32 = w_branch_a[l]
        wbb32 = _layout_w_branch_b(w_branch_b[l])
        wo32 = w_out[l]
        w_in_l = w_in32.astype(BF16)
        wa = wa32.astype(BF16)
        wbb = wbb32.astype(BF16)
        wo = wo32.astype(BF16)
        nmix = norm_mix[l].reshape(1, D_MODEL)
        nffn = norm_ffn[l].reshape(1, D_MODEL)
        hgn = hg_norm[l].reshape(1, HG_D)
        sinks = swa_sinks[l].astype(F32)

        zp = _proj(xp, nmix, w_in_l, _tile(mp, ROW_TILE))
        zp3 = zp.reshape(bsz, seq, Z_WIDTH)
        dense = (w_gate_dense, w_up_dense, w_down_dense) if l % 2 == 0 else ()
        moe_next = (w_gate_moe, w_up_moe, w_down_moe) if (l + 1 < depth and l % 2 == 0) else ()
        views = ([a[l // 2].reshape(D_MODEL, D_FF_DENSE) for a in dense]
                 + [a[(l + 1) // 2].reshape(-1, a.shape[-1]) for a in moe_next])
        oa_p, s_p, cast = _hgrn_prompt(zp3, lbl, hgn, l, tuple(views))
        if dense:
            dense_bf16 = tuple(c.reshape(a.shape[1:]) for c, a in zip(cast, dense))
        if moe_next:
            moe_bf16 = tuple(c.reshape(a.shape[1:]) for c, a in zip(cast[len(dense):], moe_next))
        ob_p, k_p, v_p = _swa_prompt(zp3, sinks, cos_p, sin_p, perm)
        oa_p = oa_p.reshape(mp, HG_W)
        ob_p = ob_p.reshape(mp, SWQ_W)

        zs = _proj(xs, nmix, w_in32, nsamp)
        oa_s, s_s = _hgrn_decode(zs, state32, lbl, hgn, l)
        q2 = zs[:, N_HG_COLS + R_SWQ:N_HG_COLS + R_SWQ + SWQ_W].reshape(nsamp * SW_PAIRS, LANES)
        sink_row = jnp.pad(sinks, (0, LANES - SW_Q_HEADS)).reshape(1, LANES)
        ob_s, k_s, v_s = _swa_decode(q2, zs, kcache, vcache, l, cos_s, sin_s, perm32, sink_row)
        x1s, hns = _merge(xs, oa_s, ob_s.reshape(nsamp, SWQ_W), zs, wa32, wbb32, wo32, nffn, nsamp)

        j = l // 2
        if l % 2 == 0:
            xp = _merge_ffn(xp, oa_p, ob_p, zp, wa, wbb, wo, nffn, *dense_bf16, _tile(mp, ROW_TILE))
            xs = _ffn_stream(hns, x1s, w_gate_dense[j], w_up_dense[j], w_down_dense[j], 256)
        else:
            x1p, hnp = _merge(xp, oa_p, ob_p, zp, wa, wbb, wo, nffn, _tile(mp, MERGE_TILE))
            wr32 = jnp.pad(w_router[j], ((0, 0), (0, LANES - N_EXPERTS)))
            wg, wu, wd = moe_bf16
            last = l == depth - 1
            xp = _moe_ffn(hnp, x1p, nfin, wr32.astype(BF16), wg, wu, wd, _tile(mp, ROW_TILE), MOE_BLOCK, last)
            xs = _moe_ffn(hns, x1s, nfin, wr32, wg, wu, wd, nsamp, DEC_MOE_BLOCK, last)
            normed = last

        outs["sp"].append(s_p)
        outs["kp"].append(k_p.reshape(bsz, WINDOW, SW_KV_HEADS, SW_HD))
        outs["vp"].append(v_p.reshape(bsz, WINDOW, SW_KV_HEADS, SW_HD))
        outs["ss"].append(s_s.astype(state_hgrn.dtype))
        outs["ks"].append(k_s.reshape(nsamp, wb, SW_KV_HEADS, SW_HD))
        outs["vs"].append(v_s.reshape(nsamp, wb, SW_KV_HEADS, SW_HD))

    yp, ys = (xp, xs) if normed else (_norm(xp, nfin, _tile(mp, NORM_TILE)), _norm(xs, nfin, nsamp))

    return (yp.reshape(x_prompt.shape), ys.reshape(x_sample.shape),
            jnp.stack(outs["sp"]), jnp.stack(outs["kp"]), jnp.stack(outs["vp"]),
            jnp.stack(outs["ss"]), jnp.stack(outs["ks"]), jnp.stack(outs["vs"]))
```

```python
import functools

import jax
import jax.numpy as jnp
from jax import lax
from jax.experimental import pallas as pl
from jax.experimental.pallas import tpu as pltpu

F32 = jnp.float32
BF16 = jnp.bfloat16

D_MODEL = 1024
PAST_LEN = 16384
HG_HEADS = 4
HG_D = 128
HG_W = HG_HEADS * HG_D
SW_Q_HEADS = 8
SW_KV_HEADS = 2
SW_HD = 64
SW_GROUP = SW_Q_HEADS // SW_KV_HEADS
WINDOW = 128
ROPE_THETA = 10000.0
D_FF_DENSE = 2816
N_EXPERTS = 8
D_FF_EXPERT = 3584
RMS_EPS = 1e-6

LANES = 128
SUBLANES = 8
VMEM_LIMIT = 56 * 1024 * 1024

SWQ_W = SW_Q_HEADS * SW_HD
SW_PAIRS = SWQ_W // LANES
KV_W = SW_KV_HEADS * SW_HD
Z_WIDTH = 4 * HG_W + 2 * D_MODEL + SWQ_W + 2 * KV_W
ZB_Q, ZB_F, ZB_I, ZB_G = 0, 1, 2, 3
N_HG_COLS = 4 * HG_W
R_GA, R_GB, R_SWQ, R_K, R_V = 0, D_MODEL, 2 * D_MODEL, 2 * D_MODEL + SWQ_W, 2 * D_MODEL + SWQ_W + KV_W
PROJ_CHUNKS = (1024, 1024, 1024, 1024, 768)

HG_CHUNK = 128
SW_UNROLL = 4
SEQ_BLOCK = 1024
DEC_TILE = 8
ROW_TILE = 512
MERGE_TILE = 1024
NORM_TILE = 1024
MOE_BLOCK = 512
DEC_MOE_BLOCK = 128


def _dot_dims(a, b, dims):
    precision = lax.Precision.HIGHEST if a.dtype == F32 else None
    return lax.dot_general(a, b, (dims, ((), ())), precision=precision, preferred_element_type=F32)


def _dot(a, b):
    return _dot_dims(a, b, ((1,), (0,)))


def _dot_nt(a, b):
    return _dot_dims(a, b, ((1,), (1,)))


def _dot_tn(a, b):
    return _dot_dims(a, b, ((0,), (0,)))


def _sigmoid(x):
    return 1.0 / (1.0 + jnp.exp(-x))


def _rms(x, w):
    ms = jnp.mean(x * x, axis=-1, keepdims=True)
    return x * lax.rsqrt(ms + RMS_EPS) * w


def _params(*sem):
    return pltpu.CompilerParams(dimension_semantics=sem, vmem_limit_bytes=VMEM_LIMIT)


def _proj_body(x_ref, nw_ref, w_ref, z_ref):
    h = _rms(x_ref[...], nw_ref[...]).astype(w_ref.dtype)
    off = 0
    for width in PROJ_CHUNKS:
        z_ref[:, off:off + width] = _dot(h, w_ref[:, off:off + width]).astype(z_ref.dtype)
        off += width


def _proj(x2d, nw, w, tm):
    m = x2d.shape[0]
    n = w.shape[1]
    return pl.pallas_call(
        _proj_body,
        grid=(m // tm,),
        in_specs=[
            pl.BlockSpec((tm, D_MODEL), lambda i: (i, 0)),
            pl.BlockSpec((1, D_MODEL), lambda i: (0, 0)),
            pl.BlockSpec((D_MODEL, n), lambda i: (0, 0), pipeline_mode=pl.Buffered(1)),
        ],
        out_specs=pl.BlockSpec((tm, n), lambda i: (i, 0)),
        out_shape=jax.ShapeDtypeStruct((m, n), w.dtype),
        compiler_params=_params("arbitrary"),
        name="proj",
    )(x2d, nw, w)


def _lower_bound(lbl, layer):
    mx = jnp.max(lbl, axis=0, keepdims=True)
    e = jnp.exp(lbl - mx)
    sm = e / jnp.sum(e, axis=0, keepdims=True)
    cum = sm[0:1, :]
    for i in range(1, layer + 1):
        cum = cum + sm[i:i + 1, :]
    return cum - sm[0:1, :]


def _split3(x):
    hi = x.astype(BF16)
    r = x - hi.astype(F32)
    mid = r.astype(BF16)
    lo = (r - mid.astype(F32)).astype(BF16)
    return hi, mid, lo


def _hgrn_tile(zh_ref, lb, nw, o_ref, st_scr, b_all_scr):
    C = HG_CHUNK
    col_q, col_f, col_i, col_g = (slice(b * HG_W, (b + 1) * HG_W) for b in (ZB_Q, ZB_F, ZB_I, ZB_G))
    row = lax.broadcasted_iota(jnp.int32, (C, C), 0)
    col = lax.broadcasted_iota(jnp.int32, (C, C), 1)
    xr = row ^ col
    tri = jnp.where(row >= col, 1.0, 0.0).astype(BF16)
    sub4 = (lax.broadcasted_iota(jnp.int32, (SUBLANES, HG_D), 0) & 4) == 0
    levels = (1, 2, 4, 8, 16, 32, 64)
    pair_level = {m: (xr >= m) & (xr < 2 * m) for m in levels}
    upper_half = {m: (row & m) != 0 for m in levels if m < SUBLANES}

    def chunk(rows, b_scr):
        hf = zh_ref[rows, col_f].astype(F32)
        fg_all = lb + (1.0 - lb) * _sigmoid(hf)
        g_all = jnp.log2(fg_all)
        k_all = 1.0 - fg_all
        g1, g2, g3 = _split3(g_all)
        b_all = _dot(tri, g1) + _dot(tri, g2) + _dot(tri, g3)
        b_scr[...] = b_all
        hq = zh_ref[rows, col_q].astype(F32)
        q_all = hq * _sigmoid(hq) * (HG_D ** -0.5)
        gate = zh_ref[rows, col_g].astype(F32)
        gate_all = gate * _sigmoid(gate)
        for h in range(HG_HEADS):
            sl = slice(h * HG_D, (h + 1) * HG_D)
            head(h, rows, b_scr, b_all[:, sl], q_all[:, sl], k_all[:, sl], fg_all[:, sl], gate_all[:, sl])

    def head(h, rows, b_scr, b, q, k, f, gate):
        sl = slice(h * HG_D, (h + 1) * HG_D)
        v = zh_ref[rows, col_i.start + h * HG_D:col_i.start + (h + 1) * HG_D]

        acc = _dot_nt(q.astype(BF16), k.astype(BF16))
        for m in levels:
            if m == 1:
                w = jnp.where(upper_half[m], q * f, k)
            elif m < SUBLANES:
                if m == 2:
                    pieces = []
                    for j in range(C // SUBLANES):
                        lo = jnp.broadcast_to(b_scr[SUBLANES * j + 1:SUBLANES * j + 2, sl], (SUBLANES, HG_D))
                        hi = jnp.broadcast_to(b_scr[SUBLANES * j + 5:SUBLANES * j + 6, sl], (SUBLANES, HG_D))
                        pieces.append(jnp.where(sub4, lo, hi))
                else:
                    pieces = [jnp.broadcast_to(b_scr[i * 2 * m + m - 1:i * 2 * m + m, sl], (2 * m, HG_D))
                              for i in range(C // (2 * m))]
                d = b - jnp.concatenate(pieces, axis=0)
                w = jnp.where(upper_half[m], q, k) * jnp.exp2(jnp.where(upper_half[m], d, -d))
            else:
                expo, qk = [], []
                for i in range(C // (2 * m)):
                    lo, mid, hi = i * 2 * m, i * 2 * m + m, (i + 1) * 2 * m
                    bref = b_scr[mid - 1:mid, sl]
                    expo += [bref - b[lo:mid], b[mid:hi] - bref]
                    qk += [k[lo:mid], q[mid:hi]]
                w = jnp.concatenate(qk, axis=0) * jnp.exp2(jnp.concatenate(expo, axis=0))
            wb = w.astype(BF16)
            acc = jnp.where(pair_level[m], _dot_nt(wb, wb), acc)
        a = jnp.where(row >= col, acc, 0.0)

        st = st_scr[h]
        o = _dot(a.astype(BF16), v) + _dot_nt((q * jnp.exp2(b)).astype(BF16), st.astype(BF16))
        b_last = b[C - 1:C, :]
        kd = k * jnp.exp2(b_last - b)
        st_scr[h] = jnp.exp2(b_last) * st + _dot_tn(v, kd.astype(BF16))

        o_ref[rows, sl] = (_rms(o, nw) * gate).astype(o_ref.dtype)

    for u in range(zh_ref.shape[0] // C):
        chunk(slice(u * C, (u + 1) * C), b_all_scr.at[u])


def _hgrn_prompt_body(zh_ref, lbl_ref, nw_ref, *refs, layer, n_cast):
    cast_in = refs[:n_cast]
    o_ref, s_ref = refs[n_cast:n_cast + 2]
    cast_out = refs[n_cast + 2:2 * n_cast + 2]
    st_scr, b_all_scr = refs[2 * n_cast + 2:]
    c = pl.program_id(1)

    @pl.when(c == 0)
    def _():
        st_scr[...] = jnp.zeros_like(st_scr)

    for src, dst in zip(cast_in, cast_out):
        dst[...] = src[...].astype(dst.dtype)

    _hgrn_tile(zh_ref, _lower_bound(lbl_ref[...], layer), nw_ref[...], o_ref, st_scr, b_all_scr)

    @pl.when(c == pl.num_programs(1) - 1)
    def _():
        for h in range(HG_HEADS):
            s_ref[h] = st_scr[h].T


def _hgrn_prompt(z3, lbl, nw, layer, cast=()):
    bsz, seq, _ = z3.shape
    rows = _tile(seq, SEQ_BLOCK)
    steps_per_seq = seq // rows
    steps = bsz * steps_per_seq

    def slab(a):
        assert a.shape[0] % (steps * 2 * SUBLANES) == 0, a.shape
        return pl.BlockSpec((a.shape[0] // steps, a.shape[1]), lambda b, c: (b * steps_per_seq + c, 0))

    outs = pl.pallas_call(
        functools.partial(_hgrn_prompt_body, layer=layer, n_cast=len(cast)),
        grid=(bsz, steps_per_seq),
        in_specs=[pl.BlockSpec((None, rows, N_HG_COLS), lambda b, c: (b, c, 0)),
                  pl.BlockSpec(lbl.shape, lambda b, c: (0, 0)),
                  pl.BlockSpec((1, HG_D), lambda b, c: (0, 0))] + [slab(a) for a in cast],
        out_specs=[pl.BlockSpec((None, rows, HG_W), lambda b, c: (b, c, 0)),
                   pl.BlockSpec((None, HG_HEADS, HG_D, HG_D), lambda b, c: (b, 0, 0, 0))] + [slab(a) for a in cast],
        out_shape=[jax.ShapeDtypeStruct((bsz, seq, HG_W), BF16),
                   jax.ShapeDtypeStruct((bsz, HG_HEADS, HG_D, HG_D), F32)]
        + [jax.ShapeDtypeStruct(a.shape, BF16) for a in cast],
        scratch_shapes=[pltpu.VMEM((HG_HEADS, HG_D, HG_D), F32),
                        pltpu.VMEM((rows // HG_CHUNK, HG_CHUNK, HG_W), F32)],
        compiler_params=_params("arbitrary", "arbitrary"),
        name="hgrn_prompt",
    )(z3, lbl, nw, *cast)
    return outs[0], outs[1], outs[2:]


def _hgrn_decode_body(zq_ref, zf_ref, zi_ref, zg_ref, lbl_ref, nw_ref, s_ref, o_ref, sn_ref, o_scr, *, layer):
    lb = _lower_bound(lbl_ref[...], layer)
    fg = lb + (1.0 - lb) * _sigmoid(zf_ref[...].astype(F32))
    kk = 1.0 - fg
    hq = zq_ref[...].astype(F32)
    q = hq * _sigmoid(hq) * (HG_D ** -0.5)
    v = zi_ref[...].astype(F32)
    eye = (lax.broadcasted_iota(jnp.int32, (HG_D, HG_D), 0)
           == lax.broadcasted_iota(jnp.int32, (HG_D, HG_D), 1))

    def column(r):
        return jnp.sum(jnp.where(eye, jnp.broadcast_to(r, (HG_D, HG_D)), 0.0), axis=1, keepdims=True)

    for i in range(DEC_TILE):
        for h in range(HG_HEADS):
            sl = slice(h * HG_D, (h + 1) * HG_D)
            sn = column(fg[i:i + 1, sl]) * s_ref[i, h] + column(kk[i:i + 1, sl]) * v[i:i + 1, sl]
            sn_ref[i, h] = sn
            o_scr[i:i + 1, sl] = jnp.sum(column(q[i:i + 1, sl]) * sn, axis=0, keepdims=True)

    gate = zg_ref[...].astype(F32)
    gate = gate * _sigmoid(gate)
    nw = nw_ref[...]
    for h in range(HG_HEADS):
        sl = slice(h * HG_D, (h + 1) * HG_D)
        o_ref[:, sl] = (_rms(o_scr[:, sl], nw) * gate[:, sl]).astype(o_ref.dtype)


def _hgrn_decode(z2, state, lbl, nw, layer):
    n = z2.shape[0]
    T = DEC_TILE

    def zspec(blk):
        return pl.BlockSpec((T, HG_W), lambda i: (i, blk))

    sblock = (T, HG_HEADS, HG_D, HG_D)
    return pl.pallas_call(
        functools.partial(_hgrn_decode_body, layer=layer),
        grid=(n // T,),
        in_specs=[zspec(ZB_Q), zspec(ZB_F), zspec(ZB_I), zspec(ZB_G),
                  pl.BlockSpec(lbl.shape, lambda i: (0, 0)),
                  pl.BlockSpec((1, HG_D), lambda i: (0, 0)),
                  pl.BlockSpec((None,) + sblock, lambda i: (layer, i, 0, 0, 0))],
        out_specs=[pl.BlockSpec((T, HG_W), lambda i: (i, 0)), pl.BlockSpec(sblock, lambda i: (i, 0, 0, 0))],
        out_shape=[jax.ShapeDtypeStruct((n, HG_W), z2.dtype),
                   jax.ShapeDtypeStruct(state.shape[1:], F32)],
        scratch_shapes=[pltpu.VMEM((T, HG_W), F32)],
        compiler_params=_params("arbitrary"),
        name="hgrn_decode",
    )(z2, z2, z2, z2, lbl, nw, state)


def _rope(x, cos, sin, perm):
    return x.astype(F32) * cos + _dot(x, perm) * sin


def _swa_prompt_body(sinks_ref, zq_ref, zk_ref, zv_ref, cos_ref, sin_ref, perm_ref,
                     o_ref, kr_ref, vr_ref, kprev, vprev):
    n = pl.program_id(1)
    W = WINDOW

    @pl.when(n == 0)
    def _():
        kprev[...] = jnp.zeros_like(kprev)
        vprev[...] = jnp.zeros_like(vprev)

    perm = perm_ref[...]
    qi = lax.broadcasted_iota(jnp.int32, (W, 2 * W), 0)
    kj = lax.broadcasted_iota(jnp.int32, (W, 2 * W), 1)
    band = (kj >= qi) & (kj <= qi + W)
    low_half = lax.broadcasted_iota(jnp.int32, (W, LANES), 1) < SW_HD
    heads = range(SW_Q_HEADS)
    scale = SW_HD ** -0.5

    def blocks(i, carry):
        for u in range(SW_UNROLL):
            block(i * SW_UNROLL + u)
        return carry

    def block(i):
        rows = pl.ds(pl.multiple_of(i * W, W), W)
        cos = cos_ref[rows, :]
        sin = sin_ref[rows, :]
        k_rot = _rope(zk_ref[rows, :], cos, sin, perm)
        kr_ref[...] = k_rot
        vr_ref[...] = zv_ref[rows, :].astype(F32)
        k_cur = k_rot.astype(BF16)
        v_cur = zv_ref[rows, :]
        kk = jnp.concatenate([kprev[...], k_cur], axis=0)
        vv = jnp.concatenate([vprev[...], v_cur], axis=0)
        valid = band & (((n > 0) | (i > 0)) | (kj >= W))

        q_in = [zq_ref[rows, j * LANES:(j + 1) * LANES] for j in range(SW_PAIRS)]
        rot = _dot(jnp.concatenate(q_in, axis=0), perm)
        pair = [(q_in[j].astype(F32) * cos + rot[j * W:(j + 1) * W] * sin) * scale for j in range(SW_PAIRS)]
        q_rot = [jnp.where(low_half if h < SW_PAIRS else ~low_half, pair[h % SW_PAIRS], 0.0).astype(BF16)
                 for h in heads]
        s_all = _dot_nt(jnp.concatenate(q_rot, axis=0), kk)
        p_all, den_all = [], []
        for h in heads:
            s = jnp.where(valid, s_all[h * W:(h + 1) * W], -jnp.inf)
            sink = sinks_ref[h]
            mx = jnp.maximum(jnp.max(s, axis=-1, keepdims=True), sink)
            p = jnp.exp(s - mx)
            den_all.append(jnp.sum(p, axis=-1, keepdims=True) + jnp.exp(sink - mx))
            p_all.append(p.astype(BF16))
        o_all = _dot(jnp.concatenate(p_all, axis=0), vv)
        o_head = [o_all[h * W:(h + 1) * W] / den_all[h] for h in heads]
        for j in range(SW_PAIRS):
            o_pair = jnp.where(low_half, o_head[j], o_head[j + SW_PAIRS])
            o_ref[rows, j * LANES:(j + 1) * LANES] = o_pair.astype(o_ref.dtype)
        kprev[...] = k_cur
        vprev[...] = v_cur

    lax.fori_loop(0, zq_ref.shape[0] // (W * SW_UNROLL), blocks, 0)


def _swa_prompt(z3, sinks, cos, sin, perm):
    bsz, seq, _ = z3.shape
    W = WINDOW
    rows = _tile(seq, SEQ_BLOCK)
    grid_spec = pltpu.PrefetchScalarGridSpec(
        num_scalar_prefetch=1,
        grid=(bsz, seq // rows),
        in_specs=[pl.BlockSpec((None, rows, SWQ_W), lambda b, n, s: (b, n, (N_HG_COLS + R_SWQ) // SWQ_W)),
                  pl.BlockSpec((None, rows, KV_W), lambda b, n, s: (b, n, (N_HG_COLS + R_K) // KV_W)),
                  pl.BlockSpec((None, rows, KV_W), lambda b, n, s: (b, n, (N_HG_COLS + R_V) // KV_W)),
                  pl.BlockSpec((rows, LANES), lambda b, n, s: (n, 0)),
                  pl.BlockSpec((rows, LANES), lambda b, n, s: (n, 0)),
                  pl.BlockSpec((LANES, LANES), lambda b, n, s: (0, 0))],
        out_specs=[pl.BlockSpec((None, rows, SWQ_W), lambda b, n, s: (b, n, 0)),
                   pl.BlockSpec((None, W, KV_W), lambda b, n, s: (b, 0, 0)),
                   pl.BlockSpec((None, W, KV_W), lambda b, n, s: (b, 0, 0))],
        scratch_shapes=[pltpu.VMEM((W, KV_W), BF16), pltpu.VMEM((W, KV_W), BF16)],
    )
    return pl.pallas_call(
        _swa_prompt_body,
        grid_spec=grid_spec,
        out_shape=[jax.ShapeDtypeStruct((bsz, seq, SWQ_W), BF16),
                   jax.ShapeDtypeStruct((bsz, W, KV_W), F32),
                   jax.ShapeDtypeStruct((bsz, W, KV_W), F32)],
        compiler_params=_params("arbitrary", "arbitrary"),
        name="swa_prompt",
    )(sinks, z3, z3, z3, cos, sin, perm)


def _swa_decode_body(q_ref, zk_ref, zv_ref, kc_ref, vc_ref, cos_ref, sin_ref, perm_ref, sink_ref,
                     o_ref, kn_ref, vn_ref):
    cos = cos_ref[...]
    sin = sin_ref[...]
    perm = perm_ref[...]
    k_new = _rope(zk_ref[...], cos, sin, perm)
    v_new = zv_ref[...].astype(F32)
    sink = sink_ref[...]
    W = kc_ref.shape[1]
    for i in range(DEC_TILE):
        kn_ref[i, 0:W - 1, :] = kc_ref[i, 1:W, :]
        vn_ref[i, 0:W - 1, :] = vc_ref[i, 1:W, :]
        kn_ref[i, W - 1:W, :] = k_new[i:i + 1, :]
        vn_ref[i, W - 1:W, :] = v_new[i:i + 1, :]
    heads = range(SW_Q_HEADS)
    lane_row = lax.broadcasted_iota(jnp.int32, (1, LANES), 1)
    lane = lax.broadcasted_iota(jnp.int32, (W, LANES), 1)
    low_half = lane_row < SW_HD
    q_all = _rope(q_ref[...], cos, sin, perm) * (SW_HD ** -0.5)
    for i in range(DEC_TILE):
        keys = kc_ref[i]
        vals = vc_ref[i]
        s = jnp.zeros((W, LANES), F32)
        s_new = jnp.zeros((1, LANES), F32)
        q_heads = []
        for h in heads:
            r = i * SW_PAIRS + h % SW_PAIRS
            q_heads.append(jnp.where(low_half if h < SW_PAIRS else ~low_half, q_all[r:r + 1, :], 0.0))
        for h in heads:
            qh = q_heads[h]
            s = jnp.where(lane == h, jnp.sum(keys * qh, axis=-1, keepdims=True), s)
            s_new = jnp.where(lane_row == h, jnp.sum(k_new[i:i + 1, :] * qh, axis=-1, keepdims=True), s_new)
        mx = jnp.maximum(jnp.maximum(jnp.max(s, axis=0, keepdims=True), s_new), sink)
        p = jnp.exp(s - mx)
        p_new = jnp.exp(s_new - mx)
        den = jnp.sum(p, axis=0, keepdims=True) + p_new + jnp.exp(sink - mx)
        w_new = p_new / den
        wgt = p / den
        o_head = [jnp.sum(wgt[:, h:h + 1] * vals, axis=0, keepdims=True) + w_new[:, h:h + 1] * v_new[i:i + 1, :]
                  for h in heads]
        for j in range(SW_PAIRS):
            r = i * SW_PAIRS + j
            o_ref[r:r + 1, :] = jnp.where(low_half, o_head[j], o_head[j + SW_PAIRS]).astype(o_ref.dtype)


def _swa_decode(q2, z2, kc, vc, layer, cos, sin, perm, sink_row):
    n = z2.shape[0]
    T = DEC_TILE
    W = kc.shape[2]
    cache = pl.BlockSpec((None, T, W, KV_W), lambda i: (layer, i, 0, 0))
    return pl.pallas_call(
        _swa_decode_body,
        grid=(n // T,),
        in_specs=[pl.BlockSpec((T * SW_PAIRS, LANES), lambda i: (i, 0)),
                  pl.BlockSpec((T, KV_W), lambda i: (i, (N_HG_COLS + R_K) // KV_W)),
                  pl.BlockSpec((T, KV_W), lambda i: (i, (N_HG_COLS + R_V) // KV_W)),
                  cache, cache,
                  pl.BlockSpec((1, LANES), lambda i: (0, 0)),
                  pl.BlockSpec((1, LANES), lambda i: (0, 0)),
                  pl.BlockSpec((LANES, LANES), lambda i: (0, 0)),
                  pl.BlockSpec((1, LANES), lambda i: (0, 0))],
        out_specs=[pl.BlockSpec((T * SW_PAIRS, LANES), lambda i: (i, 0)),
                   pl.BlockSpec((T, W, KV_W), lambda i: (i, 0, 0)),
                   pl.BlockSpec((T, W, KV_W), lambda i: (i, 0, 0))],
        out_shape=[jax.ShapeDtypeStruct((n * SW_PAIRS, LANES), q2.dtype),
                   jax.ShapeDtypeStruct((n, W, KV_W), F32),
                   jax.ShapeDtypeStruct((n, W, KV_W), F32)],
        compiler_params=_params("arbitrary"),
        name="swa_decode",
    )(q2, z2, z2, kc, vc, cos, sin, perm, sink_row)


def _merge_value(x_ref, oa_ref, ob_ref, ga_ref, gb_ref, wa_ref, wb_ref, wo_ref):
    ta = _dot(oa_ref[...], wa_ref[...])
    tb = _dot(ob_ref[...], wb_ref[...])
    mg = _sigmoid(ga_ref[...].astype(F32)) * ta + _sigmoid(gb_ref[...].astype(F32)) * tb
    return x_ref[...] + _dot(mg.astype(wo_ref.dtype), wo_ref[...])


def _merge_body(x_ref, oa_ref, ob_ref, ga_ref, gb_ref, wa_ref, wb_ref, wo_ref, nw_ref, x1_ref, hn_ref):
    x1 = _merge_value(x_ref, oa_ref, ob_ref, ga_ref, gb_ref, wa_ref, wb_ref, wo_ref)
    x1_ref[...] = x1
    hn_ref[...] = _rms(x1, nw_ref[...]).astype(hn_ref.dtype)


def _resident(shape):
    return pl.BlockSpec(shape, lambda i: (0,) * len(shape), pipeline_mode=pl.Buffered(1))


def _merge_in_specs(tm, wa, wb, wo):
    return [pl.BlockSpec((tm, D_MODEL), lambda i: (i, 0)),
            pl.BlockSpec((tm, HG_W), lambda i: (i, 0)),
            pl.BlockSpec((tm, SWQ_W), lambda i: (i, 0)),
            pl.BlockSpec((tm, D_MODEL), lambda i: (i, (N_HG_COLS + R_GA) // D_MODEL)),
            pl.BlockSpec((tm, D_MODEL), lambda i: (i, (N_HG_COLS + R_GB) // D_MODEL)),
            _resident(wa.shape), _resident(wb.shape), _resident(wo.shape), _resident((1, D_MODEL))]


def _merge(x2d, oa, ob, z2, wa, wb, wo, nw, tm):
    m = x2d.shape[0]
    return pl.pallas_call(
        _merge_body,
        grid=(m // tm,),
        in_specs=_merge_in_specs(tm, wa, wb, wo),
        out_specs=[pl.BlockSpec((tm, D_MODEL), lambda i: (i, 0)),
                   pl.BlockSpec((tm, D_MODEL), lambda i: (i, 0))],
        out_shape=[jax.ShapeDtypeStruct((m, D_MODEL), F32),
                   jax.ShapeDtypeStruct((m, D_MODEL), wo.dtype)],
        compiler_params=_params("arbitrary"),
        name="merge",
    )(x2d, oa, ob, z2, z2, wa, wb, wo, nw)


FFN_CHUNKS = (768, 768, 768, 512)


def _merge_ffn_body(x_ref, oa_ref, ob_ref, ga_ref, gb_ref, wa_ref, wb_ref, wo_ref, nw_ref,
                    wg_ref, wu_ref, wd_ref, o_ref, acc_ref):
    x1 = _merge_value(x_ref, oa_ref, ob_ref, ga_ref, gb_ref, wa_ref, wb_ref, wo_ref)
    acc_ref[...] = x1
    hn = _rms(x1, nw_ref[...]).astype(wg_ref.dtype)
    off = 0
    for width in FFN_CHUNKS:
        g = _dot(hn, wg_ref[:, off:off + width])
        u = _dot(hn, wu_ref[:, off:off + width])
        acc_ref[...] += _dot((g * _sigmoid(g) * u).astype(wd_ref.dtype), wd_ref[off:off + width, :])
        off += width
    o_ref[...] = acc_ref[...]


def _merge_ffn(x2d, oa, ob, z2, wa, wb, wo, nw, wg, wu, wd, tm):
    m = x2d.shape[0]
    return pl.pallas_call(
        _merge_ffn_body,
        grid=(m // tm,),
        in_specs=(_merge_in_specs(tm, wa, wb, wo)
                  + [_resident(wg.shape), _resident(wu.shape), _resident(wd.shape)]),
        out_specs=pl.BlockSpec((tm, D_MODEL), lambda i: (i, 0)),
        out_shape=jax.ShapeDtypeStruct((m, D_MODEL), F32),
        scratch_shapes=[pltpu.VMEM((tm, D_MODEL), F32)],
        compiler_params=_params("arbitrary"),
        name="merge_ffn",
    )(x2d, oa, ob, z2, z2, wa, wb, wo, nw, wg, wu, wd)


def _ffn_stream_body(hn_ref, x1_ref, wg_ref, wu_ref, wd_ref, o_ref):
    @pl.when(pl.program_id(0) == 0)
    def _():
        o_ref[...] = x1_ref[...]

    hn = hn_ref[...]
    g = _dot(hn, wg_ref[...])
    u = _dot(hn, wu_ref[...])
    o_ref[...] += _dot((g * _sigmoid(g) * u).astype(wd_ref.dtype), wd_ref[...])


def _ffn_stream(hn, x1, wg, wu, wd, tf):
    m = hn.shape[0]
    full = pl.BlockSpec((m, D_MODEL), lambda f: (0, 0))
    return pl.pallas_call(
        _ffn_stream_body,
        grid=(wg.shape[1] // tf,),
        in_specs=[full, full,
                  pl.BlockSpec((D_MODEL, tf), lambda f: (0, f)),
                  pl.BlockSpec((D_MODEL, tf), lambda f: (0, f)),
                  pl.BlockSpec((tf, D_MODEL), lambda f: (f, 0))],
        out_specs=full,
        out_shape=jax.ShapeDtypeStruct((m, D_MODEL), F32),
        compiler_params=_params("arbitrary"),
        name="ffn_stream",
    )(hn, x1, wg, wu, wd)


RT_E1, RT_E2, RT_G1, RT_G2, RT_R1, RT_R2 = range(6)
SEG_ALIGN = 16
EXPERT_CHUNKS = (1024, 1024, 1024, 512)


def _router_body(hn_ref, wr_ref, route_ref, route_t_ref, cnt_ref):
    T = hn_ref.shape[0]
    logits = _dot(hn_ref[...], wr_ref[...])
    lane = lax.broadcasted_iota(jnp.int32, logits.shape, 1).astype(F32)
    lg = jnp.where(lane < N_EXPERTS, logits, -jnp.inf)
    m1 = jnp.max(lg, axis=-1, keepdims=True)
    i1 = jnp.min(jnp.where(lg == m1, lane, float(LANES)), axis=-1, keepdims=True)
    lg2 = jnp.where(lane == i1, -jnp.inf, lg)
    m2 = jnp.max(lg2, axis=-1, keepdims=True)
    i2 = jnp.min(jnp.where(lg2 == m2, lane, float(LANES)), axis=-1, keepdims=True)
    e2 = jnp.exp(m2 - m1)
    sel = jnp.where((lane == i1) | (lane == i2), 1.0, 0.0)
    r = lax.broadcasted_iota(jnp.int32, (T, T), 0)
    c = lax.broadcasted_iota(jnp.int32, (T, T), 1)
    before = jnp.where(r > c, 1.0, 0.0).astype(BF16)
    rank = _dot(before, sel.astype(BF16))
    r1 = jnp.sum(jnp.where(lane == i1, rank, 0.0), axis=-1, keepdims=True)
    r2 = jnp.sum(jnp.where(lane == i2, rank, 0.0), axis=-1, keepdims=True)
    fields = {RT_E1: i1, RT_E2: i2, RT_G1: 1.0 / (1.0 + e2), RT_G2: e2 / (1.0 + e2), RT_R1: r1, RT_R2: r2}
    route = jnp.zeros_like(logits)
    for idx, val in fields.items():
        route = jnp.where(lane == float(idx), val, route)
    route_ref[...] = route
    route_t_ref[...] = route.T
    cnt_ref[...] = jnp.broadcast_to(jnp.sum(sel, axis=0, keepdims=True), cnt_ref.shape)


def _router(hn, wr, T):
    m = hn.shape[0]
    nt = m // T
    return pl.pallas_call(
        _router_body,
        grid=(nt,),
        in_specs=[pl.BlockSpec((T, D_MODEL), lambda i: (i, 0)),
                  pl.BlockSpec((D_MODEL, LANES), lambda i: (0, 0))],
        out_specs=[pl.BlockSpec((T, LANES), lambda i: (i, 0)),
                   pl.BlockSpec((LANES, T), lambda i: (0, i)),
                   pl.BlockSpec((None, SUBLANES, LANES), lambda i: (i, 0, 0))],
        out_shape=[jax.ShapeDtypeStruct((m, LANES), F32),
                   jax.ShapeDtypeStruct((LANES, m), F32),
                   jax.ShapeDtypeStruct((nt, SUBLANES, LANES), F32)],
        compiler_params=_params("arbitrary"),
        name="router",
    )(hn, wr)


def _segment_copies(seg_ref, so_ref, off_ref, step, make, fn, max_rows):
    sizes = []
    s = max_rows
    while s >= SEG_ALIGN:
        sizes.append(s)
        s //= 2
    for e in range(N_EXPERTS):
        n = seg_ref[step * N_EXPERTS + e]
        src = so_ref[step * N_EXPERTS + e]
        dst = off_ref[step * N_EXPERTS + e]
        for s in sizes:
            @pl.when((n & s) != 0)
            def _(s=s, n=n, src=src, dst=dst):
                done = n & (-2 * s)
                fn(make(pl.multiple_of(src + done, SEG_ALIGN), pl.multiple_of(dst + done, SEG_ALIGN), s))


def _stage_offset(so_ref, step, expert):
    out = jnp.zeros_like(expert)
    for e in range(N_EXPERTS):
        out = jnp.where(expert == float(e), so_ref[step * N_EXPERTS + e].astype(F32), out)
    return out


def _dispatch_body(seg_ref, so_ref, off_ref, tail_ref, tail_src_ref, tail_off_ref, used_ref, hn_ref, rt_ref,
                   xs_ref, stage, zero, sem, zero_sem):
    t = pl.program_id(0)
    nt = pl.num_programs(0)
    T = hn_ref.shape[0]
    S = stage.shape[1]
    slot = t % 2

    @pl.when(t == 0)
    def _():
        bm = zero.shape[0]
        zero[...] = jnp.zeros_like(zero)

        def tail_copy(src, dst, rows):
            return pltpu.make_async_copy(zero.at[pl.ds(src, rows)], xs_ref.at[pl.ds(dst, rows)], zero_sem.at[0])

        def block_copy(b):
            return pltpu.make_async_copy(zero, xs_ref.at[pl.ds(pl.multiple_of(b * bm, bm), bm)], zero_sem.at[0])

        for fn in (lambda c: c.start(), lambda c: c.wait()):
            _segment_copies(tail_ref, tail_src_ref, tail_off_ref, 0, tail_copy, fn, bm // 2)

            def blocks(b, carry, fn=fn):
                fn(block_copy(b))
                return carry
            lax.fori_loop(used_ref[0], xs_ref.shape[0] // bm, blocks, 0)

    def copies(step, slot, fn):
        def make(src, dst, rows):
            return pltpu.make_async_copy(stage.at[slot, pl.ds(src, rows)], xs_ref.at[pl.ds(dst, rows)], sem.at[slot])
        _segment_copies(seg_ref, so_ref, off_ref, step, make, fn, T)

    @pl.when(t >= 2)
    def _():
        copies(t - 2, slot, lambda c: c.wait())

    rt = rt_ref[...]
    tgt1 = _stage_offset(so_ref, t, rt[RT_E1:RT_E1 + 1, :]) + rt[RT_R1:RT_R1 + 1, :]
    tgt2 = _stage_offset(so_ref, t, rt[RT_E2:RT_E2 + 1, :]) + rt[RT_R2:RT_R2 + 1, :]
    rowi = lax.broadcasted_iota(jnp.int32, (S, T), 0).astype(F32)
    sel = jnp.where((rowi == tgt1) | (rowi == tgt2), 1.0, 0.0).astype(BF16)
    stage[slot] = _dot(sel, hn_ref[...]).astype(BF16)
    copies(t, slot, lambda c: c.start())

    @pl.when(t == nt - 1)
    def _():
        copies(t, slot, lambda c: c.wait())

        @pl.when(t >= 1)
        def _():
            copies(t - 1, 1 - slot, lambda c: c.wait())


def _dispatch(hn, route_t, plan, T, S, bm, nblk):
    m = hn.shape[0]
    grid_spec = pltpu.PrefetchScalarGridSpec(
        num_scalar_prefetch=7,
        grid=(m // T,),
        in_specs=[pl.BlockSpec((T, D_MODEL), lambda t, *_: (t, 0)),
                  pl.BlockSpec((LANES, T), lambda t, *_: (0, t))],
        out_specs=pl.BlockSpec(memory_space=pl.ANY),
        scratch_shapes=[pltpu.VMEM((2, S, D_MODEL), BF16), pltpu.VMEM((bm, D_MODEL), BF16),
                        pltpu.SemaphoreType.DMA((2,)), pltpu.SemaphoreType.DMA((1,))],
    )
    return pl.pallas_call(
        _dispatch_body,
        grid_spec=grid_spec,
        out_shape=jax.ShapeDtypeStruct((nblk * bm, D_MODEL), BF16),
        compiler_params=_params("arbitrary"),
        name="moe_dispatch",
    )(plan["seg"], plan["so"], plan["off"], plan["tail"], plan["tail_src"], plan["tail_off"], plan["used"],
      hn, route_t)


def _experts_body(owner_ref, used_ref, x_ref, wg_ref, wu_ref, wd_ref, y_ref, acc_ref):
    del owner_ref

    @pl.when(pl.program_id(0) < used_ref[0])
    def _():
        x = x_ref[...]
        off = 0
        for i, width in enumerate(EXPERT_CHUNKS):
            g = _dot(x, wg_ref[:, off:off + width])
            u = _dot(x, wu_ref[:, off:off + width])
            y = _dot((g * _sigmoid(g) * u).astype(BF16), wd_ref[off:off + width, :])
            if i == 0:
                acc_ref[...] = y
            else:
                acc_ref[...] += y
            off += width
        y_ref[...] = acc_ref[...].astype(y_ref.dtype)

    @pl.when(pl.program_id(0) >= used_ref[0])
    def _():
        y_ref[...] = jnp.zeros_like(y_ref)


def _experts(xs, plan, wg, wu, wd, bm):
    nblk = xs.shape[0] // bm
    buffers = 1 if bm >= 256 else 2

    def wspec(shape):
        return pl.BlockSpec((None,) + shape, lambda b, owner, used: (owner[b], 0, 0),
                            pipeline_mode=pl.Buffered(buffers))

    grid_spec = pltpu.PrefetchScalarGridSpec(
        num_scalar_prefetch=2,
        grid=(nblk,),
        in_specs=[pl.BlockSpec((bm, D_MODEL), lambda b, owner, used: (jnp.minimum(b, used[0] - 1), 0)),
                  wspec((D_MODEL, D_FF_EXPERT)), wspec((D_MODEL, D_FF_EXPERT)), wspec((D_FF_EXPERT, D_MODEL))],
        out_specs=pl.BlockSpec((bm, D_MODEL), lambda b, owner, used: (b, 0)),
        scratch_shapes=[pltpu.VMEM((bm, D_MODEL), F32)],
    )
    return pl.pallas_call(
        _experts_body,
        grid_spec=grid_spec,
        out_shape=jax.ShapeDtypeStruct(xs.shape, BF16),
        compiler_params=_params("arbitrary"),
        name="moe_experts",
    )(plan["owner"], plan["used"], xs, wg, wu, wd)


def _combine_body(seg_ref, so_ref, off_ref, route_ref, x1_ref, nw_ref, ys_ref, o_ref, ybuf, sem, *, final_norm):
    t = pl.program_id(0)
    nt = pl.num_programs(0)
    T = route_ref.shape[0]
    S = ybuf.shape[1]
    slot = t % 2

    def copies(step, slot, fn):
        def make(stage_row, grouped_row, rows):
            return pltpu.make_async_copy(ys_ref.at[pl.ds(grouped_row, rows)], ybuf.at[slot, pl.ds(stage_row, rows)],
                                         sem.at[slot])
        _segment_copies(seg_ref, so_ref, off_ref, step, make, fn, T)

    @pl.when(t == 0)
    def _():
        ybuf[...] = jnp.zeros_like(ybuf)
        copies(0, 0, lambda c: c.start())

    @pl.when(t + 1 < nt)
    def _():
        copies(t + 1, 1 - slot, lambda c: c.start())

    copies(t, slot, lambda c: c.wait())

    route = route_ref[...]
    tgt1 = _stage_offset(so_ref, t, route[:, RT_E1:RT_E1 + 1]) + route[:, RT_R1:RT_R1 + 1]
    tgt2 = _stage_offset(so_ref, t, route[:, RT_E2:RT_E2 + 1]) + route[:, RT_R2:RT_R2 + 1]
    coli = lax.broadcasted_iota(jnp.int32, (T, S), 1).astype(F32)
    sel = (jnp.where(coli == tgt1, route[:, RT_G1:RT_G1 + 1], 0.0)
           + jnp.where(coli == tgt2, route[:, RT_G2:RT_G2 + 1], 0.0))
    x2 = x1_ref[...] + _dot(sel.astype(BF16), ybuf[slot])
    o_ref[...] = _rms(x2, nw_ref[...]) if final_norm else x2


def _combine(ys, route, x1, nw, plan, T, S, final_norm):
    m = x1.shape[0]
    grid_spec = pltpu.PrefetchScalarGridSpec(
        num_scalar_prefetch=3,
        grid=(m // T,),
        in_specs=[pl.BlockSpec((T, LANES), lambda t, *_: (t, 0)),
                  pl.BlockSpec((T, D_MODEL), lambda t, *_: (t, 0)),
                  pl.BlockSpec((1, D_MODEL), lambda t, *_: (0, 0)),
                  pl.BlockSpec(memory_space=pl.ANY)],
        out_specs=pl.BlockSpec((T, D_MODEL), lambda t, *_: (t, 0)),
        scratch_shapes=[pltpu.VMEM((2, S, D_MODEL), BF16), pltpu.SemaphoreType.DMA((2,))],
    )
    return pl.pallas_call(
        functools.partial(_combine_body, final_norm=final_norm),
        grid_spec=grid_spec,
        out_shape=jax.ShapeDtypeStruct((m, D_MODEL), F32),
        compiler_params=_params("arbitrary"),
        name="moe_combine",
    )(plan["seg"], plan["so"], plan["off"], route, x1, nw, ys)


def _moe_plan(cnt, bm, nblk):
    seg = (cnt + (SEG_ALIGN - 1)) // SEG_ALIGN * SEG_ALIGN
    so = jnp.cumsum(seg, axis=1) - seg
    rows = jnp.sum(seg, axis=0)
    blocks = (rows + (bm - 1)) // bm
    blk_end = jnp.cumsum(blocks)
    start = (blk_end - blocks) * bm
    off = start[None, :] + jnp.cumsum(seg, axis=0) - seg
    owner = jnp.minimum(jnp.sum(jnp.arange(nblk)[:, None] >= blk_end[None, :], axis=1), N_EXPERTS - 1)
    as_i32 = lambda a: a.astype(jnp.int32).reshape(-1)
    return {"seg": as_i32(seg), "so": as_i32(so), "off": as_i32(off), "owner": as_i32(owner),
            "used": as_i32(blk_end[-1:]),
            "tail": as_i32(blocks * bm - rows), "tail_src": as_i32(jnp.zeros_like(rows)), "tail_off": as_i32(start + rows)}


def _moe_ffn(hn, x1, nw, wr, wg, wu, wd, T, bm, final_norm):
    m = hn.shape[0]
    tiles = m // T
    pad_rows = tiles * N_EXPERTS * (SEG_ALIGN - 1)
    stage_rows = -(-(2 * T + N_EXPERTS * (SEG_ALIGN - 1)) // LANES) * LANES
    nblk = -(-(2 * m + pad_rows) // bm) + N_EXPERTS
    route, route_t, cnt = _router(hn, wr, T)
    plan = _moe_plan(cnt[:, 0, :N_EXPERTS].astype(jnp.int32), bm, nblk)
    xs = _dispatch(hn.astype(BF16), route_t, plan, T, stage_rows, bm, nblk)
    ys = _experts(xs, plan, wg, wu, wd, bm)
    return _combine(ys, route, x1, nw, plan, T, stage_rows, final_norm)


def _norm_body(x_ref, nw_ref, o_ref):
    o_ref[...] = _rms(x_ref[...], nw_ref[...])


def _norm(x, nw, tm):
    m = x.shape[0]
    spec = pl.BlockSpec((tm, D_MODEL), lambda i: (i, 0))
    return pl.pallas_call(
        _norm_body, grid=(m // tm,),
        in_specs=[spec, pl.BlockSpec((1, D_MODEL), lambda i: (0, 0))],
        out_specs=spec, out_shape=jax.ShapeDtypeStruct((m, D_MODEL), F32),
        compiler_params=_params("arbitrary"), name="norm",
    )(x, nw)


def _layout_w_in(w):
    o = 0
    hg = w[:, o:o + 4 * HG_W]; o += 4 * HG_W
    sq = w[:, o:o + SWQ_W]; o += SWQ_W
    sk = w[:, o:o + KV_W]; o += KV_W
    sv = w[:, o:o + KV_W]; o += KV_W
    ga = w[:, o:o + D_MODEL]; o += D_MODEL
    gb = w[:, o:o + D_MODEL]
    pairs = [sq[:, h * SW_HD:(h + 1) * SW_HD] for j in range(SW_PAIRS) for h in (j, j + SW_GROUP)]
    return jnp.concatenate([hg, ga, gb] + pairs + [sk, sv], axis=1)


def _layout_w_branch_b(w):
    return jnp.concatenate([w[h * SW_HD:(h + 1) * SW_HD] for j in range(SW_PAIRS) for h in (j, j + SW_GROUP)], axis=0)


def _rope_tables(pos):
    half = SW_HD // 2
    inv = ROPE_THETA ** (-jnp.arange(half, dtype=F32) / half)
    ang = pos.astype(F32)[:, None] * inv[None, :]
    cos = jnp.cos(ang)
    sin = jnp.sin(ang)
    reps = LANES // SW_HD
    return jnp.tile(cos, (1, 2 * reps)), jnp.tile(jnp.concatenate([-sin, sin], axis=1), (1, reps))


def _rotate_half_matrix():
    j = jnp.arange(LANES)
    src = jnp.where((j % SW_HD) < SW_HD // 2, j + SW_HD // 2, j - SW_HD // 2)
    return (jnp.arange(LANES)[:, None] == src[None, :]).astype(F32)


def _tile(m, pref):
    return pref if m % pref == 0 else m


def kernel(x_prompt, x_sample, state_hgrn, cache_swa_k, cache_swa_v, norm_mix, w_in, hg_lb_logits, hg_norm,
           swa_sinks, w_branch_a, w_branch_b, w_out, norm_ffn, w_gate_dense, w_up_dense, w_down_dense,
           w_router, w_gate_moe, w_up_moe, w_down_moe, norm_final):
    depth = w_in.shape[0]
    bsz, seq, _ = x_prompt.shape
    nsamp = x_sample.shape[0]
    mp = bsz * seq
    wb = cache_swa_k.shape[2]

    cos_p, sin_p = _rope_tables(jnp.arange(seq))
    cos_s, sin_s = _rope_tables(PAST_LEN + jnp.arange(1))
    perm32 = _rotate_half_matrix()
    perm = perm32.astype(BF16)
    lbl = hg_lb_logits.astype(F32)
    state32 = state_hgrn.astype(F32)
    kcache = cache_swa_k.reshape(depth, nsamp, wb, KV_W).astype(F32)
    vcache = cache_swa_v.reshape(depth, nsamp, wb, KV_W).astype(F32)

    xp = x_prompt.reshape(mp, D_MODEL)
    xs = x_sample.reshape(nsamp, D_MODEL)
    outs = {k: [] for k in ("sp", "kp", "vp", "ss", "ks", "vs")}
    nfin = norm_final.reshape(1, D_MODEL)
    normed = False

    for l in range(depth):
        w_in32 = _layout_w_in(w_in[l])
        wa32 = w_branch_a[l]
        wbb32 = _layout_w_branch_b(w_branch_b[l])
        wo32 = w_out[l]
        w_in_l = w_in32.astype(BF16)
        wa = wa32.astype(BF16)
        wbb = wbb32.astype(BF16)
        wo = wo32.astype(BF16)
        nmix = norm_mix[l].reshape(1, D_MODEL)
        nffn = norm_ffn[l].reshape(1, D_MODEL)
        hgn = hg_norm[l].reshape(1, HG_D)
        sinks = swa_sinks[l].astype(F32)

        zp = _proj(xp, nmix, w_in_l, _tile(mp, ROW_TILE))
        zp3 = zp.reshape(bsz, seq, Z_WIDTH)
        dense = (w_gate_dense, w_up_dense, w_down_dense) if l % 2 == 0 else ()
        moe_next = (w_gate_moe, w_up_moe, w_down_moe) if (l + 1 < depth and l % 2 == 0) else ()
        views = ([a[l // 2].reshape(D_MODEL, D_FF_DENSE) for a in dense]
                 + [a[(l + 1) // 2].reshape(-1, a.shape[-1]) for a in moe_next])
        oa_p, s_p, cast = _hgrn_prompt(zp3, lbl, hgn, l, tuple(views))
        if dense:
            dense_bf16 = tuple(c.reshape(a.shape[1:]) for c, a in zip(cast, dense))
        if moe_next:
            moe_bf16 = tuple(c.reshape(a.shape[1:]) for c, a in zip(cast[len(dense):], moe_next))
        ob_p, k_p, v_p = _swa_prompt(zp3, sinks, cos_p, sin_p, perm)
        oa_p = oa_p.reshape(mp, HG_W)
        ob_p = ob_p.reshape(mp, SWQ_W)

        zs = _proj(xs, nmix, w_in32, nsamp)
        oa_s, s_s = _hgrn_decode(zs, state32, lbl, hgn, l)
        q2 = zs[:, N_HG_COLS + R_SWQ:N_HG_COLS + R_SWQ + SWQ_W].reshape(nsamp * SW_PAIRS, LANES)
        sink_row = jnp.pad(sinks, (0, LANES - SW_Q_HEADS)).reshape(1, LANES)
        ob_s, k_s, v_s = _swa_decode(q2, zs, kcache, vcache, l, cos_s, sin_s, perm32, sink_row)
        x1s, hns = _merge(xs, oa_s, ob_s.reshape(nsamp, SWQ_W), zs, wa32, wbb32, wo32, nffn, nsamp)

        j = l // 2
        if l % 2 == 0:
            xp = _merge_ffn(xp, oa_p, ob_p, zp, wa, wbb, wo, nffn, *dense_bf16, _tile(mp, ROW_TILE))
            xs = _ffn_stream(hns, x1s, w_gate_dense[j], w_up_dense[j], w_down_dense[j], 256)
        else:
            x1p, hnp = _merge(xp, oa_p, ob_p, zp, wa, wbb, wo, nffn, _tile(mp, MERGE_TILE))
            wr32 = jnp.pad(w_router[j], ((0, 0), (0, LANES - N_EXPERTS)))
            wg, wu, wd = moe_bf16
            last = l == depth - 1
            xp = _moe_ffn(hnp, x1p, nfin, wr32.astype(BF16), wg, wu, wd, _tile(mp, ROW_TILE), MOE_BLOCK, last)
            xs = _moe_ffn(hns, x1s, nfin, wr32, wg, wu, wd, nsamp, DEC_MOE_BLOCK, last)
            normed = last

        outs["sp"].append(s_p)
        outs["kp"].append(k_p.reshape(bsz, WINDOW, SW_KV_HEADS, SW_HD))
        outs["vp"].append(v_p.reshape(bsz, WINDOW, SW_KV_HEADS, SW_HD))
        outs["ss"].append(s_s.astype(state_hgrn.dtype))
        outs["ks"].append(k_s.reshape(nsamp, wb, SW_KV_HEADS, SW_HD))
        outs["vs"].append(v_s.reshape(nsamp, wb, SW_KV_HEADS, SW_HD))

    yp, ys = (xp, xs) if normed else (_norm(xp, nfin, _tile(mp, NORM_TILE)), _norm(xs, nfin, nsamp))

    return (yp.reshape(x_prompt.shape), ys.reshape(x_sample.shape),
            jnp.stack(outs["sp"]), jnp.stack(outs["kp"]), jnp.stack(outs["vp"]),
            jnp.stack(outs["ss"]), jnp.stack(outs["ks"]), jnp.stack(outs["vs"]))
```

```python
import functools

import jax
import jax.numpy as jnp
from jax import lax
from jax.experimental import pallas as pl
from jax.experimental.pallas import tpu as pltpu

F32 = jnp.float32
BF16 = jnp.bfloat16

D_MODEL = 1024
PAST_LEN = 16384
HG_HEADS = 4
HG_D = 128
HG_W = HG_HEADS * HG_D
SW_Q_HEADS = 8
SW_KV_HEADS = 2
SW_HD = 64
SW_GROUP = SW_Q_HEADS // SW_KV_HEADS
WINDOW = 128
ROPE_THETA = 10000.0
D_FF_DENSE = 2816
N_EXPERTS = 8
D_FF_EXPERT = 3584
RMS_EPS = 1e-6

LANES = 128
SUBLANES = 8
VMEM_LIMIT = 56 * 1024 * 1024

SWQ_W = SW_Q_HEADS * SW_HD
SW_PAIRS = SWQ_W // LANES
KV_W = SW_KV_HEADS * SW_HD
Z_WIDTH = 4 * HG_W + 2 * D_MODEL + SWQ_W + 2 * KV_W
ZB_Q, ZB_F, ZB_I, ZB_G = 0, 1, 2, 3
N_HG_COLS = 4 * HG_W
R_GA, R_GB, R_SWQ, R_K, R_V = 0, D_MODEL, 2 * D_MODEL, 2 * D_MODEL + SWQ_W, 2 * D_MODEL + SWQ_W + KV_W
PROJ_CHUNKS = (1024, 1024, 1024, 1024, 768)

HG_CHUNK = 128
SW_UNROLL = 8
SEQ_BLOCK = 1024
DEC_TILE = 8
ROW_TILE = 512
MERGE_TILE = 1024
NORM_TILE = 1024
MOE_BLOCK = 512
DEC_MOE_BLOCK = 128


def _dot_dims(a, b, dims):
    precision = lax.Precision.HIGHEST if a.dtype == F32 else None
    return lax.dot_general(a, b, (dims, ((), ())), precision=precision, preferred_element_type=F32)


def _dot(a, b):
    return _dot_dims(a, b, ((1,), (0,)))


def _dot_nt(a, b):
    return _dot_dims(a, b, ((1,), (1,)))


def _dot_tn(a, b):
    return _dot_dims(a, b, ((0,), (0,)))


def _sigmoid(x):
    return 1.0 / (1.0 + jnp.exp(-x))


def _rms(x, w):
    ms = jnp.mean(x * x, axis=-1, keepdims=True)
    return x * lax.rsqrt(ms + RMS_EPS) * w


def _params(*sem):
    return pltpu.CompilerParams(dimension_semantics=sem, vmem_limit_bytes=VMEM_LIMIT)


def _proj_body(x_ref, nw_ref, w_ref, z_ref):
    h = _rms(x_ref[...], nw_ref[...]).astype(w_ref.dtype)
    off = 0
    for width in PROJ_CHUNKS:
        z_ref[:, off:off + width] = _dot(h, w_ref[:, off:off + width]).astype(z_ref.dtype)
        off += width


def _proj(x2d, nw, w, tm):
    m = x2d.shape[0]
    n = w.shape[1]
    return pl.pallas_call(
        _proj_body,
        grid=(m // tm,),
        in_specs=[
            pl.BlockSpec((tm, D_MODEL), lambda i: (i, 0)),
            pl.BlockSpec((1, D_MODEL), lambda i: (0, 0)),
            pl.BlockSpec((D_MODEL, n), lambda i: (0, 0), pipeline_mode=pl.Buffered(1)),
        ],
        out_specs=pl.BlockSpec((tm, n), lambda i: (i, 0)),
        out_shape=jax.ShapeDtypeStruct((m, n), w.dtype),
        compiler_params=_params("arbitrary"),
        name="proj",
    )(x2d, nw, w)


def _lower_bound(lbl, layer):
    mx = jnp.max(lbl, axis=0, keepdims=True)
    e = jnp.exp(lbl - mx)
    sm = e / jnp.sum(e, axis=0, keepdims=True)
    cum = sm[0:1, :]
    for i in range(1, layer + 1):
        cum = cum + sm[i:i + 1, :]
    return cum - sm[0:1, :]


def _split3(x):
    hi = x.astype(BF16)
    r = x - hi.astype(F32)
    mid = r.astype(BF16)
    lo = (r - mid.astype(F32)).astype(BF16)
    return hi, mid, lo


def _hgrn_tile(zh_ref, lb, nw, o_ref, st_scr, b_all_scr):
    C = HG_CHUNK
    col_q, col_f, col_i, col_g = (slice(b * HG_W, (b + 1) * HG_W) for b in (ZB_Q, ZB_F, ZB_I, ZB_G))
    row = lax.broadcasted_iota(jnp.int32, (C, C), 0)
    col = lax.broadcasted_iota(jnp.int32, (C, C), 1)
    xr = row ^ col
    tri = jnp.where(row >= col, 1.0, 0.0).astype(BF16)
    sub4 = (lax.broadcasted_iota(jnp.int32, (SUBLANES, HG_D), 0) & 4) == 0
    levels = (1, 2, 4, 8, 16, 32, 64)
    pair_level = {m: (xr >= m) & (xr < 2 * m) for m in levels}
    upper_half = {m: (row & m) != 0 for m in levels if m < SUBLANES}

    def chunk(rows, b_scr):
        hf = zh_ref[rows, col_f].astype(F32)
        fg_all = lb + (1.0 - lb) * _sigmoid(hf)
        g_all = jnp.log2(fg_all)
        k_all = 1.0 - fg_all
        g1, g2, g3 = _split3(g_all)
        b_all = _dot(tri, g1) + _dot(tri, g2) + _dot(tri, g3)
        b_scr[...] = b_all
        hq = zh_ref[rows, col_q].astype(F32)
        q_all = hq * _sigmoid(hq) * (HG_D ** -0.5)
        gate = zh_ref[rows, col_g].astype(F32)
        gate_all = gate * _sigmoid(gate)
        for h in range(HG_HEADS):
            sl = slice(h * HG_D, (h + 1) * HG_D)
            head(h, rows, b_scr, b_all[:, sl], q_all[:, sl], k_all[:, sl], fg_all[:, sl], gate_all[:, sl])

    def head(h, rows, b_scr, b, q, k, f, gate):
        sl = slice(h * HG_D, (h + 1) * HG_D)
        v = zh_ref[rows, col_i.start + h * HG_D:col_i.start + (h + 1) * HG_D]

        acc = _dot_nt(q.astype(BF16), k.astype(BF16))
        for m in levels:
            if m == 1:
                w = jnp.where(upper_half[m], q * f, k)
            elif m < SUBLANES:
                if m == 2:
                    pieces = []
                    for j in range(C // SUBLANES):
                        lo = jnp.broadcast_to(b_scr[SUBLANES * j + 1:SUBLANES * j + 2, sl], (SUBLANES, HG_D))
                        hi = jnp.broadcast_to(b_scr[SUBLANES * j + 5:SUBLANES * j + 6, sl], (SUBLANES, HG_D))
                        pieces.append(jnp.where(sub4, lo, hi))
                else:
                    pieces = [jnp.broadcast_to(b_scr[i * 2 * m + m - 1:i * 2 * m + m, sl], (2 * m, HG_D))
                              for i in range(C // (2 * m))]
                d = b - jnp.concatenate(pieces, axis=0)
                w = jnp.where(upper_half[m], q, k) * jnp.exp2(jnp.where(upper_half[m], d, -d))
            else:
                expo, qk = [], []
                for i in range(C // (2 * m)):
                    lo, mid, hi = i * 2 * m, i * 2 * m + m, (i + 1) * 2 * m
                    bref = b_scr[mid - 1:mid, sl]
                    expo += [bref - b[lo:mid], b[mid:hi] - bref]
                    qk += [k[lo:mid], q[mid:hi]]
                w = jnp.concatenate(qk, axis=0) * jnp.exp2(jnp.concatenate(expo, axis=0))
            wb = w.astype(BF16)
            acc = jnp.where(pair_level[m], _dot_nt(wb, wb), acc)
        a = jnp.where(row >= col, acc, 0.0)

        st = st_scr[h]
        o = _dot(a.astype(BF16), v) + _dot_nt((q * jnp.exp2(b)).astype(BF16), st.astype(BF16))
        b_last = b[C - 1:C, :]
        kd = k * jnp.exp2(b_last - b)
        st_scr[h] = jnp.exp2(b_last) * st + _dot_tn(v, kd.astype(BF16))

        o_ref[rows, sl] = (_rms(o, nw) * gate).astype(o_ref.dtype)

    for u in range(zh_ref.shape[0] // C):
        chunk(slice(u * C, (u + 1) * C), b_all_scr.at[u])


def _hgrn_prompt_body(zh_ref, lbl_ref, nw_ref, *refs, layer, n_cast):
    cast_in = refs[:n_cast]
    o_ref, s_ref = refs[n_cast:n_cast + 2]
    cast_out = refs[n_cast + 2:2 * n_cast + 2]
    st_scr, b_all_scr = refs[2 * n_cast + 2:]
    c = pl.program_id(1)

    @pl.when(c == 0)
    def _():
        st_scr[...] = jnp.zeros_like(st_scr)

    for src, dst in zip(cast_in, cast_out):
        dst[...] = src[...].astype(dst.dtype)

    _hgrn_tile(zh_ref, _lower_bound(lbl_ref[...], layer), nw_ref[...], o_ref, st_scr, b_all_scr)

    @pl.when(c == pl.num_programs(1) - 1)
    def _():
        for h in range(HG_HEADS):
            s_ref[h] = st_scr[h].T


def _hgrn_prompt(z3, lbl, nw, layer, cast=()):
    bsz, seq, _ = z3.shape
    rows = _tile(seq, SEQ_BLOCK)
    steps_per_seq = seq // rows
    steps = bsz * steps_per_seq

    def slab(a):
        assert a.shape[0] % (steps * 2 * SUBLANES) == 0, a.shape
        return pl.BlockSpec((a.shape[0] // steps, a.shape[1]), lambda b, c: (b * steps_per_seq + c, 0))

    outs = pl.pallas_call(
        functools.partial(_hgrn_prompt_body, layer=layer, n_cast=len(cast)),
        grid=(bsz, steps_per_seq),
        in_specs=[pl.BlockSpec((None, rows, N_HG_COLS), lambda b, c: (b, c, 0)),
                  pl.BlockSpec(lbl.shape, lambda b, c: (0, 0)),
                  pl.BlockSpec((1, HG_D), lambda b, c: (0, 0))] + [slab(a) for a in cast],
        out_specs=[pl.BlockSpec((None, rows, HG_W), lambda b, c: (b, c, 0)),
                   pl.BlockSpec((None, HG_HEADS, HG_D, HG_D), lambda b, c: (b, 0, 0, 0))] + [slab(a) for a in cast],
        out_shape=[jax.ShapeDtypeStruct((bsz, seq, HG_W), BF16),
                   jax.ShapeDtypeStruct((bsz, HG_HEADS, HG_D, HG_D), F32)]
        + [jax.ShapeDtypeStruct(a.shape, BF16) for a in cast],
        scratch_shapes=[pltpu.VMEM((HG_HEADS, HG_D, HG_D), F32),
                        pltpu.VMEM((rows // HG_CHUNK, HG_CHUNK, HG_W), F32)],
        compiler_params=_params("arbitrary", "arbitrary"),
        name="hgrn_prompt",
    )(z3, lbl, nw, *cast)
    return outs[0], outs[1], outs[2:]


def _hgrn_decode_body(zq_ref, zf_ref, zi_ref, zg_ref, lbl_ref, nw_ref, s_ref, o_ref, sn_ref, o_scr, *, layer):
    lb = _lower_bound(lbl_ref[...], layer)
    fg = lb + (1.0 - lb) * _sigmoid(zf_ref[...].astype(F32))
    kk = 1.0 - fg
    hq = zq_ref[...].astype(F32)
    q = hq * _sigmoid(hq) * (HG_D ** -0.5)
    v = zi_ref[...].astype(F32)
    eye = (lax.broadcasted_iota(jnp.int32, (HG_D, HG_D), 0)
           == lax.broadcasted_iota(jnp.int32, (HG_D, HG_D), 1))

    def column(r):
        return jnp.sum(jnp.where(eye, jnp.broadcast_to(r, (HG_D, HG_D)), 0.0), axis=1, keepdims=True)

    for i in range(DEC_TILE):
        for h in range(HG_HEADS):
            sl = slice(h * HG_D, (h + 1) * HG_D)
            sn = column(fg[i:i + 1, sl]) * s_ref[i, h] + column(kk[i:i + 1, sl]) * v[i:i + 1, sl]
            sn_ref[i, h] = sn
            o_scr[i:i + 1, sl] = jnp.sum(column(q[i:i + 1, sl]) * sn, axis=0, keepdims=True)

    gate = zg_ref[...].astype(F32)
    gate = gate * _sigmoid(gate)
    nw = nw_ref[...]
    for h in range(HG_HEADS):
        sl = slice(h * HG_D, (h + 1) * HG_D)
        o_ref[:, sl] = (_rms(o_scr[:, sl], nw) * gate[:, sl]).astype(o_ref.dtype)


def _hgrn_decode(z2, state, lbl, nw, layer):
    n = z2.shape[0]
    T = DEC_TILE

    def zspec(blk):
        return pl.BlockSpec((T, HG_W), lambda i: (i, blk))

    sblock = (T, HG_HEADS, HG_D, HG_D)
    return pl.pallas_call(
        functools.partial(_hgrn_decode_body, layer=layer),
        grid=(n // T,),
        in_specs=[zspec(ZB_Q), zspec(ZB_F), zspec(ZB_I), zspec(ZB_G),
                  pl.BlockSpec(lbl.shape, lambda i: (0, 0)),
                  pl.BlockSpec((1, HG_D), lambda i: (0, 0)),
                  pl.BlockSpec((None,) + sblock, lambda i: (layer, i, 0, 0, 0))],
        out_specs=[pl.BlockSpec((T, HG_W), lambda i: (i, 0)), pl.BlockSpec(sblock, lambda i: (i, 0, 0, 0))],
        out_shape=[jax.ShapeDtypeStruct((n, HG_W), z2.dtype),
                   jax.ShapeDtypeStruct(state.shape[1:], F32)],
        scratch_shapes=[pltpu.VMEM((T, HG_W), F32)],
        compiler_params=_params("arbitrary"),
        name="hgrn_decode",
    )(z2, z2, z2, z2, lbl, nw, state)


def _rope(x, cos, sin, perm):
    return x.astype(F32) * cos + _dot(x, perm) * sin


def _swa_prompt_body(sinks_ref, zq_ref, zk_ref, zv_ref, cos_ref, sin_ref, perm_ref,
                     o_ref, kr_ref, vr_ref, kprev, vprev):
    n = pl.program_id(1)
    W = WINDOW

    @pl.when(n == 0)
    def _():
        kprev[...] = jnp.zeros_like(kprev)
        vprev[...] = jnp.zeros_like(vprev)

    perm = perm_ref[...]
    qi = lax.broadcasted_iota(jnp.int32, (W, 2 * W), 0)
    kj = lax.broadcasted_iota(jnp.int32, (W, 2 * W), 1)
    band = (kj >= qi) & (kj <= qi + W)
    low_half = lax.broadcasted_iota(jnp.int32, (W, LANES), 1) < SW_HD
    heads = range(SW_Q_HEADS)
    scale = SW_HD ** -0.5

    def blocks(i, carry):
        for u in range(SW_UNROLL):
            block(i * SW_UNROLL + u)
        return carry

    def block(i):
        rows = pl.ds(pl.multiple_of(i * W, W), W)
        cos = cos_ref[rows, :]
        sin = sin_ref[rows, :]
        k_rot = _rope(zk_ref[rows, :], cos, sin, perm)
        kr_ref[...] = k_rot
        vr_ref[...] = zv_ref[rows, :].astype(F32)
        k_cur = k_rot.astype(BF16)
        v_cur = zv_ref[rows, :]
        kk = jnp.concatenate([kprev[...], k_cur], axis=0)
        vv = jnp.concatenate([vprev[...], v_cur], axis=0)
        valid = band & (((n > 0) | (i > 0)) | (kj >= W))

        q_in = [zq_ref[rows, j * LANES:(j + 1) * LANES] for j in range(SW_PAIRS)]
        rot = _dot(jnp.concatenate(q_in, axis=0), perm)
        pair = [(q_in[j].astype(F32) * cos + rot[j * W:(j + 1) * W] * sin) * scale for j in range(SW_PAIRS)]
        q_rot = [jnp.where(low_half if h < SW_PAIRS else ~low_half, pair[h % SW_PAIRS], 0.0).astype(BF16)
                 for h in heads]
        s_all = _dot_nt(jnp.concatenate(q_rot, axis=0), kk)
        p_all, den_all = [], []
        for h in heads:
            s = jnp.where(valid, s_all[h * W:(h + 1) * W], -jnp.inf)
            sink = sinks_ref[h]
            mx = jnp.maximum(jnp.max(s, axis=-1, keepdims=True), sink)
            p = jnp.exp(s - mx)
            den_all.append(jnp.sum(p, axis=-1, keepdims=True) + jnp.exp(sink - mx))
            p_all.append(p.astype(BF16))
        o_all = _dot(jnp.concatenate(p_all, axis=0), vv)
        o_head = [o_all[h * W:(h + 1) * W] / den_all[h] for h in heads]
        for j in range(SW_PAIRS):
            o_pair = jnp.where(low_half, o_head[j], o_head[j + SW_PAIRS])
            o_ref[rows, j * LANES:(j + 1) * LANES] = o_pair.astype(o_ref.dtype)
        kprev[...] = k_cur
        vprev[...] = v_cur

    lax.fori_loop(0, zq_ref.shape[0] // (W * SW_UNROLL), blocks, 0)


def _swa_prompt(z3, sinks, cos, sin, perm):
    bsz, seq, _ = z3.shape
    W = WINDOW
    rows = _tile(seq, SEQ_BLOCK)
    grid_spec = pltpu.PrefetchScalarGridSpec(
        num_scalar_prefetch=1,
        grid=(bsz, seq // rows),
        in_specs=[pl.BlockSpec((None, rows, SWQ_W), lambda b, n, s: (b, n, (N_HG_COLS + R_SWQ) // SWQ_W)),
                  pl.BlockSpec((None, rows, KV_W), lambda b, n, s: (b, n, (N_HG_COLS + R_K) // KV_W)),
                  pl.BlockSpec((None, rows, KV_W), lambda b, n, s: (b, n, (N_HG_COLS + R_V) // KV_W)),
                  pl.BlockSpec((rows, LANES), lambda b, n, s: (n, 0)),
                  pl.BlockSpec((rows, LANES), lambda b, n, s: (n, 0)),
                  pl.BlockSpec((LANES, LANES), lambda b, n, s: (0, 0))],
        out_specs=[pl.BlockSpec((None, rows, SWQ_W), lambda b, n, s: (b, n, 0)),
                   pl.BlockSpec((None, W, KV_W), lambda b, n, s: (b, 0, 0)),
                   pl.BlockSpec((None, W, KV_W), lambda b, n, s: (b, 0, 0))],
        scratch_shapes=[pltpu.VMEM((W, KV_W), BF16), pltpu.VMEM((W, KV_W), BF16)],
    )
    return pl.pallas_call(
        _swa_prompt_body,
        grid_spec=grid_spec,
        out_shape=[jax.ShapeDtypeStruct((bsz, seq, SWQ_W), BF16),
                   jax.ShapeDtypeStruct((bsz, W, KV_W), F32),
                   jax.ShapeDtypeStruct((bsz, W, KV_W), F32)],
        compiler_params=_params("arbitrary", "arbitrary"),
        name="swa_prompt",
    )(sinks, z3, z3, z3, cos, sin, perm)


def _swa_decode_body(q_ref, zk_ref, zv_ref, kc_ref, vc_ref, cos_ref, sin_ref, perm_ref, sink_ref,
                     o_ref, kn_ref, vn_ref):
    cos = cos_ref[...]
    sin = sin_ref[...]
    perm = perm_ref[...]
    k_new = _rope(zk_ref[...], cos, sin, perm)
    v_new = zv_ref[...].astype(F32)
    sink = sink_ref[...]
    W = kc_ref.shape[1]
    for i in range(DEC_TILE):
        kn_ref[i, 0:W - 1, :] = kc_ref[i, 1:W, :]
        vn_ref[i, 0:W - 1, :] = vc_ref[i, 1:W, :]
        kn_ref[i, W - 1:W, :] = k_new[i:i + 1, :]
        vn_ref[i, W - 1:W, :] = v_new[i:i + 1, :]
    heads = range(SW_Q_HEADS)
    lane_row = lax.broadcasted_iota(jnp.int32, (1, LANES), 1)
    lane = lax.broadcasted_iota(jnp.int32, (W, LANES), 1)
    low_half = lane_row < SW_HD
    q_all = _rope(q_ref[...], cos, sin, perm) * (SW_HD ** -0.5)
    for i in range(DEC_TILE):
        keys = kc_ref[i]
        vals = vc_ref[i]
        s = jnp.zeros((W, LANES), F32)
        s_new = jnp.zeros((1, LANES), F32)
        q_heads = []
        for h in heads:
            r = i * SW_PAIRS + h % SW_PAIRS
            q_heads.append(jnp.where(low_half if h < SW_PAIRS else ~low_half, q_all[r:r + 1, :], 0.0))
        for h in heads:
            qh = q_heads[h]
            s = jnp.where(lane == h, jnp.sum(keys * qh, axis=-1, keepdims=True), s)
            s_new = jnp.where(lane_row == h, jnp.sum(k_new[i:i + 1, :] * qh, axis=-1, keepdims=True), s_new)
        mx = jnp.maximum(jnp.maximum(jnp.max(s, axis=0, keepdims=True), s_new), sink)
        p = jnp.exp(s - mx)
        p_new = jnp.exp(s_new - mx)
        den = jnp.sum(p, axis=0, keepdims=True) + p_new + jnp.exp(sink - mx)
        w_new = p_new / den
        wgt = p / den
        o_head = [jnp.sum(wgt[:, h:h + 1] * vals, axis=0, keepdims=True) + w_new[:, h:h + 1] * v_new[i:i + 1, :]
                  for h in heads]
        for j in range(SW_PAIRS):
            r = i * SW_PAIRS + j
            o_ref[r:r + 1, :] = jnp.where(low_half, o_head[j], o_head[j + SW_PAIRS]).astype(o_ref.dtype)


def _swa_decode(q2, z2, kc, vc, layer, cos, sin, perm, sink_row):
    n = z2.shape[0]
    T = DEC_TILE
    W = kc.shape[2]
    cache = pl.BlockSpec((None, T, W, KV_W), lambda i: (layer, i, 0, 0))
    return pl.pallas_call(
        _swa_decode_body,
        grid=(n // T,),
        in_specs=[pl.BlockSpec((T * SW_PAIRS, LANES), lambda i: (i, 0)),
                  pl.BlockSpec((T, KV_W), lambda i: (i, (N_HG_COLS + R_K) // KV_W)),
                  pl.BlockSpec((T, KV_W), lambda i: (i, (N_HG_COLS + R_V) // KV_W)),
                  cache, cache,
                  pl.BlockSpec((1, LANES), lambda i: (0, 0)),
                  pl.BlockSpec((1, LANES), lambda i: (0, 0)),
                  pl.BlockSpec((LANES, LANES), lambda i: (0, 0)),
                  pl.BlockSpec((1, LANES), lambda i: (0, 0))],
        out_specs=[pl.BlockSpec((T * SW_PAIRS, LANES), lambda i: (i, 0)),
                   pl.BlockSpec((T, W, KV_W), lambda i: (i, 0, 0)),
                   pl.BlockSpec((T, W, KV_W), lambda i: (i, 0, 0))],
        out_shape=[jax.ShapeDtypeStruct((n * SW_PAIRS, LANES), q2.dtype),
                   jax.ShapeDtypeStruct((n, W, KV_W), F32),
                   jax.ShapeDtypeStruct((n, W, KV_W), F32)],
        compiler_params=_params("arbitrary"),
        name="swa_decode",
    )(q2, z2, z2, kc, vc, cos, sin, perm, sink_row)


def _merge_value(x_ref, oa_ref, ob_ref, ga_ref, gb_ref, wa_ref, wb_ref, wo_ref):
    ta = _dot(oa_ref[...], wa_ref[...])
    tb = _dot(ob_ref[...], wb_ref[...])
    mg = _sigmoid(ga_ref[...].astype(F32)) * ta + _sigmoid(gb_ref[...].astype(F32)) * tb
    return x_ref[...] + _dot(mg.astype(wo_ref.dtype), wo_ref[...])


def _merge_body(x_ref, oa_ref, ob_ref, ga_ref, gb_ref, wa_ref, wb_ref, wo_ref, nw_ref, x1_ref, hn_ref):
    x1 = _merge_value(x_ref, oa_ref, ob_ref, ga_ref, gb_ref, wa_ref, wb_ref, wo_ref)
    x1_ref[...] = x1
    hn_ref[...] = _rms(x1, nw_ref[...]).astype(hn_ref.dtype)


def _resident(shape):
    return pl.BlockSpec(shape, lambda i: (0,) * len(shape), pipeline_mode=pl.Buffered(1))


def _merge_in_specs(tm, wa, wb, wo):
    return [pl.BlockSpec((tm, D_MODEL), lambda i: (i, 0)),
            pl.BlockSpec((tm, HG_W), lambda i: (i, 0)),
            pl.BlockSpec((tm, SWQ_W), lambda i: (i, 0)),
            pl.BlockSpec((tm, D_MODEL), lambda i: (i, (N_HG_COLS + R_GA) // D_MODEL)),
            pl.BlockSpec((tm, D_MODEL), lambda i: (i, (N_HG_COLS + R_GB) // D_MODEL)),
            _resident(wa.shape), _resident(wb.shape), _resident(wo.shape), _resident((1, D_MODEL))]


def _merge(x2d, oa, ob, z2, wa, wb, wo, nw, tm):
    m = x2d.shape[0]
    return pl.pallas_call(
        _merge_body,
        grid=(m // tm,),
        in_specs=_merge_in_specs(tm, wa, wb, wo),
        out_specs=[pl.BlockSpec((tm, D_MODEL), lambda i: (i, 0)),
                   pl.BlockSpec((tm, D_MODEL), lambda i: (i, 0))],
        out_shape=[jax.ShapeDtypeStruct((m, D_MODEL), F32),
                   jax.ShapeDtypeStruct((m, D_MODEL), wo.dtype)],
        compiler_params=_params("arbitrary"),
        name="merge",
    )(x2d, oa, ob, z2, z2, wa, wb, wo, nw)


FFN_CHUNKS = (768, 768, 768, 512)


def _merge_ffn_body(x_ref, oa_ref, ob_ref, ga_ref, gb_ref, wa_ref, wb_ref, wo_ref, nw_ref,
                    wg_ref, wu_ref, wd_ref, o_ref, acc_ref):
    x1 = _merge_value(x_ref, oa_ref, ob_ref, ga_ref, gb_ref, wa_ref, wb_ref, wo_ref)
    acc_ref[...] = x1
    hn = _rms(x1, nw_ref[...]).astype(wg_ref.dtype)
    off = 0
    for width in FFN_CHUNKS:
        g = _dot(hn, wg_ref[:, off:off + width])
        u = _dot(hn, wu_ref[:, off:off + width])
        acc_ref[...] += _dot((g * _sigmoid(g) * u).astype(wd_ref.dtype), wd_ref[off:off + width, :])
        off += width
    o_ref[...] = acc_ref[...]


def _merge_ffn(x2d, oa, ob, z2, wa, wb, wo, nw, wg, wu, wd, tm):
    m = x2d.shape[0]
    return pl.pallas_call(
        _merge_ffn_body,
        grid=(m // tm,),
        in_specs=(_merge_in_specs(tm, wa, wb, wo)
                  + [_resident(wg.shape), _resident(wu.shape), _resident(wd.shape)]),
        out_specs=pl.BlockSpec((tm, D_MODEL), lambda i: (i, 0)),
        out_shape=jax.ShapeDtypeStruct((m, D_MODEL), F32),
        scratch_shapes=[pltpu.VMEM((tm, D_MODEL), F32)],
        compiler_params=_params("arbitrary"),
        name="merge_ffn",
    )(x2d, oa, ob, z2, z2, wa, wb, wo, nw, wg, wu, wd)


def _ffn_stream_body(hn_ref, x1_ref, wg_ref, wu_ref, wd_ref, o_ref):
    @pl.when(pl.program_id(0) == 0)
    def _():
        o_ref[...] = x1_ref[...]

    hn = hn_ref[...]
    g = _dot(hn, wg_ref[...])
    u = _dot(hn, wu_ref[...])
    o_ref[...] += _dot((g * _sigmoid(g) * u).astype(wd_ref.dtype), wd_ref[...])


def _ffn_stream(hn, x1, wg, wu, wd, tf):
    m = hn.shape[0]
    full = pl.BlockSpec((m, D_MODEL), lambda f: (0, 0))
    return pl.pallas_call(
        _ffn_stream_body,
        grid=(wg.shape[1] // tf,),
        in_specs=[full, full,
                  pl.BlockSpec((D_MODEL, tf), lambda f: (0, f)),
                  pl.BlockSpec((D_MODEL, tf), lambda f: (0, f)),
                  pl.BlockSpec((tf, D_MODEL), lambda f: (f, 0))],
        out_specs=full,
        out_shape=jax.ShapeDtypeStruct((m, D_MODEL), F32),
        compiler_params=_params("arbitrary"),
        name="ffn_stream",
    )(hn, x1, wg, wu, wd)


RT_E1, RT_E2, RT_G1, RT_G2, RT_R1, RT_R2 = range(6)
SEG_ALIGN = 16
EXPERT_CHUNKS = (1024, 1024, 1024, 512)


def _router_body(hn_ref, wr_ref, route_ref, route_t_ref, cnt_ref):
    T = hn_ref.shape[0]
    logits = _dot(hn_ref[...], wr_ref[...])
    lane = lax.broadcasted_iota(jnp.int32, logits.shape, 1).astype(F32)
    lg = jnp.where(lane < N_EXPERTS, logits, -jnp.inf)
    m1 = jnp.max(lg, axis=-1, keepdims=True)
    i1 = jnp.min(jnp.where(lg == m1, lane, float(LANES)), axis=-1, keepdims=True)
    lg2 = jnp.where(lane == i1, -jnp.inf, lg)
    m2 = jnp.max(lg2, axis=-1, keepdims=True)
    i2 = jnp.min(jnp.where(lg2 == m2, lane, float(LANES)), axis=-1, keepdims=True)
    e2 = jnp.exp(m2 - m1)
    sel = jnp.where((lane == i1) | (lane == i2), 1.0, 0.0)
    r = lax.broadcasted_iota(jnp.int32, (T, T), 0)
    c = lax.broadcasted_iota(jnp.int32, (T, T), 1)
    before = jnp.where(r > c, 1.0, 0.0).astype(BF16)
    rank = _dot(before, sel.astype(BF16))
    r1 = jnp.sum(jnp.where(lane == i1, rank, 0.0), axis=-1, keepdims=True)
    r2 = jnp.sum(jnp.where(lane == i2, rank, 0.0), axis=-1, keepdims=True)
    fields = {RT_E1: i1, RT_E2: i2, RT_G1: 1.0 / (1.0 + e2), RT_G2: e2 / (1.0 + e2), RT_R1: r1, RT_R2: r2}
    route = jnp.zeros_like(logits)
    for idx, val in fields.items():
        route = jnp.where(lane == float(idx), val, route)
    route_ref[...] = route
    route_t_ref[...] = route.T
    cnt_ref[...] = jnp.broadcast_to(jnp.sum(sel, axis=0, keepdims=True), cnt_ref.shape)


def _router(hn, wr, T):
    m = hn.shape[0]
    nt = m // T
    return pl.pallas_call(
        _router_body,
        grid=(nt,),
        in_specs=[pl.BlockSpec((T, D_MODEL), lambda i: (i, 0)),
                  pl.BlockSpec((D_MODEL, LANES), lambda i: (0, 0))],
        out_specs=[pl.BlockSpec((T, LANES), lambda i: (i, 0)),
                   pl.BlockSpec((LANES, T), lambda i: (0, i)),
                   pl.BlockSpec((None, SUBLANES, LANES), lambda i: (i, 0, 0))],
        out_shape=[jax.ShapeDtypeStruct((m, LANES), F32),
                   jax.ShapeDtypeStruct((LANES, m), F32),
                   jax.ShapeDtypeStruct((nt, SUBLANES, LANES), F32)],
        compiler_params=_params("arbitrary"),
        name="router",
    )(hn, wr)


def _segment_copies(seg_ref, so_ref, off_ref, step, make, fn, max_rows):
    sizes = []
    s = max_rows
    while s >= SEG_ALIGN:
        sizes.append(s)
        s //= 2
    for e in range(N_EXPERTS):
        n = seg_ref[step * N_EXPERTS + e]
        src = so_ref[step * N_EXPERTS + e]
        dst = off_ref[step * N_EXPERTS + e]
        for s in sizes:
            @pl.when((n & s) != 0)
            def _(s=s, n=n, src=src, dst=dst):
                done = n & (-2 * s)
                fn(make(pl.multiple_of(src + done, SEG_ALIGN), pl.multiple_of(dst + done, SEG_ALIGN), s))


def _stage_offset(so_ref, step, expert):
    out = jnp.zeros_like(expert)
    for e in range(N_EXPERTS):
        out = jnp.where(expert == float(e), so_ref[step * N_EXPERTS + e].astype(F32), out)
    return out


def _dispatch_body(seg_ref, so_ref, off_ref, tail_ref, tail_src_ref, tail_off_ref, used_ref, hn_ref, rt_ref,
                   xs_ref, stage, zero, sem, zero_sem):
    t = pl.program_id(0)
    nt = pl.num_programs(0)
    T = hn_ref.shape[0]
    S = stage.shape[1]
    slot = t % 2

    @pl.when(t == 0)
    def _():
        bm = zero.shape[0]
        zero[...] = jnp.zeros_like(zero)

        def tail_copy(src, dst, rows):
            return pltpu.make_async_copy(zero.at[pl.ds(src, rows)], xs_ref.at[pl.ds(dst, rows)], zero_sem.at[0])

        def block_copy(b):
            return pltpu.make_async_copy(zero, xs_ref.at[pl.ds(pl.multiple_of(b * bm, bm), bm)], zero_sem.at[0])

        for fn in (lambda c: c.start(), lambda c: c.wait()):
            _segment_copies(tail_ref, tail_src_ref, tail_off_ref, 0, tail_copy, fn, bm // 2)

            def blocks(b, carry, fn=fn):
                fn(block_copy(b))
                return carry
            lax.fori_loop(used_ref[0], xs_ref.shape[0] // bm, blocks, 0)

    def copies(step, slot, fn):
        def make(src, dst, rows):
            return pltpu.make_async_copy(stage.at[slot, pl.ds(src, rows)], xs_ref.at[pl.ds(dst, rows)], sem.at[slot])
        _segment_copies(seg_ref, so_ref, off_ref, step, make, fn, T)

    @pl.when(t >= 2)
    def _():
        copies(t - 2, slot, lambda c: c.wait())

    rt = rt_ref[...]
    tgt1 = _stage_offset(so_ref, t, rt[RT_E1:RT_E1 + 1, :]) + rt[RT_R1:RT_R1 + 1, :]
    tgt2 = _stage_offset(so_ref, t, rt[RT_E2:RT_E2 + 1, :]) + rt[RT_R2:RT_R2 + 1, :]
    rowi = lax.broadcasted_iota(jnp.int32, (S, T), 0).astype(F32)
    sel = jnp.where((rowi == tgt1) | (rowi == tgt2), 1.0, 0.0).astype(BF16)
    stage[slot] = _dot(sel, hn_ref[...]).astype(BF16)
    copies(t, slot, lambda c: c.start())

    @pl.when(t == nt - 1)
    def _():
        copies(t, slot, lambda c: c.wait())

        @pl.when(t >= 1)
        def _():
            copies(t - 1, 1 - slot, lambda c: c.wait())


def _dispatch(hn, route_t, plan, T, S, bm, nblk):
    m = hn.shape[0]
    grid_spec = pltpu.PrefetchScalarGridSpec(
        num_scalar_prefetch=7,
        grid=(m // T,),
        in_specs=[pl.BlockSpec((T, D_MODEL), lambda t, *_: (t, 0)),
                  pl.BlockSpec((LANES, T), lambda t, *_: (0, t))],
        out_specs=pl.BlockSpec(memory_space=pl.ANY),
        scratch_shapes=[pltpu.VMEM((2, S, D_MODEL), BF16), pltpu.VMEM((bm, D_MODEL), BF16),
                        pltpu.SemaphoreType.DMA((2,)), pltpu.SemaphoreType.DMA((1,))],
    )
    return pl.pallas_call(
        _dispatch_body,
        grid_spec=grid_spec,
        out_shape=jax.ShapeDtypeStruct((nblk * bm, D_MODEL), BF16),
        compiler_params=_params("arbitrary"),
        name="moe_dispatch",
    )(plan["seg"], plan["so"], plan["off"], plan["tail"], plan["tail_src"], plan["tail_off"], plan["used"],
      hn, route_t)


def _experts_body(owner_ref, used_ref, x_ref, wg_ref, wu_ref, wd_ref, y_ref, acc_ref):
    del owner_ref

    @pl.when(pl.program_id(0) < used_ref[0])
    def _():
        x = x_ref[...]
        off = 0
        for i, width in enumerate(EXPERT_CHUNKS):
            g = _dot(x, wg_ref[:, off:off + width])
            u = _dot(x, wu_ref[:, off:off + width])
            y = _dot((g * _sigmoid(g) * u).astype(BF16), wd_ref[off:off + width, :])
            if i == 0:
                acc_ref[...] = y
            else:
                acc_ref[...] += y
            off += width
        y_ref[...] = acc_ref[...].astype(y_ref.dtype)

    @pl.when(pl.program_id(0) >= used_ref[0])
    def _():
        y_ref[...] = jnp.zeros_like(y_ref)


def _experts(xs, plan, wg, wu, wd, bm):
    nblk = xs.shape[0] // bm
    buffers = 1 if bm >= 256 else 2

    def wspec(shape):
        return pl.BlockSpec((None,) + shape, lambda b, owner, used: (owner[b], 0, 0),
                            pipeline_mode=pl.Buffered(buffers))

    grid_spec = pltpu.PrefetchScalarGridSpec(
        num_scalar_prefetch=2,
        grid=(nblk,),
        in_specs=[pl.BlockSpec((bm, D_MODEL), lambda b, owner, used: (jnp.minimum(b, used[0] - 1), 0)),
                  wspec((D_MODEL, D_FF_EXPERT)), wspec((D_MODEL, D_FF_EXPERT)), wspec((D_FF_EXPERT, D_MODEL))],
        out_specs=pl.BlockSpec((bm, D_MODEL), lambda b, owner, used: (b, 0)),
        scratch_shapes=[pltpu.VMEM((bm, D_MODEL), F32)],
    )
    return pl.pallas_call(
        _experts_body,
        grid_spec=grid_spec,
        out_shape=jax.ShapeDtypeStruct(xs.shape, BF16),
        compiler_params=_params("arbitrary"),
        name="moe_experts",
    )(plan["owner"], plan["used"], xs, wg, wu, wd)


def _combine_body(seg_ref, so_ref, off_ref, route_ref, x1_ref, nw_ref, ys_ref, o_ref, ybuf, sem, *, final_norm):
    t = pl.program_id(0)
    nt = pl.num_programs(0)
    T = route_ref.shape[0]
    S = ybuf.shape[1]
    slot = t % 2

    def copies(step, slot, fn):
        def make(stage_row, grouped_row, rows):
            return pltpu.make_async_copy(ys_ref.at[pl.ds(grouped_row, rows)], ybuf.at[slot, pl.ds(stage_row, rows)],
                                         sem.at[slot])
        _segment_copies(seg_ref, so_ref, off_ref, step, make, fn, T)

    @pl.when(t == 0)
    def _():
        ybuf[...] = jnp.zeros_like(ybuf)
        copies(0, 0, lambda c: c.start())

    @pl.when(t + 1 < nt)
    def _():
        copies(t + 1, 1 - slot, lambda c: c.start())

    copies(t, slot, lambda c: c.wait())

    route = route_ref[...]
    tgt1 = _stage_offset(so_ref, t, route[:, RT_E1:RT_E1 + 1]) + route[:, RT_R1:RT_R1 + 1]
    tgt2 = _stage_offset(so_ref, t, route[:, RT_E2:RT_E2 + 1]) + route[:, RT_R2:RT_R2 + 1]
    coli = lax.broadcasted_iota(jnp.int32, (T, S), 1).astype(F32)
    sel = (jnp.where(coli == tgt1, route[:, RT_G1:RT_G1 + 1], 0.0)
           + jnp.where(coli == tgt2, route[:, RT_G2:RT_G2 + 1], 0.0))
    x2 = x1_ref[...] + _dot(sel.astype(BF16), ybuf[slot])
    o_ref[...] = _rms(x2, nw_ref[...]) if final_norm else x2


def _combine(ys, route, x1, nw, plan, T, S, final_norm):
    m = x1.shape[0]
    grid_spec = pltpu.PrefetchScalarGridSpec(
        num_scalar_prefetch=3,
        grid=(m // T,),
        in_specs=[pl.BlockSpec((T, LANES), lambda t, *_: (t, 0)),
                  pl.BlockSpec((T, D_MODEL), lambda t, *_: (t, 0)),
                  pl.BlockSpec((1, D_MODEL), lambda t, *_: (0, 0)),
                  pl.BlockSpec(memory_space=pl.ANY)],
        out_specs=pl.BlockSpec((T, D_MODEL), lambda t, *_: (t, 0)),
        scratch_shapes=[pltpu.VMEM((2, S, D_MODEL), BF16), pltpu.SemaphoreType.DMA((2,))],
    )
    return pl.pallas_call(
        functools.partial(_combine_body, final_norm=final_norm),
        grid_spec=grid_spec,
        out_shape=jax.ShapeDtypeStruct((m, D_MODEL), F32),
        compiler_params=_params("arbitrary"),
        name="moe_combine",
    )(plan["seg"], plan["so"], plan["off"], route, x1, nw, ys)


def _moe_plan(cnt, bm, nblk):
    seg = (cnt + (SEG_ALIGN - 1)) // SEG_ALIGN * SEG_ALIGN
    so = jnp.cumsum(seg, axis=1) - seg
    rows = jnp.sum(seg, axis=0)
    blocks = (rows + (bm - 1)) // bm
    blk_end = jnp.cumsum(blocks)
    start = (blk_end - blocks) * bm
    off = start[None, :] + jnp.cumsum(seg, axis=0) - seg
    owner = jnp.minimum(jnp.sum(jnp.arange(nblk)[:, None] >= blk_end[None, :], axis=1), N_EXPERTS - 1)
    as_i32 = lambda a: a.astype(jnp.int32).reshape(-1)
    return {"seg": as_i32(seg), "so": as_i32(so), "off": as_i32(off), "owner": as_i32(owner),
            "used": as_i32(blk_end[-1:]),
            "tail": as_i32(blocks * bm - rows), "tail_src": as_i32(jnp.zeros_like(rows)), "tail_off": as_i32(start + rows)}


def _moe_ffn(hn, x1, nw, wr, wg, wu, wd, T, bm, final_norm):
    m = hn.shape[0]
    tiles = m // T
    pad_rows = tiles * N_EXPERTS * (SEG_ALIGN - 1)
    stage_rows = -(-(2 * T + N_EXPERTS * (SEG_ALIGN - 1)) // LANES) * LANES
    nblk = -(-(2 * m + pad_rows) // bm) + N_EXPERTS
    route, route_t, cnt = _router(hn, wr, T)
    plan = _moe_plan(cnt[:, 0, :N_EXPERTS].astype(jnp.int32), bm, nblk)
    xs = _dispatch(hn.astype(BF16), route_t, plan, T, stage_rows, bm, nblk)
    ys = _experts(xs, plan, wg, wu, wd, bm)
    return _combine(ys, route, x1, nw, plan, T, stage_rows, final_norm)


def _norm_body(x_ref, nw_ref, o_ref):
    o_ref[...] = _rms(x_ref[...], nw_ref[...])


def _norm(x, nw, tm):
    m = x.shape[0]
    spec = pl.BlockSpec((tm, D_MODEL), lambda i: (i, 0))
    return pl.pallas_call(
        _norm_body, grid=(m // tm,),
        in_specs=[spec, pl.BlockSpec((1, D_MODEL), lambda i: (0, 0))],
        out_specs=spec, out_shape=jax.ShapeDtypeStruct((m, D_MODEL), F32),
        compiler_params=_params("arbitrary"), name="norm",
    )(x, nw)


def _layout_w_in(w):
    o = 0
    hg = w[:, o:o + 4 * HG_W]; o += 4 * HG_W
    sq = w[:, o:o + SWQ_W]; o += SWQ_W
    sk = w[:, o:o + KV_W]; o += KV_W
    sv = w[:, o:o + KV_W]; o += KV_W
    ga = w[:, o:o + D_MODEL]; o += D_MODEL
    gb = w[:, o:o + D_MODEL]
    pairs = [sq[:, h * SW_HD:(h + 1) * SW_HD] for j in range(SW_PAIRS) for h in (j, j + SW_GROUP)]
    return jnp.concatenate([hg, ga, gb] + pairs + [sk, sv], axis=1)


def _layout_w_branch_b(w):
    return jnp.concatenate([w[h * SW_HD:(h + 1) * SW_HD] for j in range(SW_PAIRS) for h in (j, j + SW_GROUP)], axis=0)


def _rope_tables(pos):
    half = SW_HD // 2
    inv = ROPE_THETA ** (-jnp.arange(half, dtype=F32) / half)
    ang = pos.astype(F32)[:, None] * inv[None, :]
    cos = jnp.cos(ang)
    sin = jnp.sin(ang)
    reps = LANES // SW_HD
    return jnp.tile(cos, (1, 2 * reps)), jnp.tile(jnp.concatenate([-sin, sin], axis=1), (1, reps))


def _rotate_half_matrix():
    j = jnp.arange(LANES)
    src = jnp.where((j % SW_HD) < SW_HD // 2, j + SW_HD // 2, j - SW_HD // 2)
    return (jnp.arange(LANES)[:, None] == src[None, :]).astype(F32)


def _tile(m, pref):
    return pref if m % pref == 0 else m


def kernel(x_prompt, x_sample, state_hgrn, cache_swa_k, cache_swa_v, norm_mix, w_in, hg_lb_logits, hg_norm,
           swa_sinks, w_branch_a, w_branch_b, w_out, norm_ffn, w_gate_dense, w_up_dense, w_down_dense,
           w_router, w_gate_moe, w_up_moe, w_down_moe, norm_final):
    depth = w_in.shape[0]
    bsz, seq, _ = x_prompt.shape
    nsamp = x_sample.shape[0]
    mp = bsz * seq
    wb = cache_swa_k.shape[2]

    cos_p, sin_p = _rope_tables(jnp.arange(seq))
    cos_s, sin_s = _rope_tables(PAST_LEN + jnp.arange(1))
    perm32 = _rotate_half_matrix()
    perm = perm32.astype(BF16)
    lbl = hg_lb_logits.astype(F32)
    state32 = state_hgrn.astype(F32)
    kcache = cache_swa_k.reshape(depth, nsamp, wb, KV_W).astype(F32)
    vcache = cache_swa_v.reshape(depth, nsamp, wb, KV_W).astype(F32)

    xp = x_prompt.reshape(mp, D_MODEL)
    xs = x_sample.reshape(nsamp, D_MODEL)
    outs = {k: [] for k in ("sp", "kp", "vp", "ss", "ks", "vs")}
    nfin = norm_final.reshape(1, D_MODEL)
    normed = False

    for l in range(depth):
        w_in32 = _layout_w_in(w_in[l])
        wa32 = w_branch_a[l]
        wbb32 = _layout_w_branch_b(w_branch_b[l])
        wo32 = w_out[l]
        w_in_l = w_in32.astype(BF16)
        wa = wa32.astype(BF16)
        wbb = wbb32.astype(BF16)
        wo = wo32.astype(BF16)
        nmix = norm_mix[l].reshape(1, D_MODEL)
        nffn = norm_ffn[l].reshape(1, D_MODEL)
        hgn = hg_norm[l].reshape(1, HG_D)
        sinks = swa_sinks[l].astype(F32)

        zp = _proj(xp, nmix, w_in_l, _tile(mp, ROW_TILE))
        zp3 = zp.reshape(bsz, seq, Z_WIDTH)
        dense = (w_gate_dense, w_up_dense, w_down_dense) if l % 2 == 0 else ()
        moe_next = (w_gate_moe, w_up_moe, w_down_moe) if (l + 1 < depth and l % 2 == 0) else ()
        views = ([a[l // 2].reshape(D_MODEL, D_FF_DENSE) for a in dense]
                 + [a[(l + 1) // 2].reshape(-1, a.shape[-1]) for a in moe_next])
        oa_p, s_p, cast = _hgrn_prompt(zp3, lbl, hgn, l, tuple(views))
        if dense:
            dense_bf16 = tuple(c.reshape(a.shape[1:]) for c, a in zip(cast, dense))
        if moe_next:
            moe_bf16 = tuple(c.reshape(a.shape[1:]) for c, a in zip(cast[len(dense):], moe_next))
        ob_p, k_p, v_p = _swa_prompt(zp3, sinks, cos_p, sin_p, perm)
        oa_p = oa_p.reshape(mp, HG_W)
        ob_p = ob_p.reshape(mp, SWQ_W)

        zs = _proj(xs, nmix, w_in32, nsamp)
        oa_s, s_s = _hgrn_decode(zs, state32, lbl, hgn, l)
        q2 = zs[:, N_HG_COLS + R_SWQ:N_HG_COLS + R_SWQ + SWQ_W].reshape(nsamp * SW_PAIRS, LANES)
        sink_row = jnp.pad(sinks, (0, LANES - SW_Q_HEADS)).reshape(1, LANES)
        ob_s, k_s, v_s = _swa_decode(q2, zs, kcache, vcache, l, cos_s, sin_s, perm32, sink_row)
        x1s, hns = _merge(xs, oa_s, ob_s.reshape(nsamp, SWQ_W), zs, wa32, wbb32, wo32, nffn, nsamp)

        j = l // 2
        if l % 2 == 0:
            xp = _merge_ffn(xp, oa_p, ob_p, zp, wa, wbb, wo, nffn, *dense_bf16, _tile(mp, ROW_TILE))
            xs = _ffn_stream(hns, x1s, w_gate_dense[j], w_up_dense[j], w_down_dense[j], 256)
        else:
            x1p, hnp = _merge(xp, oa_p, ob_p, zp, wa, wbb, wo, nffn, _tile(mp, MERGE_TILE))
            wr32 = jnp.pad(w_router[j], ((0, 0), (0, LANES - N_EXPERTS)))
            wg, wu, wd = moe_bf16
            last = l == depth - 1
            xp = _moe_ffn(hnp, x1p, nfin, wr32.astype(BF16), wg, wu, wd, _tile(mp, ROW_TILE), MOE_BLOCK, last)
            xs = _moe_ffn(hns, x1s, nfin, wr32, wg, wu, wd, nsamp, DEC_MOE_BLOCK, last)
            normed = last

        outs["sp"].append(s_p)
        outs["kp"].append(k_p.reshape(bsz, WINDOW, SW_KV_HEADS, SW_HD))
        outs["vp"].append(v_p.reshape(bsz, WINDOW, SW_KV_HEADS, SW_HD))
        outs["ss"].append(s_s.astype(state_hgrn.dtype))
        outs["ks"].append(k_s.reshape(nsamp, wb, SW_KV_HEADS, SW_HD))
        outs["vs"].append(v_s.reshape(nsamp, wb, SW_KV_HEADS, SW_HD))

    yp, ys = (xp, xs) if normed else (_norm(xp, nfin, _tile(mp, NORM_TILE)), _norm(xs, nfin, nsamp))

    return (yp.reshape(x_prompt.shape), ys.reshape(x_sample.shape),
            jnp.stack(outs["sp"]), jnp.stack(outs["kp"]), jnp.stack(outs["vp"]),
            jnp.stack(outs["ss"]), jnp.stack(outs["ks"]), jnp.stack(outs["vs"]))
```
